```python
import math
import jax
import jax.numpy as jnp
from jax import lax
import numpy as np

D_MODEL = 1024
BATCH = 16
SEQ = 256
DEPTH = 4
DEC_BATCH = 4
DEC_SEQ = 1024
PAST_LEN = 512

GRID_W = 64
POS_BASE = 10000.0
RMS_EPS = 1e-6
L2_EPS = 1e-6

RW_HEADS = 6
RW_HEAD_DIM = 64
RW_DIM = RW_HEADS * RW_HEAD_DIM
RW_DECAY_RANK = 64
RW_ICLR_RANK = 64
RW_GATE_RANK = 128
RW_COLS = 3 * RW_DIM + 2 * RW_DECAY_RANK + 2 * RW_ICLR_RANK + RW_GATE_RANK
RW_LN_EPS = 64e-5

GLA_HEADS = 4
GLA_DK = 32
GLA_DV = 64
GLA_KDIM = GLA_HEADS * GLA_DK
GLA_DIM = GLA_HEADS * GLA_DV
GLA_GATE_RANK = 16
GLA_GATE_NORM = 16.0
GLA_CHUNK = 32
GLA_COLS = 2 * GLA_KDIM + 2 * GLA_DIM + 2 * GLA_GATE_RANK

DN_HEADS = 6
DN_HEAD_DIM = 64
DN_DIM = DN_HEADS * DN_HEAD_DIM
DN_CONV = 3
DN_CHUNK = 64
DN_COLS = 4 * DN_DIM + 4 * DN_HEADS

MIX_DIM = RW_DIM + GLA_DIM + DN_DIM
PROJ_COLS = RW_COLS + GLA_COLS + DN_COLS

N_GROUPS = 4
EXPERTS_PER_GROUP = 4
TOP_K_IN_GROUP = 2
D_EXPERT = 512

kernel_name = 'hybrid_rwkv7_gla_gdn_flow_step'


def _split(x, sizes):
    cuts, acc = [], 0
    for s in sizes[:-1]:
        acc += s
        cuts.append(acc)
    return jnp.split(x, cuts, axis=-1)


def _heads(x, n_heads, head_dim):
    return x.reshape(x.shape[:-1] + (n_heads, head_dim))


def _rmsnorm(x, w):
    xf = x.astype(jnp.float32)
    y = xf * lax.rsqrt(jnp.mean(xf * xf, axis=-1, keepdims=True) + RMS_EPS)
    return (y * w.astype(jnp.float32)).astype(x.dtype)


def _l2norm(x):
    xf = x.astype(jnp.float32)
    return xf * lax.rsqrt(jnp.sum(xf * xf, axis=-1, keepdims=True) + L2_EPS)


def _gated_head_rms(o, w, z):
    y = o * lax.rsqrt(jnp.mean(o * o, axis=-1, keepdims=True) + RMS_EPS)
    return y * w.astype(jnp.float32) * jax.nn.silu(z.astype(jnp.float32))


def _to_dirs(x):
    return jnp.stack([x, jnp.flip(x, axis=1)])


def _orient(x):
    return jnp.stack([x[0], jnp.flip(x[1], axis=1)])


def _merge(y):
    return y[0] + jnp.flip(y[1], axis=1)


def _to_chunks(x, c):
    d, b, t, h = x.shape[:4]
    x = x.reshape((d, b, t // c, c, h) + x.shape[4:])
    return jnp.moveaxis(x, 4, 2)


def _from_chunks(x):
    d, b, h, n, c = x.shape[:5]
    x = jnp.moveaxis(x, 2, 4)
    return x.reshape((d, b, n * c, h) + x.shape[5:])


def _grid_pos_embed(n_tokens, dim):
    rows = n_tokens // GRID_W
    row = jnp.broadcast_to(jnp.arange(rows)[:, None], (rows, GRID_W)).reshape(-1)
    col = jnp.broadcast_to(jnp.arange(GRID_W)[None, :], (rows, GRID_W)).reshape(-1)
    quarter = dim // 4
    omega = 1.0 / (POS_BASE ** (jnp.arange(quarter, dtype=jnp.float32) / quarter))

    def axis_embed(pos):
        ang = pos.astype(jnp.float32)[:, None] * omega[None, :]
        return jnp.concatenate([jnp.sin(ang), jnp.cos(ang)], axis=-1)

    return jnp.concatenate([axis_embed(row), axis_embed(col)], axis=-1)


def _dwconv_centred(x, w):
    ch, kw = x.shape[-1], w.shape[0]
    return lax.conv_general_dilated(x, w[:, None, :].astype(x.dtype), window_strides=(1,),
                                    padding=((kw // 2, kw // 2),),
                                    dimension_numbers=('NWC', 'WIO', 'NWC'),
                                    feature_group_count=ch)


def _rwkv7_scan(r, w, k, v, kk, a, s0):
    def step(S, inp):
        r_t, w_t, k_t, v_t, kk_t, a_t = inp
        sa = jnp.einsum('dbhk,dbhkv->dbhv', kk_t, S)
        S = (w_t[..., None] * S - (kk_t * a_t)[..., None] * sa[..., None, :]
             + k_t[..., None] * v_t[..., None, :])
        return S, jnp.einsum('dbhk,dbhkv->dbhv', r_t, S)

    xs = tuple(jnp.moveaxis(t, 2, 0) for t in (r, w, k, v, kk, a))
    s_fin, y = lax.scan(step, s0, xs)
    return jnp.moveaxis(y, 0, 2), s_fin


def _rwkv7_group(p, s0, mu, w0, w2, a0, a2, g2, k_k, k_a, r_k, ln_w, ln_b):
    f32 = jnp.float32
    dt = p.dtype
    b, t, _ = p.shape
    prev = jnp.pad(p[:, :-1], ((0, 0), (1, 0), (0, 0)))
    nxt = jnp.pad(p[:, 1:], ((0, 0), (0, 1), (0, 0)))
    p = p + mu[0] * (prev - p) + mu[1] * (nxt - p)
    r, k, v, xw_f, xw_b, xa_f, xa_b, xg = _split(
        p, (RW_DIM, RW_DIM, RW_DIM, RW_DECAY_RANK, RW_DECAY_RANK, RW_ICLR_RANK, RW_ICLR_RANK, RW_GATE_RANK))
    hd = lambda z: _heads(z, RW_HEADS, RW_HEAD_DIM)
    w_raw = w0[:, None, None, :] + jnp.einsum('dbtr,drc->dbtc', jnp.tanh(jnp.stack([xw_f, xw_b])), w2)
    decay = jnp.exp(-jnp.exp(-jax.nn.softplus(-w_raw.astype(f32)) - 0.5))
    a = jax.nn.sigmoid((a0[:, None, None, :]
                        + jnp.einsum('dbtr,drc->dbtc', jnp.stack([xa_f, xa_b]), a2)).astype(f32))
    g = jax.nn.sigmoid(xg) @ g2
    kf, rf, vf = k.astype(f32), r.astype(f32), v.astype(f32)
    kk = _l2norm(hd(kf * k_k.astype(f32)))
    k_mod = kf[None] * (1.0 + (a - 1.0) * k_a.astype(f32))
    y, s_fin = _rwkv7_scan(_to_dirs(hd(rf)), _orient(hd(decay)), _orient(hd(k_mod)),
                           _to_dirs(hd(vf)), _to_dirs(kk), _orient(hd(a)), s0.astype(f32))
    y = _merge(y)
    mean = jnp.mean(y, axis=-1, keepdims=True)
    var = jnp.mean(jnp.square(y - mean), axis=-1, keepdims=True)
    y = (y - mean) * lax.rsqrt(var + RW_LN_EPS) * hd(ln_w.astype(f32)) + hd(ln_b.astype(f32))
    bonus = jnp.sum(hd(rf) * hd(kf) * hd(r_k.astype(f32)), axis=-1, keepdims=True) * hd(vf)
    y = (y + bonus).reshape(b, t, RW_DIM) * g.astype(f32)
    return y.astype(dt), s_fin


def _gla_chunked(q, k, v, g, s0):
    f32 = jnp.float32
    qc, kc, vc, gc = (_to_chunks(z.astype(f32), GLA_CHUNK) for z in (q, k, v, g))
    bcum = jnp.cumsum(gc, axis=-2)
    b_last = bcum[..., -1:, :]
    causal = jnp.tril(jnp.ones((GLA_CHUNK, GLA_CHUNK), dtype=bool))
    diff = bcum[..., :, None, :] - bcum[..., None, :, :]
    pair_decay = jnp.exp(jnp.where(causal[:, :, None], diff, -jnp.inf))
    scores = jnp.einsum('...ik,...jk,...ijk->...ij', qc, kc, pair_decay)
    o_intra = scores @ vc
    q_in = qc * jnp.exp(bcum)
    k_out = kc * jnp.exp(b_last - bcum)
    s_decay = jnp.exp(b_last[..., 0, :])

    def step(S, inp):
        qi, ki, vi, di = inp
        o = qi @ S
        S = di[..., :, None] * S + jnp.swapaxes(ki, -1, -2) @ vi
        return S, o

    xs = tuple(jnp.moveaxis(z, 3, 0) for z in (q_in, k_out, vc, s_decay))
    s_fin, o_inter = lax.scan(step, s0.astype(f32), xs)
    return _from_chunks(o_intra + jnp.moveaxis(o_inter, 0, 3)), s_fin


def _gla_group(p, s0, gk2, gk_b, norm_w):
    f32 = jnp.float32
    dt = p.dtype
    b, t, _ = p.shape
    q, k, v, gf, gb, og = _split(p, (GLA_KDIM, GLA_KDIM, GLA_DIM, GLA_GATE_RANK, GLA_GATE_RANK, GLA_DIM))
    q = _heads(q.astype(f32), GLA_HEADS, GLA_DK) * (GLA_DK ** -0.5)
    k = _heads(k.astype(f32), GLA_HEADS, GLA_DK)
    v = _heads(v.astype(f32), GLA_HEADS, GLA_DV)
    gk = jax.nn.log_sigmoid((jnp.einsum('dbtr,drc->dbtc', jnp.stack([gf, gb]), gk2)
                             + gk_b[:, None, None, :]).astype(f32)) / GLA_GATE_NORM
    o, s_fin = _gla_chunked(_to_dirs(q), _to_dirs(k), _to_dirs(v),
                            _orient(_heads(gk, GLA_HEADS, GLA_DK)), s0)
    o = _gated_head_rms(_merge(o), norm_w, _heads(og, GLA_HEADS, GLA_DV))
    return o.reshape(b, t, GLA_DIM).astype(dt), s_fin


def _gdn_chunked(q, k, v, beta, g, s0):
    f32 = jnp.float32
    qc, kc, vc = (_to_chunks(z.astype(f32), DN_CHUNK) for z in (q, k, v))
    bc, gc = (_to_chunks(z.astype(f32), DN_CHUNK) for z in (beta, g))
    gam = jnp.cumsum(gc, axis=-1)
    gam_last = gam[..., -1:]
    causal = jnp.tril(jnp.ones((DN_CHUNK, DN_CHUNK), dtype=bool))
    pair_decay = jnp.exp(jnp.where(causal, gam[..., :, None] - gam[..., None, :], -jnp.inf))
    kb = kc * bc[..., None]
    a_mat = (kb @ jnp.swapaxes(kc, -1, -2)) * pair_decay
    rhs = jnp.concatenate([vc * bc[..., None], kb * jnp.exp(gam)[..., None]], axis=-1)
    sol = lax.linalg.triangular_solve(a_mat, rhs, left_side=True, lower=True, unit_diagonal=True)
    u, w = sol[..., :DN_HEAD_DIM], sol[..., DN_HEAD_DIM:]
    qk = (qc @ jnp.swapaxes(kc, -1, -2)) * pair_decay
    q_in = qc * jnp.exp(gam)[..., None]
    k_out = kc * jnp.exp(gam_last - gam)[..., None]
    s_decay = jnp.exp(gam_last)

    def step(S, inp):
        ui, wi, qki, qi, ki, di = inp
        vn = ui - wi @ S
        o = qi @ S + qki @ vn
        S = di[..., None] * S + jnp.swapaxes(ki, -1, -2) @ vn
        return S, o

    xs = tuple(jnp.moveaxis(z, 3, 0) for z in (u, w, qk, q_in, k_out, s_decay))
    s_fin, o = lax.scan(step, s0.astype(f32), xs)
    return _from_chunks(jnp.moveaxis(o, 0, 3)), s_fin


def _gdn_group(p, s0, conv_w, a_log, dt_bias, norm_w):
    f32 = jnp.float32
    dt = p.dtype
    b, t, _ = p.shape
    qkv, z, a_f, a_b, b_f, b_b = _split(p, (3 * DN_DIM, DN_DIM, DN_HEADS, DN_HEADS, DN_HEADS, DN_HEADS))
    qkv = jax.nn.silu(_dwconv_centred(qkv, conv_w))
    q, k, v = jnp.split(qkv, 3, axis=-1)
    q = _l2norm(_heads(q, DN_HEADS, DN_HEAD_DIM)) * (DN_HEAD_DIM ** -0.5)
    k = _l2norm(_heads(k, DN_HEADS, DN_HEAD_DIM))
    v = _heads(v.astype(f32), DN_HEADS, DN_HEAD_DIM)
    beta = jax.nn.sigmoid(jnp.stack([b_f, b_b]).astype(f32))
    g = (-jnp.exp(a_log.astype(f32))[:, None, None, :]
         * jax.nn.softplus(jnp.stack([a_f, a_b]).astype(f32) + dt_bias.astype(f32)[:, None, None, :]))
    o, s_fin = _gdn_chunked(_to_dirs(q), _to_dirs(k), _to_dirs(v), _orient(beta), _orient(g), s0)
    o = _gated_head_rms(_merge(o), norm_w, _heads(z, DN_HEADS, DN_HEAD_DIM))
    return o.reshape(b, t, DN_DIM).astype(dt), s_fin


def _hier_moe(h, wg, bg, we, be, w_gate, w_up, w_down):
    f32 = jnp.float32
    logit_g = (h @ wg + bg).astype(f32)
    grp_oh = jax.nn.one_hot(jnp.argmax(logit_g, axis=-1), N_GROUPS, dtype=f32)
    p_grp = jnp.sum(jax.nn.softmax(logit_g, axis=-1) * grp_oh, axis=-1, keepdims=True)
    logit_e = (jnp.einsum('btd,dge->btge', h, we) + be).astype(f32)
    logit_e = jnp.einsum('btge,btg->bte', logit_e, grp_oh)
    top_v, top_i = lax.top_k(logit_e, TOP_K_IN_GROUP)
    p_top = jax.nn.softmax(top_v, axis=-1) * p_grp
    gate_e = jnp.sum(jax.nn.one_hot(top_i, EXPERTS_PER_GROUP, dtype=f32) * p_top[..., None], axis=-2)
    gate = grp_oh[..., None] * gate_e[..., None, :]
    hid = (jax.nn.silu(jnp.einsum('btd,gedf->btgef', h, w_gate))
           * jnp.einsum('btd,gedf->btgef', h, w_up))
    hid = hid * gate.astype(h.dtype)[..., None]
    return jnp.einsum('btgef,gefd->btd', hid, w_down)


def _layer(x, cond, ada_w, ada_b, n1_w, n2_w, w_out, mix_p, moe_p, s0):
    (w_in, rw_mu, rw_w0, rw_w2, rw_a0, rw_a2, rw_g2, rw_kk, rw_ka, rw_rk, rw_ln_w, rw_ln_b,
     gla_gk2, gla_gk_b, gla_norm_w, dn_conv, dn_a_log, dn_dt_bias, dn_norm_w) = mix_p
    s_rw0, s_gla0, s_dn0 = s0
    mod = (jax.nn.silu(cond) @ ada_w + ada_b)[:, None, :]
    sh1, sc1, g1, sh2, sc2, g2 = jnp.split(mod, 6, axis=-1)
    h = _rmsnorm(x, n1_w) * (1.0 + sc1) + sh1
    p_rw, p_gla, p_dn = _split(h @ w_in, (RW_COLS, GLA_COLS, DN_COLS))
    y_rw, s_rw = _rwkv7_group(p_rw, s_rw0, rw_mu, rw_w0, rw_w2, rw_a0, rw_a2, rw_g2,
                              rw_kk, rw_ka, rw_rk, rw_ln_w, rw_ln_b)
    y_gla, s_gla = _gla_group(p_gla, s_gla0, gla_gk2, gla_gk_b, gla_norm_w)
    y_dn, s_dn = _gdn_group(p_dn, s_dn0, dn_conv, dn_a_log, dn_dt_bias, dn_norm_w)
    x = x + g1 * (jnp.concatenate([y_rw, y_gla, y_dn], axis=-1) @ w_out)
    h2 = _rmsnorm(x, n2_w) * (1.0 + sc2) + sh2
    x = x + g2 * _hier_moe(h2, *moe_p)
    return x, (s_rw, s_gla, s_dn)


def setup_inputs(seed: int = 0) -> dict:
    key = jax.random.key(seed)
    keys = iter(jax.random.split(key, 48))
    f32 = jnp.float32
    L, D, G, E, F = DEPTH, D_MODEL, N_GROUPS, EXPERTS_PER_GROUP, D_EXPERT

    def nrm(shape, scale):
        return jax.random.normal(next(keys), shape, f32) * scale

    def uni(shape, lo, hi):
        return jax.random.uniform(next(keys), shape, f32, lo, hi)

    x_prompt = nrm((BATCH, SEQ, D), 1.0)
    x_sample = nrm((DEC_BATCH, DEC_SEQ, D), 1.0)
    c = nrm((DEC_BATCH, D), 1.0)
    state_rwkv = nrm((DEC_BATCH, L, 2, RW_HEADS, RW_HEAD_DIM, RW_HEAD_DIM), 0.3)
    state_gla = nrm((DEC_BATCH, L, 2, GLA_HEADS, GLA_DK, GLA_DV), 0.3)
    state_delta = nrm((DEC_BATCH, L, 2, DN_HEADS, DN_HEAD_DIM, DN_HEAD_DIM), 0.3)
    c_ctx = nrm((D,), 1.0)
    ada_w = nrm((L, D, 6 * D), 0.5 * D ** -0.5)
    ada_b = nrm((L, 6 * D), 0.1)
    norm1_w = 1.0 + nrm((L, D), 0.05)
    norm2_w = 1.0 + nrm((L, D), 0.05)
    final_norm_w = 1.0 + nrm((D,), 0.05)
    w_in = nrm((L, D, PROJ_COLS), D ** -0.5)
    w_out = nrm((L, MIX_DIM, D), MIX_DIM ** -0.5)
    rwkv_mu = uni((L, 2, RW_COLS), 0.0, 0.5)
    rwkv_w0 = uni((L, 2, RW_DIM), -6.5, -1.5)
    rwkv_w2 = nrm((L, 2, RW_DECAY_RANK, RW_DIM), 0.1 * RW_DECAY_RANK ** -0.5)
    rwkv_a0 = nrm((L, 2, RW_DIM), 0.5)
    rwkv_a2 = nrm((L, 2, RW_ICLR_RANK, RW_DIM), 0.5 * RW_ICLR_RANK ** -0.5)
    rwkv_g2 = nrm((L, RW_GATE_RANK, RW_DIM), RW_GATE_RANK ** -0.5)
    rwkv_kk = 0.85 + nrm((L, RW_DIM), 0.05)
    rwkv_ka = 1.0 + nrm((L, RW_DIM), 0.05)
    rwkv_rk = nrm((L, RW_DIM), 0.1)
    rwkv_ln_w = 1.0 + nrm((L, RW_DIM), 0.05)
    rwkv_ln_b = nrm((L, RW_DIM), 0.05)
    gla_gk2 = nrm((L, 2, GLA_GATE_RANK, GLA_KDIM), GLA_GATE_RANK ** -0.5)
    gla_gk_b = nrm((L, 2, GLA_KDIM), 0.1)
    gla_norm_w = 1.0 + nrm((L, GLA_DV), 0.05)
    dn_conv = nrm((L, DN_CONV, 3 * DN_DIM), DN_CONV ** -0.5)
    dn_a_log = jnp.log(uni((L, 2, DN_HEADS), 1.0, 16.0))
    dt = jnp.exp(uni((L, 2, DN_HEADS), math.log(1e-3), math.log(1e-1)))
    dn_dt_bias = dt + jnp.log(-jnp.expm1(-dt))
    dn_norm_w = 1.0 + nrm((L, DN_HEAD_DIM), 0.05)
    moe_wg = nrm((L, D, G), D ** -0.5)
    moe_bg = nrm((L, G), 0.01)
    moe_we = nrm((L, D, G, E), D ** -0.5)
    moe_be = nrm((L, G, E), 0.01)
    moe_w_gate = nrm((L, G, E, D, F), D ** -0.5)
    moe_w_up = nrm((L, G, E, D, F), D ** -0.5)
    moe_w_down = nrm((L, G, E, F, D), F ** -0.5)
    return {'x_prompt': x_prompt, 'x_sample': x_sample, 'c': c,
            'state_rwkv': state_rwkv, 'state_gla': state_gla, 'state_delta': state_delta,
            'c_ctx': c_ctx, 'ada_w': ada_w, 'ada_b': ada_b,
            'norm1_w': norm1_w, 'norm2_w': norm2_w, 'final_norm_w': final_norm_w,
            'w_in': w_in, 'w_out': w_out,
            'rwkv_mu': rwkv_mu, 'rwkv_w0': rwkv_w0, 'rwkv_w2': rwkv_w2, 'rwkv_a0': rwkv_a0,
            'rwkv_a2': rwkv_a2, 'rwkv_g2': rwkv_g2, 'rwkv_kk': rwkv_kk, 'rwkv_ka': rwkv_ka,
            'rwkv_rk': rwkv_rk, 'rwkv_ln_w': rwkv_ln_w, 'rwkv_ln_b': rwkv_ln_b,
            'gla_gk2': gla_gk2, 'gla_gk_b': gla_gk_b, 'gla_norm_w': gla_norm_w,
            'dn_conv': dn_conv, 'dn_a_log': dn_a_log, 'dn_dt_bias': dn_dt_bias, 'dn_norm_w': dn_norm_w,
            'moe_wg': moe_wg, 'moe_bg': moe_bg, 'moe_we': moe_we, 'moe_be': moe_be,
            'moe_w_gate': moe_w_gate, 'moe_w_up': moe_w_up, 'moe_w_down': moe_w_down}


def reference(x_prompt, x_sample, c, state_rwkv, state_gla, state_delta, c_ctx, ada_w, ada_b,
              norm1_w, norm2_w, final_norm_w, w_in, w_out, rwkv_mu, rwkv_w0, rwkv_w2, rwkv_a0,
              rwkv_a2, rwkv_g2, rwkv_kk, rwkv_ka, rwkv_rk, rwkv_ln_w, rwkv_ln_b, gla_gk2, gla_gk_b,
              gla_norm_w, dn_conv, dn_a_log, dn_dt_bias, dn_norm_w, moe_wg, moe_bg, moe_we, moe_be,
              moe_w_gate, moe_w_up, moe_w_down):
    f32 = jnp.float32
    n_ctx = x_prompt.shape[0]
    xp = x_prompt
    xs = x_sample + _grid_pos_embed(x_sample.shape[1], x_sample.shape[2])[None].astype(x_sample.dtype)
    rw_list, gla_list, dn_list = [], [], []
    for l in range(DEPTH):
        mix_p = (w_in[l], rwkv_mu[l], rwkv_w0[l], rwkv_w2[l], rwkv_a0[l], rwkv_a2[l], rwkv_g2[l],
                 rwkv_kk[l], rwkv_ka[l], rwkv_rk[l], rwkv_ln_w[l], rwkv_ln_b[l],
                 gla_gk2[l], gla_gk_b[l], gla_norm_w[l],
                 dn_conv[l], dn_a_log[l], dn_dt_bias[l], dn_norm_w[l])
        moe_p = (moe_wg[l], moe_bg[l], moe_we[l], moe_be[l], moe_w_gate[l], moe_w_up[l], moe_w_down[l])
        zero_states = (jnp.zeros((2, n_ctx, RW_HEADS, RW_HEAD_DIM, RW_HEAD_DIM), f32),
                       jnp.zeros((2, n_ctx, GLA_HEADS, GLA_DK, GLA_DV), f32),
                       jnp.zeros((2, n_ctx, DN_HEADS, DN_HEAD_DIM, DN_HEAD_DIM), f32))
        xp, (s_rw, s_gla, s_dn) = _layer(xp, c_ctx[None, :], ada_w[l], ada_b[l], norm1_w[l], norm2_w[l],
                                         w_out[l], mix_p, moe_p, zero_states)
        rw_list.append(s_rw)
        gla_list.append(s_gla)
        dn_list.append(s_dn)
        cached = (jnp.swapaxes(state_rwkv[:, l], 0, 1), jnp.swapaxes(state_gla[:, l], 0, 1),
                  jnp.swapaxes(state_delta[:, l], 0, 1))
        xs, _ = _layer(xs, c, ada_w[l], ada_b[l], norm1_w[l], norm2_w[l], w_out[l], mix_p, moe_p, cached)
    y_prompt = _rmsnorm(xp, final_norm_w)
    y_sample = _rmsnorm(xs, final_norm_w)
    new_state_rwkv = jnp.transpose(jnp.stack(rw_list), (2, 0, 1, 3, 4, 5)).astype(x_prompt.dtype)
    new_state_gla = jnp.transpose(jnp.stack(gla_list), (2, 0, 1, 3, 4, 5)).astype(x_prompt.dtype)
    new_state_delta = jnp.transpose(jnp.stack(dn_list), (2, 0, 1, 3, 4, 5)).astype(x_prompt.dtype)
    return (y_prompt, y_sample, new_state_rwkv, new_state_gla, new_state_delta)
```

```python
import functools
import math

import numpy as np
import jax
import jax.numpy as jnp
from jax import lax
from jax.experimental import pallas as pl
from jax.experimental.pallas import tpu as pltpu

F32 = jnp.float32
BF16 = jnp.bfloat16
HI = lax.Precision.HIGHEST

D_MODEL = 1024
DEPTH = 4
GRID_W = 64
POS_BASE = 10000.0
RMS_EPS = 1e-6
L2_EPS = 1e-6

RW_HEADS = 6
RW_DIM = 384
RW_COLS = 1536
RW_LN_EPS = 64e-5
GLA_HEADS = 4
GLA_DK = 32
GLA_DV = 64
GLA_KDIM = 128
GLA_DIM = 256
GLA_GATE_RANK = 16
GLA_GATE_NORM = 16.0
DN_HEADS = 6
DN_DIM = 384
N_GROUPS = 4
EXPERTS_PER_GROUP = 4
N_EXPERTS = 16
D_EXPERT = 512

LANES = 128
CHUNK = 64
HEAD = 64
GLA_OFF = RW_COLS
GLA_PCOLS = 896
DN_OFF = GLA_OFF + GLA_PCOLS
DN_PCOLS = 1664
P_COLS = DN_OFF + DN_PCOLS
VMEM_LIMIT = 56 * 1024 * 1024


def _mm(a, b):
    return jnp.dot(a.astype(BF16), b.astype(BF16), preferred_element_type=F32)


def _mm_nt(a, b):
    return lax.dot_general(a.astype(BF16), b.astype(BF16), (((1,), (1,)), ((), ())),
                           preferred_element_type=F32)


def _mm_tn(a, b):
    return lax.dot_general(a.astype(BF16), b.astype(BF16), (((0,), (0,)), ((), ())),
                           preferred_element_type=F32)


def _mmh(a, b):
    return jnp.dot(a, b, precision=HI, preferred_element_type=F32)


def _mmh_nt(a, b):
    return lax.dot_general(a, b, (((1,), (1,)), ((), ())), precision=HI, preferred_element_type=F32)


def _iota(shape, dim):
    return lax.broadcasted_iota(jnp.int32, shape, dim)


def _sigmoid(x):
    return 1.0 / (1.0 + jnp.exp(-x))


def _silu(x):
    return x * _sigmoid(x)


def _softplus(x):
    return jnp.maximum(x, 0.0) + jnp.log(1.0 + jnp.exp(-jnp.abs(x)))


def _log_sigmoid(x):
    return -_softplus(-x)


def _tri(n, d, strict):
    r = _iota((n, n), 0)
    c = _iota((n, n), 1)
    if d == 0:
        return (c < r) if strict else (c <= r)
    return (c > r) if strict else (c >= r)


def _pair_masks(d):
    n = 2 * CHUNK
    r = _iota((n, n), 0)
    c = _iota((n, n), 1)
    t = r & (CHUNK - 1)
    s = c & (CHUNK - 1)
    incl = (s <= t) if d == 0 else (s >= t)
    strict = (s < t) if d == 0 else (s > t)
    same_head = (r >> 6) == (c >> 6)
    eye = s == t
    return incl, strict, same_head, eye


def _stack2(a):
    lane = _iota(a.shape, 1)
    return jnp.concatenate([jnp.where(lane < HEAD, a, 0.0), jnp.where(lane >= HEAD, a, 0.0)], axis=0)


def _dup2(a):
    return jnp.concatenate([a, a], axis=0)


def _papply(a2, x):
    r = _mm(a2[:, :CHUNK], x)
    lane = _iota((CHUNK, LANES), 1)
    return jnp.where(lane < HEAD, r[:CHUNK], r[CHUNK:])


def _split_bf16(a):
    hi = a.astype(BF16)
    return hi, (a - hi.astype(F32)).astype(BF16)


def _mm3(a, b):
    a_hi, a_lo = _split_bf16(a)
    b_hi, b_lo = _split_bf16(b)
    dot = lambda x, y: jnp.dot(x, y, preferred_element_type=F32)
    return dot(a_hi, b_hi) + (dot(a_hi, b_lo) + dot(a_lo, b_hi))


def _inv_unit_tri(l2, same_head, eye):
    n = 2 * CHUNK
    t = _iota((n, n), 0) & (CHUNK - 1)
    s = _iota((n, n), 1) & (CHUNK - 1)
    same_blk = lambda width: _shr(t, width) == _shr(s, width)
    per_head = lambda x: jnp.where(same_head, x, 0.0)
    n0 = jnp.where(same_blk(8), -l2, 0.0)
    p1 = _mm3(per_head(n0), n0)
    p2 = _mm3(per_head(p1), p1)
    inv = jnp.where(eye, 1.0, 0.0) + n0
    inv = inv + _mm3(per_head(inv), p1)
    inv = inv + _mm3(per_head(inv), p2)
    for width in (8, 16, 32):
        off = jnp.where(same_blk(2 * width) & jnp.logical_not(same_blk(width)), l2, 0.0)
        inv = inv - _mm3(per_head(_mm3(per_head(inv), off)), inv)
    return inv


def _head_sum(x, bm):
    return _mmh(x, bm)


def _shr(x, width):
    return x >> (width.bit_length() - 1)


def _block_ones(n, width):
    r = _iota((n, n), 0)
    c = _iota((n, n), 1)
    return jnp.where(_shr(r, width) == _shr(c, width), 1.0, 0.0).astype(F32)


def _shifted(ref, start, c, n_chunks, t_len):
    pc = ref[pl.ds(start, CHUNK), :]
    pb = ref[pl.ds(pl.multiple_of(jnp.maximum(start - 8, 0), 8), 8), :]
    nb = ref[pl.ds(pl.multiple_of(jnp.minimum(start + CHUNK, t_len - 8), 8), 8), :]
    carry_p = jnp.where(c > 0, pb[7:8, :], 0.0)
    carry_n = jnp.where(c < n_chunks - 1, nb[0:1, :], 0.0)
    row = _iota(pc.shape, 0)
    prev = jnp.where(row == 0, carry_p, pltpu.roll(pc, 1, 0))
    nxt = jnp.where(row == CHUNK - 1, carry_n, pltpu.roll(pc, CHUNK - 1, 0))
    return pc, prev, nxt


def _ada_kernel(c_ref, w_ref, b_ref, o_ref):
    c = c_ref[...]
    o_ref[0] = _mm(_silu(c), w_ref[0]) + b_ref[0]


def _ada_call(cond8, ada_w, ada_b):
    nl = ada_w.shape[0]
    return pl.pallas_call(
        _ada_kernel,
        grid=(nl, 6),
        in_specs=[pl.BlockSpec((8, D_MODEL), lambda l, j: (0, 0)),
                  pl.BlockSpec((1, D_MODEL, D_MODEL), lambda l, j: (l, 0, j)),
                  pl.BlockSpec((1, 1, D_MODEL), lambda l, j: (l, 0, j))],
        out_specs=pl.BlockSpec((1, 8, D_MODEL), lambda l, j: (l, 0, j)),
        out_shape=jax.ShapeDtypeStruct((nl, 8, 6 * D_MODEL), F32),
        compiler_params=pltpu.CompilerParams(dimension_semantics=("arbitrary", "arbitrary"),
                                             vmem_limit_bytes=VMEM_LIMIT),
        name="ada_mod",
    )(cond8, ada_w, ada_b.reshape(nl, 1, 6 * D_MODEL))


def _cond_row(i, tile, n_ctx_tok, lat_t):
    tok = i * tile
    return jnp.where(tok < n_ctx_tok, 0, 1 + (tok - n_ctx_tok) // lat_t)


def _in_kernel(x_ref, sh_ref, sc_ref, nw_ref, w_ref, o_ref):
    x = x_ref[...]
    y = x * lax.rsqrt(jnp.mean(x * x, axis=-1, keepdims=True) + RMS_EPS) * nw_ref[...]
    h = y * (1.0 + sc_ref[0]) + sh_ref[0]
    o_ref[...] = jnp.dot(h.astype(BF16), w_ref[...], preferred_element_type=F32)


def _in_call(x, mod3, layer, nw, w_in_p, n_ctx_tok, lat_t, tile=256):
    n = x.shape[0]
    row = lambda i: layer * 8 + _cond_row(i, tile, n_ctx_tok, lat_t)
    return pl.pallas_call(
        _in_kernel,
        grid=(n // tile,),
        in_specs=[pl.BlockSpec((tile, D_MODEL), lambda i: (i, 0)),
                  pl.BlockSpec((1, 1, D_MODEL), lambda i: (row(i), 0, 0)),
                  pl.BlockSpec((1, 1, D_MODEL), lambda i: (row(i), 0, 1)),
                  pl.BlockSpec((1, D_MODEL), lambda i: (0, 0)),
                  pl.BlockSpec((D_MODEL, P_COLS), lambda i: (0, 0))],
        out_specs=pl.BlockSpec((tile, P_COLS), lambda i: (i, 0)),
        out_shape=jax.ShapeDtypeStruct((n, P_COLS), F32),
        compiler_params=pltpu.CompilerParams(dimension_semantics=("arbitrary",), vmem_limit_bytes=VMEM_LIMIT),
        name="in_proj",
    )(x, mod3, mod3, nw.reshape(1, D_MODEL), w_in_p)


EXP_M05 = math.exp(-0.5)


def _rwkv_kernel(*refs, t_len, has_init):
    if has_init:
        (r_ref, k_ref, v_ref, xw_ref, xa_ref, xg_ref, mu_ref, vec_ref, w2_ref, a2_ref, g2_ref, s0_ref,
         y_ref, sfin_ref, xs_ref, yd_ref, st_ref) = refs
    else:
        (r_ref, k_ref, v_ref, xw_ref, xa_ref, xg_ref, mu_ref, vec_ref, w2_ref, a2_ref, g2_ref,
         y_ref, sfin_ref, xs_ref, yd_ref, st_ref) = refs
    n_chunks = t_len // CHUNK
    vec = vec_ref[0]
    bm = _block_ones(LANES, HEAD)

    def shift_body(c, carry):
        start = pl.multiple_of(c * CHUNK, CHUNK)
        for idx, ref in enumerate((r_ref, k_ref, v_ref, xw_ref, xa_ref, xg_ref)):
            pc, prev, nxt = _shifted(ref, start, c, n_chunks, t_len)
            mu0 = mu_ref[0, 0:1, idx * LANES:(idx + 1) * LANES]
            mu1 = mu_ref[0, 1:2, idx * LANES:(idx + 1) * LANES]
            xs_ref[idx, pl.ds(start, CHUNK), :] = pc + mu0 * (prev - pc) + mu1 * (nxt - pc)
        return carry

    lax.fori_loop(0, n_chunks, shift_body, 0)

    if has_init:
        st_ref[...] = s0_ref[0, 0]
    else:
        st_ref[...] = jnp.zeros_like(st_ref)

    def chunk_dir(d, c):
        start = pl.multiple_of(c * CHUNK, CHUNK)
        r = xs_ref[0, pl.ds(start, CHUNK), :]
        k = xs_ref[1, pl.ds(start, CHUNK), :]
        v = xs_ref[2, pl.ds(start, CHUNK), :]
        xw = xs_ref[3, pl.ds(start, CHUNK), :]
        xa = xs_ref[4, pl.ds(start, CHUNK), :]
        incl, strict, same_head, eye = _pair_masks(d)
        w_raw = vec[d:d + 1] + _mm(jnp.tanh(xw), w2_ref[d])
        lw = -EXP_M05 * _sigmoid(w_raw)
        a = _sigmoid(vec[2 + d:3 + d] + _mm(xa, a2_ref[d]))
        kk = k * vec[4:5]
        kk = kk * lax.rsqrt(_head_sum(kk * kk, bm) + L2_EPS)
        kmod = k * (1.0 + (a - 1.0) * vec[5:6])
        cum = _mmh(jnp.where(_tri(CHUNK, d, False), 1.0, 0.0), lw)
        half = 0.5 * jnp.sum(lw, axis=0, keepdims=True)
        em = jnp.exp(half)
        e_up = jnp.exp(half - cum)
        rh = r * jnp.exp(cum - half)
        ch = kk * jnp.exp(cum - lw - half)
        bh = kk * a * e_up
        kh = kmod * e_up
        sc, sr, db, dk = _stack2(ch), _stack2(rh), _dup2(bh), _dup2(kh)
        l_cb = jnp.where(strict, _mm_nt(sc, db), 0.0)
        l_ck = jnp.where(strict, _mm_nt(sc, dk), 0.0)
        m_rb = jnp.where(incl, _mm_nt(sr, db), 0.0)
        m_rk = jnp.where(incl, _mm_nt(sr, dk), 0.0)
        t2 = _inv_unit_tri(l_cb, same_head, eye)
        w_bar = _papply(t2, ch * em)
        u_bar = -_papply(t2, _papply(l_ck, v))
        y0 = _papply(m_rk, v)
        st = st_ref[d]
        u = u_bar - _mm_nt(w_bar, st)
        y = y0 + _mm_nt(rh * em, st) + _papply(m_rb, u)
        st_ref[d] = st * (em * em) + jnp.where(bm > 0.5, _mm_tn(u, bh * em) + _mm_tn(v, kh * em), 0.0)
        yd_ref[d, pl.ds(start, CHUNK), :] = y

    def seq_body(i, carry):
        chunk_dir(0, i)
        chunk_dir(1, n_chunks - 1 - i)
        return carry

    lax.fori_loop(0, n_chunks, seq_body, 0)
    sfin_ref[0, 0] = st_ref[...]

    def post_body(c, carry):
        start = pl.multiple_of(c * CHUNK, CHUNK)
        r = xs_ref[0, pl.ds(start, CHUNK), :]
        k = xs_ref[1, pl.ds(start, CHUNK), :]
        v = xs_ref[2, pl.ds(start, CHUNK), :]
        xg = xs_ref[5, pl.ds(start, CHUNK), :]
        y = yd_ref[0, pl.ds(start, CHUNK), :] + yd_ref[1, pl.ds(start, CHUNK), :]
        mean = _head_sum(y, bm) * (1.0 / HEAD)
        yc = y - mean
        var = _head_sum(yc * yc, bm) * (1.0 / HEAD)
        yn = yc * lax.rsqrt(var + RW_LN_EPS) * vec[7:8] + vec[8:9]
        bonus = _head_sum(r * k * vec[6:7], bm) * v
        g = _mm(_sigmoid(xg), g2_ref[...])
        y_ref[pl.ds(start, CHUNK), :] = (yn + bonus) * g
        return carry

    lax.fori_loop(0, n_chunks, post_body, 0)


def _rwkv_call(p, base_blk, n_seq, t_len, prm, s0):
    has_init = s0 is not None
    def pspec(tile_fn):
        return pl.BlockSpec((t_len, LANES), lambda b, j: (base_blk + b, tile_fn(j)))

    in_specs = [pspec(lambda j: j), pspec(lambda j: 3 + j), pspec(lambda j: 6 + j),
                pspec(lambda j: 9), pspec(lambda j: 10), pspec(lambda j: 11),
                pl.BlockSpec((1, 2, 6 * LANES), lambda b, j: (j, 0, 0)),
                pl.BlockSpec((1, 16, LANES), lambda b, j: (j, 0, 0)),
                pl.BlockSpec((2, LANES, LANES), lambda b, j: (0, 0, j)),
                pl.BlockSpec((2, LANES, LANES), lambda b, j: (0, 0, j)),
                pl.BlockSpec((LANES, LANES), lambda b, j: (0, j))]
    args = [p, p, p, p, p, p, prm["mu"], prm["vec"], prm["w2"], prm["a2"], prm["g2"]]
    if has_init:
        in_specs.append(pl.BlockSpec((1, 1, 2, LANES, LANES), lambda b, j: (b, j, 0, 0, 0)))
        args.append(s0)
    y, sfin = pl.pallas_call(
        functools.partial(_rwkv_kernel, t_len=t_len, has_init=has_init),
        grid=(n_seq, 3),
        in_specs=in_specs,
        out_specs=[pl.BlockSpec((t_len, LANES), lambda b, j: (b, j)),
                   pl.BlockSpec((1, 1, 2, LANES, LANES), lambda b, j: (b, j, 0, 0, 0))],
        out_shape=[jax.ShapeDtypeStruct((n_seq * t_len, RW_DIM), F32),
                   jax.ShapeDtypeStruct((n_seq, 3, 2, LANES, LANES), F32)],
        scratch_shapes=[pltpu.VMEM((6, t_len, LANES), F32),
                        pltpu.VMEM((2, t_len, LANES), F32),
                        pltpu.VMEM((2, LANES, LANES), F32)],
        compiler_params=pltpu.CompilerParams(dimension_semantics=("arbitrary", "arbitrary"),
                                             vmem_limit_bytes=VMEM_LIMIT),
        name="rwkv7_mix",
    )(*args)
    return y, sfin


def _gdn_kernel(*refs, t_len, has_init):
    if has_init:
        (q_ref, k_ref, v_ref, z_ref, gt_ref, cw_ref, vec_ref, gvec_ref, ea_ref, eb_ref, s0_ref,
         y_ref, sfin_ref, xs_ref, yd_ref, st_ref) = refs
    else:
        (q_ref, k_ref, v_ref, z_ref, gt_ref, cw_ref, vec_ref, gvec_ref, ea_ref, eb_ref,
         y_ref, sfin_ref, xs_ref, yd_ref, st_ref) = refs
    n_chunks = t_len // CHUNK
    j = pl.program_id(1)
    vec = vec_ref[0]
    gvec = gvec_ref[...]
    bm = _block_ones(LANES, HEAD)

    def conv_body(c, carry):
        start = pl.multiple_of(c * CHUNK, CHUNK)
        for idx, ref in enumerate((q_ref, k_ref, v_ref)):
            pc, prev, nxt = _shifted(ref, start, c, n_chunks, t_len)
            cw = cw_ref[0, :, idx * LANES:(idx + 1) * LANES]
            xs_ref[idx, pl.ds(start, CHUNK), :] = _silu(cw[0:1] * prev + cw[1:2] * pc + cw[2:3] * nxt)
        return carry

    lax.fori_loop(0, n_chunks, conv_body, 0)

    if has_init:
        st_ref[...] = s0_ref[0, 0]
    else:
        st_ref[...] = jnp.zeros_like(st_ref)

    def chunk_dir(d, c):
        start = pl.multiple_of(c * CHUNK, CHUNK)
        q = xs_ref[0, pl.ds(start, CHUNK), :]
        k = xs_ref[1, pl.ds(start, CHUNK), :]
        v = xs_ref[2, pl.ds(start, CHUNK), :]
        gt = gt_ref[pl.ds(start, CHUNK), :]
        incl, strict, same_head, eye = _pair_masks(d)
        q = q * lax.rsqrt(_head_sum(q * q, bm) + L2_EPS) * (HEAD ** -0.5)
        k = k * lax.rsqrt(_head_sum(k * k, bm) + L2_EPS)
        g_tile = -jnp.exp(gvec[0:1]) * _softplus(gt + gvec[1:2])
        gam_tile = _mmh(jnp.where(_tri(CHUNK, d, False), 1.0, 0.0), g_tile)
        gam = _mmh(gam_tile, ea_ref[d])
        beta = _mmh(_sigmoid(gt), eb_ref[d])
        last = gam[CHUNK - 1:CHUNK] if d == 0 else gam[0:1]
        rr = _iota((2 * CHUNK, LANES), 0)
        ll = _iota((2 * CHUNK, LANES), 1)
        sel = ll == (6 * d + 2 * j + (rr >> 6))
        gd = _dup2(gam_tile)
        gam_c = jnp.sum(jnp.where(sel, gd, 0.0), axis=1, keepdims=True)
        gam_r = _mmh_nt(jnp.where(sel, 1.0, 0.0), gd)
        pair = jnp.exp(jnp.where(incl, gam_c - gam_r, -jnp.inf))
        kb = k * beta
        dk = _dup2(k)
        a2 = jnp.where(strict, _mm_nt(_stack2(kb), dk) * pair, 0.0)
        qk = _mm_nt(_stack2(q), dk) * pair
        t2 = _inv_unit_tri(a2, same_head, eye)
        eg = jnp.exp(gam)
        u = _papply(t2, v * beta)
        w = _papply(t2, kb * eg)
        st = st_ref[d]
        vn = u - _mm_nt(w, st)
        o = _mm_nt(q * eg, st) + _papply(qk, vn)
        st_ref[d] = st * jnp.exp(last) + jnp.where(bm > 0.5, _mm_tn(vn, k * jnp.exp(last - gam)), 0.0)
        yd_ref[d, pl.ds(start, CHUNK), :] = o

    def seq_body(i, carry):
        chunk_dir(0, i)
        chunk_dir(1, n_chunks - 1 - i)
        return carry

    lax.fori_loop(0, n_chunks, seq_body, 0)
    sfin_ref[0, 0] = st_ref[...]

    def post_body(c, carry):
        start = pl.multiple_of(c * CHUNK, CHUNK)
        o = yd_ref[0, pl.ds(start, CHUNK), :] + yd_ref[1, pl.ds(start, CHUNK), :]
        z = z_ref[pl.ds(start, CHUNK), :]
        ms = _head_sum(o * o, bm) * (1.0 / HEAD)
        y_ref[pl.ds(start, CHUNK), :] = o * lax.rsqrt(ms + RMS_EPS) * vec[0:1] * _silu(z)
        return carry

    lax.fori_loop(0, n_chunks, post_body, 0)


def _gdn_call(p, base_blk, n_seq, t_len, prm, s0):
    has_init = s0 is not None
    dn_t = DN_OFF // LANES
    def pspec(tile_fn):
        return pl.BlockSpec((t_len, LANES), lambda b, j: (base_blk + b, dn_t + tile_fn(j)))

    in_specs = [pspec(lambda j: j), pspec(lambda j: 3 + j), pspec(lambda j: 6 + j), pspec(lambda j: 9 + j),
                pspec(lambda j: 12),
                pl.BlockSpec((1, 8, 3 * LANES), lambda b, j: (j, 0, 0)),
                pl.BlockSpec((1, 8, LANES), lambda b, j: (j, 0, 0)),
                pl.BlockSpec((8, LANES), lambda b, j: (0, 0)),
                pl.BlockSpec((2, LANES, LANES), lambda b, j: (0, 0, j)),
                pl.BlockSpec((2, LANES, LANES), lambda b, j: (0, 0, j))]
    args = [p, p, p, p, p, prm["cw"], prm["vec"], prm["gvec"], prm["ea"], prm["eb"]]
    if has_init:
        in_specs.append(pl.BlockSpec((1, 1, 2, LANES, LANES), lambda b, j: (b, j, 0, 0, 0)))
        args.append(s0)
    y, sfin = pl.pallas_call(
        functools.partial(_gdn_kernel, t_len=t_len, has_init=has_init),
        grid=(n_seq, 3),
        in_specs=in_specs,
        out_specs=[pl.BlockSpec((t_len, LANES), lambda b, j: (b, j)),
                   pl.BlockSpec((1, 1, 2, LANES, LANES), lambda b, j: (b, j, 0, 0, 0))],
        out_shape=[jax.ShapeDtypeStruct((n_seq * t_len, DN_DIM), F32),
                   jax.ShapeDtypeStruct((n_seq, 3, 2, LANES, LANES), F32)],
        scratch_shapes=[pltpu.VMEM((3, t_len, LANES), F32),
                        pltpu.VMEM((2, t_len, LANES), F32),
                        pltpu.VMEM((2, LANES, LANES), F32)],
        compiler_params=pltpu.CompilerParams(dimension_semantics=("arbitrary", "arbitrary"),
                                             vmem_limit_bytes=VMEM_LIMIT),
        name="gdn_mix",
    )(*args)
    return y, sfin


def _gla_kernel(*refs, t_len, has_init):
    if has_init:
        (q_ref, k_ref, v_ref, og_ref, gt_ref, gk2_ref, vec_ref, s0_ref,
         y_ref, sfin_ref, yd_ref, st_ref) = refs
    else:
        (q_ref, k_ref, v_ref, og_ref, gt_ref, gk2_ref, vec_ref,
         y_ref, sfin_ref, yd_ref, st_ref) = refs
    n_chunks = t_len // CHUNK
    vec = vec_ref[...]
    sr = _iota((GLA_DIM, GLA_KDIM), 0)
    sc = _iota((GLA_DIM, GLA_KDIM), 1)
    st_mask = _shr(sr, GLA_DV) == _shr(sc, GLA_DK)

    if has_init:
        st_ref[...] = s0_ref[0]
    else:
        st_ref[...] = jnp.zeros_like(st_ref)

    def chunk_dir(d, c):
        start = pl.multiple_of(c * CHUNK, CHUNK)
        q = q_ref[pl.ds(start, CHUNK), :] * (GLA_DK ** -0.5)
        k = k_ref[pl.ds(start, CHUNK), :]
        v = v_ref[pl.ds(start, CHUNK), :]
        gt = gt_ref[pl.ds(start, CHUNK), :]
        gk = _log_sigmoid(_mm(gt, gk2_ref[d]) + vec[d:d + 1, :GLA_KDIM]) * (1.0 / GLA_GATE_NORM)
        b = _mmh(jnp.where(_tri(CHUNK, d, False), 1.0, 0.0), gk)
        tot = jnp.sum(gk, axis=0, keepdims=True)
        half = 0.5 * tot
        lane = _iota((CHUNK, GLA_KDIM), 1)
        qt = q * jnp.exp(b - half)
        q_stack = jnp.concatenate([jnp.where(_shr(lane, GLA_DK) == h, qt, 0.0) for h in range(GLA_HEADS)], axis=0)
        rr = _iota((GLA_HEADS * CHUNK, CHUNK), 0) & (CHUNK - 1)
        cc = _iota((GLA_HEADS * CHUNK, CHUNK), 1)
        causal = (cc <= rr) if d == 0 else (cc >= rr)
        scores = jnp.where(causal, _mm_nt(q_stack, k * jnp.exp(half - b)), 0.0)
        full = _mm(scores, v)
        vl = _shr(_iota((CHUNK, GLA_DIM), 1), GLA_DV)
        o = jnp.zeros((CHUNK, GLA_DIM), F32)
        for h in range(GLA_HEADS):
            o = o + jnp.where(vl == h, full[h * CHUNK:(h + 1) * CHUNK], 0.0)
        st = st_ref[d]
        o = o + _mm_nt(q * jnp.exp(b), st)
        st_ref[d] = st * jnp.exp(tot) + jnp.where(st_mask, _mm_tn(v, k * jnp.exp(tot - b)), 0.0)
        yd_ref[d, pl.ds(start, CHUNK), :] = o

    def seq_body(i, carry):
        chunk_dir(0, i)
        chunk_dir(1, n_chunks - 1 - i)
        return carry

    lax.fori_loop(0, n_chunks, seq_body, 0)
    sfin_ref[0] = st_ref[...]

    bm = _block_ones(GLA_DIM, GLA_DV)

    def post_body(c, carry):
        start = pl.multiple_of(c * CHUNK, CHUNK)
        o = yd_ref[0, pl.ds(start, CHUNK), :] + yd_ref[1, pl.ds(start, CHUNK), :]
        og = og_ref[pl.ds(start, CHUNK), :]
        ms = _mmh(o * o, bm) * (1.0 / GLA_DV)
        y_ref[pl.ds(start, CHUNK), :] = o * lax.rsqrt(ms + RMS_EPS) * vec[2:3] * _silu(og)
        return carry

    lax.fori_loop(0, n_chunks, post_body, 0)


def _gla_call(p, base_blk, n_seq, t_len, prm, s0):
    has_init = s0 is not None
    t128 = GLA_OFF // LANES
    t256 = GLA_OFF // GLA_DIM
    in_specs = [pl.BlockSpec((t_len, LANES), lambda b: (base_blk + b, t128)),
                pl.BlockSpec((t_len, LANES), lambda b: (base_blk + b, t128 + 1)),
                pl.BlockSpec((t_len, GLA_DIM), lambda b: (base_blk + b, t256 + 1)),
                pl.BlockSpec((t_len, GLA_DIM), lambda b: (base_blk + b, t256 + 2)),
                pl.BlockSpec((t_len, LANES), lambda b: (base_blk + b, t128 + 6)),
                pl.BlockSpec((2, LANES, LANES), lambda b: (0, 0, 0)),
                pl.BlockSpec((8, GLA_DIM), lambda b: (0, 0))]
    args = [p, p, p, p, p, prm["gk2"], prm["vec"]]
    if has_init:
        in_specs.append(pl.BlockSpec((1, 2, GLA_DIM, GLA_KDIM), lambda b: (b, 0, 0, 0)))
        args.append(s0)
    y, sfin = pl.pallas_call(
        functools.partial(_gla_kernel, t_len=t_len, has_init=has_init),
        grid=(n_seq,),
        in_specs=in_specs,
        out_specs=[pl.BlockSpec((t_len, GLA_DIM), lambda b: (b, 0)),
                   pl.BlockSpec((1, 2, GLA_DIM, GLA_KDIM), lambda b: (b, 0, 0, 0))],
        out_shape=[jax.ShapeDtypeStruct((n_seq * t_len, GLA_DIM), F32),
                   jax.ShapeDtypeStruct((n_seq, 2, GLA_DIM, GLA_KDIM), F32)],
        scratch_shapes=[pltpu.VMEM((2, t_len, GLA_DIM), F32),
                        pltpu.VMEM((2, GLA_DIM, GLA_KDIM), F32)],
        compiler_params=pltpu.CompilerParams(dimension_semantics=("arbitrary",), vmem_limit_bytes=VMEM_LIMIT),
        name="gla_mix",
    )(*args)
    return y, sfin


def _out_kernel(yr_ref, yg_ref, yd_ref, x_ref, g1_ref, sh_ref, sc_ref, nw_ref, wo_ref, wr_ref, br_ref,
                x1_ref, h2_ref, gate_ref):
    mix = (_mm(yr_ref[...], wo_ref[0:RW_DIM, :]) + _mm(yg_ref[...], wo_ref[RW_DIM:RW_DIM + GLA_DIM, :])
           + _mm(yd_ref[...], wo_ref[RW_DIM + GLA_DIM:, :]))
    x1 = x_ref[...] + g1_ref[0] * mix
    x1_ref[...] = x1
    h2 = x1 * lax.rsqrt(jnp.mean(x1 * x1, axis=-1, keepdims=True) + RMS_EPS) * nw_ref[...]
    h2 = h2 * (1.0 + sc_ref[0]) + sh_ref[0]
    h2_ref[...] = h2
    logits = _mmh(h2, wr_ref[...]) + br_ref[...]
    lane_i = _iota(logits.shape, 1)
    lane = lane_i.astype(F32)
    neg = -jnp.inf
    big = 1e9
    lg = jnp.where(lane_i < N_GROUPS, logits, neg)
    gmax = jnp.max(lg, axis=1, keepdims=True)
    gsel = jnp.min(jnp.where(lg == gmax, lane, big), axis=1, keepdims=True)
    p_grp = 1.0 / jnp.sum(jnp.exp(lg - gmax), axis=1, keepdims=True)
    e_lane = lane_i - N_GROUPS
    e_grp = _shr(jnp.maximum(e_lane, 0), EXPERTS_PER_GROUP).astype(F32)
    in_grp = (e_lane >= 0) & (e_lane < N_EXPERTS) & (e_grp == gsel)
    le = jnp.where(in_grp, logits, neg)
    v1 = jnp.max(le, axis=1, keepdims=True)
    i1 = jnp.min(jnp.where(le == v1, lane, big), axis=1, keepdims=True)
    le2 = jnp.where(lane == i1, neg, le)
    v2 = jnp.max(le2, axis=1, keepdims=True)
    i2 = jnp.min(jnp.where(le2 == v2, lane, big), axis=1, keepdims=True)
    e2 = jnp.exp(v2 - v1)
    p1 = p_grp / (1.0 + e2)
    p2 = p_grp * e2 / (1.0 + e2)
    gate_ref[...] = jnp.where(lane == i1, p1, jnp.where(lane == i2, p2, 0.0))


def _out_call(y_rw, y_gla, y_dn, x, mod3, layer, nw, w_out, wr, br, n_ctx_tok, lat_t, tile=256):
    n = x.shape[0]
    row = lambda i: layer * 8 + _cond_row(i, tile, n_ctx_tok, lat_t)
    def modspec(part):
        return pl.BlockSpec((1, 1, D_MODEL), lambda i: (row(i), 0, part))

    return pl.pallas_call(
        _out_kernel,
        grid=(n // tile,),
        in_specs=[pl.BlockSpec((tile, RW_DIM), lambda i: (i, 0)),
                  pl.BlockSpec((tile, GLA_DIM), lambda i: (i, 0)),
                  pl.BlockSpec((tile, DN_DIM), lambda i: (i, 0)),
                  pl.BlockSpec((tile, D_MODEL), lambda i: (i, 0)),
                  modspec(2), modspec(3), modspec(4),
                  pl.BlockSpec((1, D_MODEL), lambda i: (0, 0)),
                  pl.BlockSpec((D_MODEL, D_MODEL), lambda i: (0, 0)),
                  pl.BlockSpec((D_MODEL, LANES), lambda i: (0, 0)),
                  pl.BlockSpec((1, LANES), lambda i: (0, 0))],
        out_specs=[pl.BlockSpec((tile, D_MODEL), lambda i: (i, 0)),
                   pl.BlockSpec((tile, D_MODEL), lambda i: (i, 0)),
                   pl.BlockSpec((tile, LANES), lambda i: (i, 0))],
        out_shape=[jax.ShapeDtypeStruct((n, D_MODEL), F32),
                   jax.ShapeDtypeStruct((n, D_MODEL), F32),
                   jax.ShapeDtypeStruct((n, LANES), F32)],
        compiler_params=pltpu.CompilerParams(dimension_semantics=("arbitrary",), vmem_limit_bytes=VMEM_LIMIT),
        name="out_proj_router",
    )(y_rw, y_gla, y_dn, x, mod3, mod3, mod3, nw.reshape(1, D_MODEL), w_out, wr, br)


def _moe_kernel(h_ref, gate_ref, x_ref, g2_ref, wg_ref, wu_ref, wd_ref, o_ref, acc_ref):
    e = pl.program_id(1)

    @pl.when(e == 0)
    def _():
        acc_ref[...] = jnp.zeros_like(acc_ref)

    h = h_ref[...].astype(BF16)
    gates = gate_ref[...]
    lane = _iota(gates.shape, 1)
    gcol = jnp.sum(jnp.where(lane == N_GROUPS + e, gates, 0.0), axis=1, keepdims=True)
    hid = _silu(jnp.dot(h, wg_ref[0].astype(BF16), preferred_element_type=F32))
    hid = hid * jnp.dot(h, wu_ref[0].astype(BF16), preferred_element_type=F32) * gcol
    acc_ref[...] += _mm(hid, wd_ref[0])

    @pl.when(e == N_EXPERTS - 1)
    def _():
        o_ref[...] = x_ref[...] + g2_ref[0] * acc_ref[...]


def _moe_call(h2, gates, x1, mod3, layer, w_gate, w_up, w_down, n_ctx_tok, lat_t, tile=1024):
    n = h2.shape[0]
    row = lambda i: layer * 8 + _cond_row(i, tile, n_ctx_tok, lat_t)
    return pl.pallas_call(
        _moe_kernel,
        grid=(n // tile, N_EXPERTS),
        in_specs=[pl.BlockSpec((tile, D_MODEL), lambda i, e: (i, 0)),
                  pl.BlockSpec((tile, LANES), lambda i, e: (i, 0)),
                  pl.BlockSpec((tile, D_MODEL), lambda i, e: (i, 0)),
                  pl.BlockSpec((1, 1, D_MODEL), lambda i, e: (row(i), 0, 5)),
                  pl.BlockSpec((1, D_MODEL, D_EXPERT), lambda i, e: (layer * N_EXPERTS + e, 0, 0)),
                  pl.BlockSpec((1, D_MODEL, D_EXPERT), lambda i, e: (layer * N_EXPERTS + e, 0, 0)),
                  pl.BlockSpec((1, D_EXPERT, D_MODEL), lambda i, e: (layer * N_EXPERTS + e, 0, 0))],
        out_specs=pl.BlockSpec((tile, D_MODEL), lambda i, e: (i, 0)),
        out_shape=jax.ShapeDtypeStruct((n, D_MODEL), F32),
        scratch_shapes=[pltpu.VMEM((tile, D_MODEL), F32)],
        compiler_params=pltpu.CompilerParams(dimension_semantics=("arbitrary", "arbitrary"),
                                             vmem_limit_bytes=VMEM_LIMIT),
        name="moe_dense",
    )(h2, gates, x1, mod3, w_gate, w_up, w_down)


def _norm_kernel(x_ref, w_ref, o_ref):
    x = x_ref[...]
    o_ref[...] = x * lax.rsqrt(jnp.mean(x * x, axis=-1, keepdims=True) + RMS_EPS) * w_ref[...]


def _norm_call(x, w, base_tile, n_tiles, tile=256):
    return pl.pallas_call(
        _norm_kernel,
        grid=(n_tiles,),
        in_specs=[pl.BlockSpec((tile, D_MODEL), lambda i: (base_tile + i, 0)),
                  pl.BlockSpec((1, D_MODEL), lambda i: (0, 0))],
        out_specs=pl.BlockSpec((tile, D_MODEL), lambda i: (i, 0)),
        out_shape=jax.ShapeDtypeStruct((n_tiles * tile, D_MODEL), F32),
        compiler_params=pltpu.CompilerParams(dimension_semantics=("arbitrary",), vmem_limit_bytes=VMEM_LIMIT),
        name="final_norm",
    )(x, w.reshape(1, D_MODEL))


def _pos_embed(n_tokens, dim):
    rows = n_tokens // GRID_W
    row = jnp.broadcast_to(jnp.arange(rows)[:, None], (rows, GRID_W)).reshape(-1)
    col = jnp.broadcast_to(jnp.arange(GRID_W)[None, :], (rows, GRID_W)).reshape(-1)
    quarter = dim // 4
    omega = 1.0 / (POS_BASE ** (jnp.arange(quarter, dtype=F32) / quarter))

    def axis_embed(pos):
        ang = pos.astype(F32)[:, None] * omega[None, :]
        return jnp.concatenate([jnp.sin(ang), jnp.cos(ang)], axis=-1)

    return jnp.concatenate([axis_embed(row), axis_embed(col)], axis=-1)


def _relayout_w_in(w_in_l):
    rw = w_in_l[:, :RW_COLS]
    g0 = RW_COLS
    q, k, v = w_in_l[:, g0:g0 + 128], w_in_l[:, g0 + 128:g0 + 256], w_in_l[:, g0 + 256:g0 + 512]
    gates = w_in_l[:, g0 + 512:g0 + 544]
    og = w_in_l[:, g0 + 544:g0 + 800]
    d0 = g0 + 800
    qkvz = w_in_l[:, d0:d0 + 1536]
    dgates = w_in_l[:, d0 + 1536:d0 + 1560]
    z = lambda n: jnp.zeros((D_MODEL, n), w_in_l.dtype)
    return jnp.concatenate([rw, q, k, v, og, gates, z(96), qkvz, dgates, z(104)], axis=1).astype(BF16)


def _pair_cols(x):
    return jnp.moveaxis(x.reshape(x.shape[:-1] + (3, LANES)), -2, 0)


def _rwkv_params(mu, w0, w2, a0, a2, g2, kk, ka, rk, ln_w, ln_b):
    tiles = mu.reshape(2, 12, LANES)
    mu_p = jnp.stack([jnp.concatenate([tiles[:, j], tiles[:, 3 + j], tiles[:, 6 + j], tiles[:, 9], tiles[:, 10],
                                       tiles[:, 11]], axis=-1) for j in range(3)])
    rows = [w0[0], w0[1], a0[0], a0[1], kk, ka, rk, ln_w, ln_b]
    vec = jnp.stack(rows + [jnp.zeros_like(kk)] * 7)
    vec = _pair_cols(vec)
    zero = jnp.zeros((64, RW_DIM), F32)
    w2c = jnp.stack([jnp.concatenate([w2[0], zero]), jnp.concatenate([zero, w2[1]])])
    a2c = jnp.stack([jnp.concatenate([a2[0], zero]), jnp.concatenate([zero, a2[1]])])
    return {"mu": mu_p, "vec": vec, "w2": w2c, "a2": a2c, "g2": g2}


def _gdn_consts():
    ea = np.zeros((2, LANES, DN_DIM), np.float32)
    eb = np.zeros((2, LANES, DN_DIM), np.float32)
    for d in range(2):
        for h in range(DN_HEADS):
            ea[d, 6 * d + h, h * HEAD:(h + 1) * HEAD] = 1.0
            eb[d, 12 + 6 * d + h, h * HEAD:(h + 1) * HEAD] = 1.0
    return jnp.asarray(ea), jnp.asarray(eb)


def _gdn_params(conv, a_log, dt_bias, norm_w):
    tiles = conv.reshape(3, 9, LANES)
    cw = jnp.stack([jnp.concatenate([tiles[:, j], tiles[:, 3 + j], tiles[:, 6 + j]], axis=-1) for j in range(3)])
    cw = jnp.concatenate([cw, jnp.zeros((3, 5, 3 * LANES), F32)], axis=1)
    vec = jnp.zeros((8, DN_DIM), F32).at[0].set(jnp.tile(norm_w, DN_HEADS))
    vec = _pair_cols(vec)
    gvec = jnp.zeros((8, LANES), F32)
    gvec = gvec.at[0, 0:12].set(a_log.reshape(12)).at[1, 0:12].set(dt_bias.reshape(12))
    ea, eb = _gdn_consts()
    return {"cw": cw, "vec": vec, "gvec": gvec, "ea": ea, "eb": eb}


def _gla_params(gk2, gk_b, norm_w):
    g = jnp.zeros((2, LANES, LANES), F32)
    g = g.at[0, 0:16].set(gk2[0]).at[1, 16:32].set(gk2[1])
    vec = jnp.zeros((8, GLA_DIM), F32)
    vec = vec.at[0, :GLA_KDIM].set(gk_b[0]).at[1, :GLA_KDIM].set(gk_b[1]).at[2].set(jnp.tile(norm_w, GLA_HEADS))
    return {"gk2": g, "vec": vec}


def _pair_state_in(s):
    b = s.shape[0]
    st = jnp.swapaxes(s, -1, -2).reshape(b, 2, 3, 2, HEAD, HEAD)
    eye = jnp.eye(2, dtype=s.dtype)
    bd = jnp.einsum("bdjhvk,hg->bjdhvgk", st, eye)
    return bd.reshape(b, 3, 2, LANES, LANES)


def _pair_state_out(st):
    b = st.shape[0]
    x = st.reshape(b, 3, 2, 2, HEAD, 2, HEAD)
    diag = jnp.stack([x[:, :, :, 0, :, 0, :], x[:, :, :, 1, :, 1, :]], axis=3)
    diag = jnp.transpose(diag, (0, 2, 1, 3, 5, 4))
    return diag.reshape(b, 2, 6, HEAD, HEAD)


def _gla_state_in(s):
    b = s.shape[0]
    st = jnp.swapaxes(s, -1, -2)
    eye = jnp.eye(GLA_HEADS, dtype=s.dtype)
    bd = jnp.einsum("bdhvk,hg->bdhvgk", st, eye)
    return bd.reshape(b, 2, GLA_DIM, GLA_KDIM)


def _gla_state_out(st):
    b = st.shape[0]
    x = st.reshape(b, 2, GLA_HEADS, GLA_DV, GLA_HEADS, GLA_DK)
    diag = jnp.stack([x[:, :, h, :, h, :] for h in range(GLA_HEADS)], axis=2)
    return jnp.swapaxes(diag, -1, -2)


def kernel(x_prompt, x_sample, c, state_rwkv, state_gla, state_delta, c_ctx, ada_w, ada_b, norm1_w, norm2_w, final_norm_w, w_in, w_out, rwkv_mu, rwkv_w0, rwkv_w2, rwkv_a0, rwkv_a2, rwkv_g2, rwkv_kk, rwkv_ka, rwkv_rk, rwkv_ln_w, rwkv_ln_b, gla_gk2, gla_gk_b, gla_norm_w, dn_conv, dn_a_log, dn_dt_bias, dn_norm_w, moe_wg, moe_bg, moe_we, moe_be, moe_w_gate, moe_w_up, moe_w_down):
    n_ctx, t_ctx, d = x_prompt.shape
    n_lat, t_lat, _ = x_sample.shape
    depth = ada_w.shape[0]
    n_ctx_tok = n_ctx * t_ctx
    n_lat_tok = n_lat * t_lat
    assert d == D_MODEL and n_ctx_tok % 1024 == 0 and t_lat % 1024 == 0 and t_ctx % 256 == 0 and n_lat <= 7
    assert n_ctx_tok % t_lat == 0

    xs = x_sample + _pos_embed(t_lat, d)[None].astype(x_sample.dtype)
    x = jnp.concatenate([x_prompt.reshape(n_ctx_tok, d), xs.reshape(n_lat_tok, d)], axis=0)

    cond8 = jnp.zeros((8, d), F32).at[0].set(c_ctx).at[1:1 + n_lat].set(c)
    mod = _ada_call(cond8, ada_w, ada_b)
    mod3 = mod.reshape(depth * 8, 1, 6 * d)

    w_gate = moe_w_gate.reshape(depth * N_EXPERTS, d, D_EXPERT)
    w_up = moe_w_up.reshape(depth * N_EXPERTS, d, D_EXPERT)
    w_down = moe_w_down.reshape(depth * N_EXPERTS, D_EXPERT, d)

    lat_base = n_ctx_tok // t_lat
    rw_states, gla_states, dn_states = [], [], []
    for l in range(depth):
        p = _in_call(x, mod3, l, norm1_w[l], _relayout_w_in(w_in[l]), n_ctx_tok, t_lat)
        rw_p = _rwkv_params(rwkv_mu[l], rwkv_w0[l], rwkv_w2[l], rwkv_a0[l], rwkv_a2[l], rwkv_g2[l], rwkv_kk[l],
                            rwkv_ka[l], rwkv_rk[l], rwkv_ln_w[l], rwkv_ln_b[l])
        dn_p = _gdn_params(dn_conv[l], dn_a_log[l], dn_dt_bias[l], dn_norm_w[l])
        gla_p = _gla_params(gla_gk2[l], gla_gk_b[l], gla_norm_w[l])

        y_rw_c, s_rw = _rwkv_call(p, 0, n_ctx, t_ctx, rw_p, None)
        y_rw_l, _ = _rwkv_call(p, lat_base, n_lat, t_lat, rw_p, _pair_state_in(state_rwkv[:, l]))
        y_dn_c, s_dn = _gdn_call(p, 0, n_ctx, t_ctx, dn_p, None)
        y_dn_l, _ = _gdn_call(p, lat_base, n_lat, t_lat, dn_p, _pair_state_in(state_delta[:, l]))
        y_gla_c, s_gla = _gla_call(p, 0, n_ctx, t_ctx, gla_p, None)
        y_gla_l, _ = _gla_call(p, lat_base, n_lat, t_lat, gla_p, _gla_state_in(state_gla[:, l]))
        rw_states.append(_pair_state_out(s_rw))
        dn_states.append(_pair_state_out(s_dn))
        gla_states.append(_gla_state_out(s_gla))

        wr = jnp.zeros((d, LANES), F32)
        wr = wr.at[:, :N_GROUPS].set(moe_wg[l]).at[:, N_GROUPS:N_GROUPS + N_EXPERTS].set(moe_we[l].reshape(d, N_EXPERTS))
        br = jnp.zeros((1, LANES), F32)
        br = br.at[0, :N_GROUPS].set(moe_bg[l]).at[0, N_GROUPS:N_GROUPS + N_EXPERTS].set(moe_be[l].reshape(N_EXPERTS))
        x1, h2, gates = _out_call(jnp.concatenate([y_rw_c, y_rw_l]), jnp.concatenate([y_gla_c, y_gla_l]),
                                  jnp.concatenate([y_dn_c, y_dn_l]), x, mod3, l, norm2_w[l],
                                  w_out[l].astype(BF16), wr, br, n_ctx_tok, t_lat)
        x = _moe_call(h2, gates, x1, mod3, l, w_gate, w_up, w_down, n_ctx_tok, t_lat)

    y_prompt = _norm_call(x, final_norm_w, 0, n_ctx_tok // 256).reshape(n_ctx, t_ctx, d)
    y_sample = _norm_call(x, final_norm_w, n_ctx_tok // 256, n_lat_tok // 256).reshape(n_lat, t_lat, d)
    new_rw = jnp.stack(rw_states, axis=1).astype(x_prompt.dtype)
    new_gla = jnp.stack(gla_states, axis=1).astype(x_prompt.dtype)
    new_dn = jnp.stack(dn_states, axis=1).astype(x_prompt.dtype)
    return (y_prompt, y_sample, new_rw, new_gla, new_dn)
```

```python
import functools
import math

import numpy as np
import jax
import jax.numpy as jnp
from jax import lax
from jax.experimental import pallas as pl
from jax.experimental.pallas import tpu as pltpu

F32 = jnp.float32
BF16 = jnp.bfloat16
HI = lax.Precision.HIGHEST

D_MODEL = 1024
DEPTH = 4
GRID_W = 64
POS_BASE = 10000.0
RMS_EPS = 1e-6
L2_EPS = 1e-6

RW_HEADS = 6
RW_DIM = 384
RW_COLS = 1536
RW_LN_EPS = 64e-5
GLA_HEADS = 4
GLA_DK = 32
GLA_DV = 64
GLA_KDIM = 128
GLA_DIM = 256
GLA_GATE_RANK = 16
GLA_GATE_NORM = 16.0
DN_HEADS = 6
DN_DIM = 384
N_GROUPS = 4
EXPERTS_PER_GROUP = 4
N_EXPERTS = 16
D_EXPERT = 512

LANES = 128
CHUNK = 64
HEAD = 64
PAIRS_W = 384
DN_OFF = RW_COLS
GLA_OFF = DN_OFF + 4 * PAIRS_W
GLA_PCOLS = 896
DN_GATE_OFF = GLA_OFF + GLA_PCOLS
P_COLS = DN_GATE_OFF + LANES
VMEM_LIMIT = 56 * 1024 * 1024


def _mm(a, b):
    return jnp.dot(a.astype(BF16), b.astype(BF16), preferred_element_type=F32)


def _mm_nt(a, b):
    return lax.dot_general(a.astype(BF16), b.astype(BF16), (((1,), (1,)), ((), ())),
                           preferred_element_type=F32)


def _mm_tn(a, b):
    return lax.dot_general(a.astype(BF16), b.astype(BF16), (((0,), (0,)), ((), ())),
                           preferred_element_type=F32)


def _mmh(a, b):
    return jnp.dot(a, b, precision=HI, preferred_element_type=F32)


def _mmh_nt(a, b):
    return lax.dot_general(a, b, (((1,), (1,)), ((), ())), precision=HI, preferred_element_type=F32)


def _iota(shape, dim):
    return lax.broadcasted_iota(jnp.int32, shape, dim)


def _sigmoid(x):
    return 1.0 / (1.0 + jnp.exp(-x))


def _silu(x):
    return x * _sigmoid(x)


def _softplus(x):
    return jnp.maximum(x, 0.0) + jnp.log(1.0 + jnp.exp(-jnp.abs(x)))


def _log_sigmoid(x):
    return -_softplus(-x)


def _tri(n, d, strict):
    r = _iota((n, n), 0)
    c = _iota((n, n), 1)
    if d == 0:
        return (c < r) if strict else (c <= r)
    return (c > r) if strict else (c >= r)


QUAD = 4 * HEAD


def _quad_masks():
    r = _iota((QUAD, QUAD), 0)
    c = _iota((QUAD, QUAD), 1)
    t = r & (CHUNK - 1)
    s = c & (CHUNK - 1)
    fwd = r < 2 * HEAD
    bwd = jnp.logical_not(fwd)
    strict = (fwd & (s < t)) | (bwd & (s > t))
    incl = strict | (s == t)
    same_head = (r >> 6) == (c >> 6)
    return incl, strict, same_head, t, s


def _qstack(a):
    grp = _iota(a.shape, 1) >> 6
    return jnp.concatenate([jnp.where(grp == v, a, 0.0) for v in range(4)], axis=0)


def _qdup(a):
    return jnp.concatenate([a, a, a, a], axis=0)


def _qdiag(r):
    grp = _iota((CHUNK, QUAD), 1) >> 6
    out = jnp.where(grp == 0, r[0:CHUNK], 0.0)
    for v in range(1, 4):
        out = jnp.where(grp == v, r[v * CHUNK:(v + 1) * CHUNK], out)
    return out


def _qapply(a2, x):
    return _qdiag(_mm(a2[:, :CHUNK], x))


def _inv_unit_tri(l2, same_head, t, s):
    same_blk = lambda width: _shr(t, width) == _shr(s, width)
    per_head = lambda x: jnp.where(same_head, x, 0.0)
    n0 = jnp.where(same_blk(8), -l2, 0.0)
    p1 = _mm(per_head(n0), n0)
    p2 = _mm(per_head(p1), p1)
    inv = jnp.where(s == t, 1.0, 0.0) + n0
    inv = inv + _mm(per_head(inv), p1)
    inv = inv + _mm(per_head(inv), p2)
    for width in (8, 16, 32):
        off = jnp.where(same_blk(2 * width) & jnp.logical_not(same_blk(width)), l2, 0.0)
        inv = inv - _mm(per_head(_mm(per_head(inv), off)), inv)
    return inv


def _split2(x):
    hi = x.astype(BF16)
    return hi, (x - hi.astype(F32)).astype(BF16)


def _head_sums(xs, bm):
    tiles = [x[:, j * LANES:(j + 1) * LANES] for x in xs for j in range(3)]
    parts = [p for tl in tiles for p in _split2(tl)]
    out = jnp.dot(jnp.concatenate(parts, axis=0), bm, preferred_element_type=F32)
    res = []
    for i in range(len(xs)):
        cols = []
        for j in range(3):
            base = (i * 3 + j) * 2 * CHUNK
            cols.append(out[base:base + CHUNK] + out[base + CHUNK:base + 2 * CHUNK])
        res.append(jnp.concatenate(cols, axis=1))
    return res


def _cumsum_rows(x, d):
    row = _iota(x.shape, 0)
    k = 1
    while k < CHUNK:
        if d == 0:
            x = x + jnp.where(row >= k, pltpu.roll(x, k, 0), 0.0)
        else:
            x = x + jnp.where(row < CHUNK - k, pltpu.roll(x, CHUNK - k, 0), 0.0)
        k *= 2
    return x


def _shr(x, width):
    return x >> (width.bit_length() - 1)


def _block_ones(n, width):
    r = _iota((n, n), 0)
    c = _iota((n, n), 1)
    return jnp.where(_shr(r, width) == _shr(c, width), 1.0, 0.0).astype(F32)


def _shifted(ref, start, c, n_chunks, t_len):
    pc = ref[pl.ds(start, CHUNK), :]
    pb = ref[pl.ds(pl.multiple_of(jnp.maximum(start - 8, 0), 8), 8), :]
    nb = ref[pl.ds(pl.multiple_of(jnp.minimum(start + CHUNK, t_len - 8), 8), 8), :]
    carry_p = jnp.where(c > 0, pb[7:8, :], 0.0)
    carry_n = jnp.where(c < n_chunks - 1, nb[0:1, :], 0.0)
    row = _iota(pc.shape, 0)
    prev = jnp.where(row == 0, carry_p, pltpu.roll(pc, 1, 0))
    nxt = jnp.where(row == CHUNK - 1, carry_n, pltpu.roll(pc, CHUNK - 1, 0))
    return pc, prev, nxt


def _ada_kernel(c_ref, w_ref, b_ref, o_ref):
    c = c_ref[...]
    o_ref[0] = _mm(_silu(c), w_ref[0]) + b_ref[0]


def _ada_call(cond8, ada_w, ada_b):
    nl = ada_w.shape[0]
    return pl.pallas_call(
        _ada_kernel,
        grid=(nl, 6),
        in_specs=[pl.BlockSpec((8, D_MODEL), lambda l, j: (0, 0)),
                  pl.BlockSpec((1, D_MODEL, D_MODEL), lambda l, j: (l, 0, j)),
                  pl.BlockSpec((1, 1, D_MODEL), lambda l, j: (l, 0, j))],
        out_specs=pl.BlockSpec((1, 8, D_MODEL), lambda l, j: (l, 0, j)),
        out_shape=jax.ShapeDtypeStruct((nl, 8, 6 * D_MODEL), F32),
        compiler_params=pltpu.CompilerParams(dimension_semantics=("arbitrary", "arbitrary"),
                                             vmem_limit_bytes=VMEM_LIMIT),
        name="ada_mod",
    )(cond8, ada_w, ada_b.reshape(nl, 1, 6 * D_MODEL))


def _cond_row(i, tile, n_ctx_tok, lat_t):
    tok = i * tile
    return jnp.where(tok < n_ctx_tok, 0, 1 + (tok - n_ctx_tok) // lat_t)


def _in_kernel(x_ref, sh_ref, sc_ref, nw_ref, w_ref, o_ref):
    x = x_ref[...]
    y = x * lax.rsqrt(jnp.mean(x * x, axis=-1, keepdims=True) + RMS_EPS) * nw_ref[...]
    h = y * (1.0 + sc_ref[0]) + sh_ref[0]
    o_ref[...] = jnp.dot(h.astype(BF16), w_ref[...], preferred_element_type=F32)


def _in_call(x, mod3, layer, nw, w_in_p, n_ctx_tok, lat_t, tile=256):
    n = x.shape[0]
    row = lambda i: layer * 8 + _cond_row(i, tile, n_ctx_tok, lat_t)
    return pl.pallas_call(
        _in_kernel,
        grid=(n // tile,),
        in_specs=[pl.BlockSpec((tile, D_MODEL), lambda i: (i, 0)),
                  pl.BlockSpec((1, 1, D_MODEL), lambda i: (row(i), 0, 0)),
                  pl.BlockSpec((1, 1, D_MODEL), lambda i: (row(i), 0, 1)),
                  pl.BlockSpec((1, D_MODEL), lambda i: (0, 0)),
                  pl.BlockSpec((D_MODEL, P_COLS), lambda i: (0, 0))],
        out_specs=pl.BlockSpec((tile, P_COLS), lambda i: (i, 0)),
        out_shape=jax.ShapeDtypeStruct((n, P_COLS), F32),
        compiler_params=pltpu.CompilerParams(dimension_semantics=("arbitrary",), vmem_limit_bytes=VMEM_LIMIT),
        name="in_proj",
    )(x, mod3, mod3, nw.reshape(1, D_MODEL), w_in_p)


EXP_M05 = math.exp(-0.5)


def _rwkv_kernel(*refs, t_len, has_init):
    if has_init:
        (r_ref, k_ref, v_ref, x3_ref, mu_ref, vec_ref, w2_ref, a2_ref, g2_ref, s0_ref,
         y_ref, sfin_ref, xs_ref, yd_ref, st_ref) = refs
    else:
        (r_ref, k_ref, v_ref, x3_ref, mu_ref, vec_ref, w2_ref, a2_ref, g2_ref,
         y_ref, sfin_ref, xs_ref, yd_ref, st_ref) = refs
    n_chunks = t_len // CHUNK
    vec = vec_ref[...]
    bm = _block_ones(LANES, HEAD).astype(BF16)

    def shift_body(c, carry):
        start = pl.multiple_of(c * CHUNK, CHUNK)
        for idx, ref in enumerate((r_ref, k_ref, v_ref, x3_ref)):
            pc, prev, nxt = _shifted(ref, start, c, n_chunks, t_len)
            mu0 = mu_ref[0:1, idx * PAIRS_W:(idx + 1) * PAIRS_W]
            mu1 = mu_ref[1:2, idx * PAIRS_W:(idx + 1) * PAIRS_W]
            xs_ref[idx, pl.ds(start, CHUNK), :] = pc + mu0 * (prev - pc) + mu1 * (nxt - pc)
        return carry

    lax.fori_loop(0, n_chunks, shift_body, 0)

    if has_init:
        st_ref[...] = s0_ref[0]
    else:
        st_ref[...] = jnp.zeros_like(st_ref)

    def prep(d, start):
        k = xs_ref[1, pl.ds(start, CHUNK), :]
        x3 = xs_ref[3, pl.ds(start, CHUNK), :]
        xw, xa = x3[:, 0:LANES], x3[:, LANES:2 * LANES]
        w_raw = vec[d:d + 1] + _mm(jnp.tanh(xw), w2_ref[d])
        lw = -EXP_M05 * _sigmoid(w_raw)
        a = _sigmoid(vec[2 + d:3 + d] + _mm(xa, a2_ref[d]))
        kmod = k * (1.0 + (a - 1.0) * vec[5:6])
        cum = _cumsum_rows(lw, d)
        half = 0.5 * jnp.sum(lw, axis=0, keepdims=True)
        return k * vec[4:5], a, kmod, cum, lw, half

    def seq_body(i, carry):
        starts = (pl.multiple_of(i * CHUNK, CHUNK), pl.multiple_of((n_chunks - 1 - i) * CHUNK, CHUNK))
        pre = [prep(d, starts[d]) for d in range(2)]
        norms = _head_sums([pre[0][0] * pre[0][0], pre[1][0] * pre[1][0]], bm)
        fac = []
        for d in range(2):
            kk, a, kmod, cum, lw, half = pre[d]
            r = xs_ref[0, pl.ds(starts[d], CHUNK), :]
            kk = kk * lax.rsqrt(norms[d] + L2_EPS)
            em = jnp.exp(half)
            e_up = jnp.exp(half - cum)
            rh = r * jnp.exp(cum - half)
            ch = kk * jnp.exp(cum - lw - half)
            bh = kk * a * e_up
            kh = kmod * e_up
            v = xs_ref[2, pl.ds(starts[d], CHUNK), :]
            fac.append(dict(rh=rh, ch=ch, bh=bh, kh=kh, c0=ch * em, r0=rh * em, bt=bh * em, kt=kh * em, v=v,
                            dec=jnp.broadcast_to(em * em, (CHUNK, PAIRS_W))))
        incl, strict, same_head, t_idx, s_idx = _quad_masks()
        ys = []
        for j in range(3):
            ln = slice(j * LANES, (j + 1) * LANES)
            q = lambda name: jnp.concatenate([fac[0][name][:, ln], fac[1][name][:, ln]], axis=1)
            ch, rh, bh, kh, v = q("ch"), q("rh"), q("bh"), q("kh"), q("v")
            g = _mm_nt(jnp.concatenate([_qstack(ch), _qstack(rh)], axis=0),
                       jnp.concatenate([_qdup(bh), _qdup(kh)], axis=0))
            l_cb = jnp.where(strict, g[:QUAD, :QUAD], 0.0)
            l_ck = jnp.where(strict, g[:QUAD, QUAD:], 0.0)
            m_rb = jnp.where(incl, g[QUAD:, :QUAD], 0.0)
            m_rk = jnp.where(incl, g[QUAD:, QUAD:], 0.0)
            t2 = _inv_unit_tri(l_cb, same_head, t_idx, s_idx)
            lv_y0 = _mm(jnp.concatenate([l_ck[:, :CHUNK], m_rk[:, :CHUNK]], axis=0), v)
            lv, y0 = _qdiag(lv_y0[:QUAD]), _qdiag(lv_y0[QUAD:])
            wu = _mm(t2[:, :CHUNK], jnp.concatenate([q("c0"), lv], axis=1))
            w_bar, u_bar = _qdiag(wu[:, :QUAD]), -_qdiag(wu[:, QUAD:])
            st = st_ref[j]
            wr = _mm_nt(jnp.concatenate([w_bar, q("r0")], axis=0), st)
            u = u_bar - wr[:CHUNK]
            ys.append(y0 + wr[CHUNK:] + _qapply(m_rb, u))
            upd = _mm_tn(jnp.concatenate([u, v], axis=0), jnp.concatenate([q("bt"), q("kt")], axis=0))
            st_ref[j] = st * q("dec")[0:1] + jnp.where(same_head, upd, 0.0)
        for d in range(2):
            yd_ref[d, pl.ds(starts[d], CHUNK), :] = jnp.concatenate(
                [y[:, d * LANES:(d + 1) * LANES] for y in ys], axis=1)
        return carry

    lax.fori_loop(0, n_chunks, seq_body, 0)
    sfin_ref[0] = st_ref[...]

    def post_body(c, carry):
        start = pl.multiple_of(c * CHUNK, CHUNK)
        r = xs_ref[0, pl.ds(start, CHUNK), :]
        k = xs_ref[1, pl.ds(start, CHUNK), :]
        v = xs_ref[2, pl.ds(start, CHUNK), :]
        xg = xs_ref[3, pl.ds(start, CHUNK), 2 * LANES:3 * LANES]
        y = yd_ref[0, pl.ds(start, CHUNK), :] + yd_ref[1, pl.ds(start, CHUNK), :]
        mean, rk = _head_sums([y, r * k * vec[6:7]], bm)
        yc = y - mean * (1.0 / HEAD)
        var = _head_sums([yc * yc], bm)[0] * (1.0 / HEAD)
        yn = yc * lax.rsqrt(var + RW_LN_EPS) * vec[7:8] + vec[8:9]
        g = _mm(_sigmoid(xg), g2_ref[...])
        y_ref[pl.ds(start, CHUNK), :] = (yn + rk * v) * g
        return carry

    lax.fori_loop(0, n_chunks, post_body, 0)


def _rwkv_call(p, base_blk, n_seq, t_len, prm, s0):
    has_init = s0 is not None

    def pspec(col_blk):
        return pl.BlockSpec((t_len, PAIRS_W), lambda b: (base_blk + b, col_blk))

    full = lambda shape: pl.BlockSpec(shape, lambda b: (0,) * len(shape))
    in_specs = [pspec(0), pspec(1), pspec(2), pspec(3),
                full((2, RW_COLS)), full((16, PAIRS_W)), full((2, LANES, PAIRS_W)), full((2, LANES, PAIRS_W)),
                full((LANES, PAIRS_W))]
    args = [p, p, p, p, prm["mu"], prm["vec"], prm["w2"], prm["a2"], prm["g2"]]
    if has_init:
        in_specs.append(pl.BlockSpec((1, 3, QUAD, QUAD), lambda b: (b, 0, 0, 0)))
        args.append(s0)
    y, sfin = pl.pallas_call(
        functools.partial(_rwkv_kernel, t_len=t_len, has_init=has_init),
        grid=(n_seq,),
        in_specs=in_specs,
        out_specs=[pl.BlockSpec((t_len, PAIRS_W), lambda b: (b, 0)),
                   pl.BlockSpec((1, 3, QUAD, QUAD), lambda b: (b, 0, 0, 0))],
        out_shape=[jax.ShapeDtypeStruct((n_seq * t_len, RW_DIM), F32),
                   jax.ShapeDtypeStruct((n_seq, 3, QUAD, QUAD), F32)],
        scratch_shapes=[pltpu.VMEM((4, t_len, PAIRS_W), F32),
                        pltpu.VMEM((2, t_len, PAIRS_W), F32),
                        pltpu.VMEM((3, QUAD, QUAD), F32)],
        compiler_params=pltpu.CompilerParams(dimension_semantics=("arbitrary",), vmem_limit_bytes=VMEM_LIMIT),
        name="rwkv7_mix",
    )(*args)
    return y, sfin


def _gdn_kernel(*refs, t_len, has_init):
    if has_init:
        (q_ref, k_ref, v_ref, z_ref, gt_ref, cw_ref, vec_ref, gvec_ref, s0_ref,
         y_ref, sfin_ref, xs_ref, yd_ref, st_ref) = refs
    else:
        (q_ref, k_ref, v_ref, z_ref, gt_ref, cw_ref, vec_ref, gvec_ref,
         y_ref, sfin_ref, xs_ref, yd_ref, st_ref) = refs
    n_chunks = t_len // CHUNK
    vec = vec_ref[...]
    gvec = gvec_ref[...]
    bm = _block_ones(LANES, HEAD).astype(BF16)

    def conv_body(c, carry):
        start = pl.multiple_of(c * CHUNK, CHUNK)
        for idx, ref in enumerate((q_ref, k_ref, v_ref)):
            pc, prev, nxt = _shifted(ref, start, c, n_chunks, t_len)
            cw = cw_ref[:, idx * PAIRS_W:(idx + 1) * PAIRS_W]
            xs_ref[idx, pl.ds(start, CHUNK), :] = _silu(cw[0:1] * prev + cw[1:2] * pc + cw[2:3] * nxt)
        return carry

    lax.fori_loop(0, n_chunks, conv_body, 0)

    if has_init:
        st_ref[...] = s0_ref[0]
    else:
        st_ref[...] = jnp.zeros_like(st_ref)

    def lane_bcast(tile, lane, width):
        return jnp.broadcast_to(tile[:, lane:lane + 1], (tile.shape[0], width))

    def per_head(tile, first_lane):
        return jnp.concatenate([lane_bcast(tile, first_lane + h, HEAD) for h in range(DN_HEADS)], axis=1)

    def seq_body(i, carry):
        starts = (pl.multiple_of(i * CHUNK, CHUNK), pl.multiple_of((n_chunks - 1 - i) * CHUNK, CHUNK))
        qs = [xs_ref[0, pl.ds(starts[d], CHUNK), :] for d in range(2)]
        ks = [xs_ref[1, pl.ds(starts[d], CHUNK), :] for d in range(2)]
        norms = _head_sums([qs[0] * qs[0], ks[0] * ks[0], qs[1] * qs[1], ks[1] * ks[1]], bm)
        fac, gam_tiles = [], []
        for d in range(2):
            gt = gt_ref[pl.ds(starts[d], CHUNK), :]
            q = qs[d] * lax.rsqrt(norms[2 * d] + L2_EPS) * (HEAD ** -0.5)
            k = ks[d] * lax.rsqrt(norms[2 * d + 1] + L2_EPS)
            v = xs_ref[2, pl.ds(starts[d], CHUNK), :]
            g_tile = -jnp.exp(gvec[0:1]) * _softplus(gt + gvec[1:2])
            gam_tile = _cumsum_rows(g_tile, d)
            gam = per_head(gam_tile, 6 * d)
            beta = per_head(_sigmoid(gt), 12 + 6 * d)
            last = gam[CHUNK - 1:CHUNK] if d == 0 else gam[0:1]
            eg = jnp.exp(gam)
            kb = k * beta
            fac.append(dict(q=q, k=k, kb=kb, vb=v * beta, kbe=kb * eg, qe=q * eg, ko=k * jnp.exp(last - gam),
                            el=jnp.broadcast_to(jnp.exp(last), (CHUNK, PAIRS_W))))
            gam_tiles.append(gam_tile)
        gam_rows = [jnp.transpose(gam_tiles[d]) for d in range(2)]
        incl, strict, same_head, t_idx, s_idx = _quad_masks()
        os_ = []
        for j in range(3):
            ln = slice(j * LANES, (j + 1) * LANES)
            qd = lambda name: jnp.concatenate([fac[0][name][:, ln], fac[1][name][:, ln]], axis=1)
            gam_c, gam_r = [], []
            for d in range(2):
                for hh in range(2):
                    lane = 6 * d + 2 * j + hh
                    gam_c.append(lane_bcast(gam_tiles[d], lane, QUAD))
                    row = gam_rows[d][lane:lane + 1, :]
                    gam_r.append(jnp.broadcast_to(jnp.concatenate([row, row, row, row], axis=1), (CHUNK, QUAD)))
            diff = jnp.concatenate(gam_c, axis=0) - jnp.concatenate(gam_r, axis=0)
            pair = jnp.exp(jnp.where(incl, diff, -jnp.inf))
            k = qd("k")
            g = _mm_nt(jnp.concatenate([_qstack(qd("kb")), _qstack(qd("q"))], axis=0), _qdup(k))
            a2 = jnp.where(strict, g[:QUAD] * pair, 0.0)
            qk = g[QUAD:] * pair
            t2 = _inv_unit_tri(a2, same_head, t_idx, s_idx)
            uw = _mm(t2[:, :CHUNK], jnp.concatenate([qd("vb"), qd("kbe")], axis=1))
            u, w = _qdiag(uw[:, :QUAD]), _qdiag(uw[:, QUAD:])
            st = st_ref[j]
            wq = _mm_nt(jnp.concatenate([w, qd("qe")], axis=0), st)
            vn = u - wq[:CHUNK]
            os_.append(wq[CHUNK:] + _qapply(qk, vn))
            st_ref[j] = st * qd("el")[0:1] + jnp.where(same_head, _mm_tn(vn, qd("ko")), 0.0)
        for d in range(2):
            yd_ref[d, pl.ds(starts[d], CHUNK), :] = jnp.concatenate(
                [o[:, d * LANES:(d + 1) * LANES] for o in os_], axis=1)
        return carry

    lax.fori_loop(0, n_chunks, seq_body, 0)
    sfin_ref[0] = st_ref[...]

    def post_body(c, carry):
        start = pl.multiple_of(c * CHUNK, CHUNK)
        o = yd_ref[0, pl.ds(start, CHUNK), :] + yd_ref[1, pl.ds(start, CHUNK), :]
        z = z_ref[pl.ds(start, CHUNK), :]
        ms = _head_sums([o * o], bm)[0] * (1.0 / HEAD)
        y_ref[pl.ds(start, CHUNK), :] = o * lax.rsqrt(ms + RMS_EPS) * vec[0:1] * _silu(z)
        return carry

    lax.fori_loop(0, n_chunks, post_body, 0)


def _gdn_call(p, base_blk, n_seq, t_len, prm, s0):
    has_init = s0 is not None
    dn_blk = DN_OFF // PAIRS_W

    def pspec(col_blk):
        return pl.BlockSpec((t_len, PAIRS_W), lambda b: (base_blk + b, dn_blk + col_blk))

    full = lambda shape: pl.BlockSpec(shape, lambda b: (0,) * len(shape))
    in_specs = [pspec(0), pspec(1), pspec(2), pspec(3),
                pl.BlockSpec((t_len, LANES), lambda b: (base_blk + b, DN_GATE_OFF // LANES)),
                full((8, 3 * PAIRS_W)), full((8, PAIRS_W)), full((8, LANES))]
    args = [p, p, p, p, p, prm["cw"], prm["vec"], prm["gvec"]]
    if has_init:
        in_specs.append(pl.BlockSpec((1, 3, QUAD, QUAD), lambda b: (b, 0, 0, 0)))
        args.append(s0)
    y, sfin = pl.pallas_call(
        functools.partial(_gdn_kernel, t_len=t_len, has_init=has_init),
        grid=(n_seq,),
        in_specs=in_specs,
        out_specs=[pl.BlockSpec((t_len, PAIRS_W), lambda b: (b, 0)),
                   pl.BlockSpec((1, 3, QUAD, QUAD), lambda b: (b, 0, 0, 0))],
        out_shape=[jax.ShapeDtypeStruct((n_seq * t_len, DN_DIM), F32),
                   jax.ShapeDtypeStruct((n_seq, 3, QUAD, QUAD), F32)],
        scratch_shapes=[pltpu.VMEM((3, t_len, PAIRS_W), F32),
                        pltpu.VMEM((2, t_len, PAIRS_W), F32),
                        pltpu.VMEM((3, QUAD, QUAD), F32)],
        compiler_params=pltpu.CompilerParams(dimension_semantics=("arbitrary",), vmem_limit_bytes=VMEM_LIMIT),
        name="gdn_mix",
    )(*args)
    return y, sfin


def _gla_kernel(*refs, t_len, has_init):
    if has_init:
        (q_ref, k_ref, v_ref, og_ref, gt_ref, gk2_ref, vec_ref, s0_ref,
         y_ref, sfin_ref, yd_ref, st_ref) = refs
    else:
        (q_ref, k_ref, v_ref, og_ref, gt_ref, gk2_ref, vec_ref,
         y_ref, sfin_ref, yd_ref, st_ref) = refs
    n_chunks = t_len // CHUNK
    vec = vec_ref[...]
    sr = _iota((GLA_DIM, GLA_KDIM), 0)
    sc = _iota((GLA_DIM, GLA_KDIM), 1)
    st_mask = _shr(sr, GLA_DV) == _shr(sc, GLA_DK)

    if has_init:
        st_ref[...] = s0_ref[0]
    else:
        st_ref[...] = jnp.zeros_like(st_ref)

    def chunk_dir(d, c):
        start = pl.multiple_of(c * CHUNK, CHUNK)
        q = q_ref[pl.ds(start, CHUNK), :] * (GLA_DK ** -0.5)
        k = k_ref[pl.ds(start, CHUNK), :]
        v = v_ref[pl.ds(start, CHUNK), :]
        gt = gt_ref[pl.ds(start, CHUNK), :]
        gk = _log_sigmoid(_mm(gt, gk2_ref[d]) + vec[d:d + 1, :GLA_KDIM]) * (1.0 / GLA_GATE_NORM)
        b = _mmh(jnp.where(_tri(CHUNK, d, False), 1.0, 0.0), gk)
        tot = jnp.sum(gk, axis=0, keepdims=True)
        half = 0.5 * tot
        lane = _iota((CHUNK, GLA_KDIM), 1)
        qt = q * jnp.exp(b - half)
        q_stack = jnp.concatenate([jnp.where(_shr(lane, GLA_DK) == h, qt, 0.0) for h in range(GLA_HEADS)], axis=0)
        rr = _iota((GLA_HEADS * CHUNK, CHUNK), 0) & (CHUNK - 1)
        cc = _iota((GLA_HEADS * CHUNK, CHUNK), 1)
        causal = (cc <= rr) if d == 0 else (cc >= rr)
        scores = jnp.where(causal, _mm_nt(q_stack, k * jnp.exp(half - b)), 0.0)
        full = _mm(scores, v)
        vl = _shr(_iota((CHUNK, GLA_DIM), 1), GLA_DV)
        o = jnp.zeros((CHUNK, GLA_DIM), F32)
        for h in range(GLA_HEADS):
            o = o + jnp.where(vl == h, full[h * CHUNK:(h + 1) * CHUNK], 0.0)
        st = st_ref[d]
        o = o + _mm_nt(q * jnp.exp(b), st)
        st_ref[d] = st * jnp.exp(tot) + jnp.where(st_mask, _mm_tn(v, k * jnp.exp(tot - b)), 0.0)
        yd_ref[d, pl.ds(start, CHUNK), :] = o

    def seq_body(i, carry):
        chunk_dir(0, i)
        chunk_dir(1, n_chunks - 1 - i)
        return carry

    lax.fori_loop(0, n_chunks, seq_body, 0)
    sfin_ref[0] = st_ref[...]

    bm = _block_ones(GLA_DIM, GLA_DV)

    def post_body(c, carry):
        start = pl.multiple_of(c * CHUNK, CHUNK)
        o = yd_ref[0, pl.ds(start, CHUNK), :] + yd_ref[1, pl.ds(start, CHUNK), :]
        og = og_ref[pl.ds(start, CHUNK), :]
        ms = _mmh(o * o, bm) * (1.0 / GLA_DV)
        y_ref[pl.ds(start, CHUNK), :] = o * lax.rsqrt(ms + RMS_EPS) * vec[2:3] * _silu(og)
        return carry

    lax.fori_loop(0, n_chunks, post_body, 0)


def _gla_call(p, base_blk, n_seq, t_len, prm, s0):
    has_init = s0 is not None
    t128 = GLA_OFF // LANES
    t256 = GLA_OFF // GLA_DIM
    in_specs = [pl.BlockSpec((t_len, LANES), lambda b: (base_blk + b, t128)),
                pl.BlockSpec((t_len, LANES), lambda b: (base_blk + b, t128 + 1)),
                pl.BlockSpec((t_len, GLA_DIM), lambda b: (base_blk + b, t256 + 1)),
                pl.BlockSpec((t_len, GLA_DIM), lambda b: (base_blk + b, t256 + 2)),
                pl.BlockSpec((t_len, LANES), lambda b: (base_blk + b, t128 + 6)),
                pl.BlockSpec((2, LANES, LANES), lambda b: (0, 0, 0)),
                pl.BlockSpec((8, GLA_DIM), lambda b: (0, 0))]
    args = [p, p, p, p, p, prm["gk2"], prm["vec"]]
    if has_init:
        in_specs.append(pl.BlockSpec((1, 2, GLA_DIM, GLA_KDIM), lambda b: (b, 0, 0, 0)))
        args.append(s0)
    y, sfin = pl.pallas_call(
        functools.partial(_gla_kernel, t_len=t_len, has_init=has_init),
        grid=(n_seq,),
        in_specs=in_specs,
        out_specs=[pl.BlockSpec((t_len, GLA_DIM), lambda b: (b, 0)),
                   pl.BlockSpec((1, 2, GLA_DIM, GLA_KDIM), lambda b: (b, 0, 0, 0))],
        out_shape=[jax.ShapeDtypeStruct((n_seq * t_len, GLA_DIM), F32),
                   jax.ShapeDtypeStruct((n_seq, 2, GLA_DIM, GLA_KDIM), F32)],
        scratch_shapes=[pltpu.VMEM((2, t_len, GLA_DIM), F32),
                        pltpu.VMEM((2, GLA_DIM, GLA_KDIM), F32)],
        compiler_params=pltpu.CompilerParams(dimension_semantics=("arbitrary",), vmem_limit_bytes=VMEM_LIMIT),
        name="gla_mix",
    )(*args)
    return y, sfin


def _out_kernel(yr_ref, yg_ref, yd_ref, x_ref, g1_ref, sh_ref, sc_ref, nw_ref, wo_ref, wr_ref, br_ref,
                x1_ref, h2_ref, gate_ref):
    mix = (_mm(yr_ref[...], wo_ref[0:RW_DIM, :]) + _mm(yg_ref[...], wo_ref[RW_DIM:RW_DIM + GLA_DIM, :])
           + _mm(yd_ref[...], wo_ref[RW_DIM + GLA_DIM:, :]))
    x1 = x_ref[...] + g1_ref[0] * mix
    x1_ref[...] = x1
    h2 = x1 * lax.rsqrt(jnp.mean(x1 * x1, axis=-1, keepdims=True) + RMS_EPS) * nw_ref[...]
    h2 = h2 * (1.0 + sc_ref[0]) + sh_ref[0]
    h2_ref[...] = h2
    logits = _mmh(h2, wr_ref[...]) + br_ref[...]
    lane_i = _iota(logits.shape, 1)
    lane = lane_i.astype(F32)
    neg = -jnp.inf
    big = 1e9
    lg = jnp.where(lane_i < N_GROUPS, logits, neg)
    gmax = jnp.max(lg, axis=1, keepdims=True)
    gsel = jnp.min(jnp.where(lg == gmax, lane, big), axis=1, keepdims=True)
    p_grp = 1.0 / jnp.sum(jnp.exp(lg - gmax), axis=1, keepdims=True)
    e_lane = lane_i - N_GROUPS
    e_grp = _shr(jnp.maximum(e_lane, 0), EXPERTS_PER_GROUP).astype(F32)
    in_grp = (e_lane >= 0) & (e_lane < N_EXPERTS) & (e_grp == gsel)
    le = jnp.where(in_grp, logits, neg)
    v1 = jnp.max(le, axis=1, keepdims=True)
    i1 = jnp.min(jnp.where(le == v1, lane, big), axis=1, keepdims=True)
    le2 = jnp.where(lane == i1, neg, le)
    v2 = jnp.max(le2, axis=1, keepdims=True)
    i2 = jnp.min(jnp.where(le2 == v2, lane, big), axis=1, keepdims=True)
    e2 = jnp.exp(v2 - v1)
    p1 = p_grp / (1.0 + e2)
    p2 = p_grp * e2 / (1.0 + e2)
    gate_ref[...] = jnp.where(lane == i1, p1, jnp.where(lane == i2, p2, 0.0))


def _out_call(y_rw, y_gla, y_dn, x, mod3, layer, nw, w_out, wr, br, n_ctx_tok, lat_t, tile=256):
    n = x.shape[0]
    row = lambda i: layer * 8 + _cond_row(i, tile, n_ctx_tok, lat_t)
    def modspec(part):
        return pl.BlockSpec((1, 1, D_MODEL), lambda i: (row(i), 0, part))

    return pl.pallas_call(
        _out_kernel,
        grid=(n // tile,),
        in_specs=[pl.BlockSpec((tile, RW_DIM), lambda i: (i, 0)),
                  pl.BlockSpec((tile, GLA_DIM), lambda i: (i, 0)),
                  pl.BlockSpec((tile, DN_DIM), lambda i: (i, 0)),
                  pl.BlockSpec((tile, D_MODEL), lambda i: (i, 0)),
                  modspec(2), modspec(3), modspec(4),
                  pl.BlockSpec((1, D_MODEL), lambda i: (0, 0)),
                  pl.BlockSpec((D_MODEL, D_MODEL), lambda i: (0, 0)),
                  pl.BlockSpec((D_MODEL, LANES), lambda i: (0, 0)),
                  pl.BlockSpec((1, LANES), lambda i: (0, 0))],
        out_specs=[pl.BlockSpec((tile, D_MODEL), lambda i: (i, 0)),
                   pl.BlockSpec((tile, D_MODEL), lambda i: (i, 0)),
                   pl.BlockSpec((tile, LANES), lambda i: (i, 0))],
        out_shape=[jax.ShapeDtypeStruct((n, D_MODEL), F32),
                   jax.ShapeDtypeStruct((n, D_MODEL), F32),
                   jax.ShapeDtypeStruct((n, LANES), F32)],
        compiler_params=pltpu.CompilerParams(dimension_semantics=("arbitrary",), vmem_limit_bytes=VMEM_LIMIT),
        name="out_proj_router",
    )(y_rw, y_gla, y_dn, x, mod3, mod3, mod3, nw.reshape(1, D_MODEL), w_out, wr, br)


def _moe_kernel(h_ref, gate_ref, x_ref, g2_ref, wg_ref, wu_ref, wd_ref, o_ref, acc_ref):
    e = pl.program_id(1)

    @pl.when(e == 0)
    def _():
        acc_ref[...] = jnp.zeros_like(acc_ref)

    h = h_ref[...].astype(BF16)
    gates = gate_ref[...]
    lane = _iota(gates.shape, 1)
    gcol = jnp.sum(jnp.where(lane == N_GROUPS + e, gates, 0.0), axis=1, keepdims=True)
    hid = _silu(jnp.dot(h, wg_ref[0].astype(BF16), preferred_element_type=F32))
    hid = hid * jnp.dot(h, wu_ref[0].astype(BF16), preferred_element_type=F32) * gcol
    acc_ref[...] += _mm(hid, wd_ref[0])

    @pl.when(e == N_EXPERTS - 1)
    def _():
        o_ref[...] = x_ref[...] + g2_ref[0] * acc_ref[...]


def _moe_call(h2, gates, x1, mod3, layer, w_gate, w_up, w_down, n_ctx_tok, lat_t, tile=1024):
    n = h2.shape[0]
    row = lambda i: layer * 8 + _cond_row(i, tile, n_ctx_tok, lat_t)
    return pl.pallas_call(
        _moe_kernel,
        grid=(n // tile, N_EXPERTS),
        in_specs=[pl.BlockSpec((tile, D_MODEL), lambda i, e: (i, 0)),
                  pl.BlockSpec((tile, LANES), lambda i, e: (i, 0)),
                  pl.BlockSpec((tile, D_MODEL), lambda i, e: (i, 0)),
                  pl.BlockSpec((1, 1, D_MODEL), lambda i, e: (row(i), 0, 5)),
                  pl.BlockSpec((1, D_MODEL, D_EXPERT), lambda i, e: (layer * N_EXPERTS + e, 0, 0)),
                  pl.BlockSpec((1, D_MODEL, D_EXPERT), lambda i, e: (layer * N_EXPERTS + e, 0, 0)),
                  pl.BlockSpec((1, D_EXPERT, D_MODEL), lambda i, e: (layer * N_EXPERTS + e, 0, 0))],
        out_specs=pl.BlockSpec((tile, D_MODEL), lambda i, e: (i, 0)),
        out_shape=jax.ShapeDtypeStruct((n, D_MODEL), F32),
        scratch_shapes=[pltpu.VMEM((tile, D_MODEL), F32)],
        compiler_params=pltpu.CompilerParams(dimension_semantics=("arbitrary", "arbitrary"),
                                             vmem_limit_bytes=VMEM_LIMIT),
        name="moe_dense",
    )(h2, gates, x1, mod3, w_gate, w_up, w_down)


def _norm_kernel(x_ref, w_ref, o_ref):
    x = x_ref[...]
    o_ref[...] = x * lax.rsqrt(jnp.mean(x * x, axis=-1, keepdims=True) + RMS_EPS) * w_ref[...]


def _norm_call(x, w, base_tile, n_tiles, tile=256):
    return pl.pallas_call(
        _norm_kernel,
        grid=(n_tiles,),
        in_specs=[pl.BlockSpec((tile, D_MODEL), lambda i: (base_tile + i, 0)),
                  pl.BlockSpec((1, D_MODEL), lambda i: (0, 0))],
        out_specs=pl.BlockSpec((tile, D_MODEL), lambda i: (i, 0)),
        out_shape=jax.ShapeDtypeStruct((n_tiles * tile, D_MODEL), F32),
        compiler_params=pltpu.CompilerParams(dimension_semantics=("arbitrary",), vmem_limit_bytes=VMEM_LIMIT),
        name="final_norm",
    )(x, w.reshape(1, D_MODEL))


def _pos_embed(n_tokens, dim):
    rows = n_tokens // GRID_W
    row = jnp.broadcast_to(jnp.arange(rows)[:, None], (rows, GRID_W)).reshape(-1)
    col = jnp.broadcast_to(jnp.arange(GRID_W)[None, :], (rows, GRID_W)).reshape(-1)
    quarter = dim // 4
    omega = 1.0 / (POS_BASE ** (jnp.arange(quarter, dtype=F32) / quarter))

    def axis_embed(pos):
        ang = pos.astype(F32)[:, None] * omega[None, :]
        return jnp.concatenate([jnp.sin(ang), jnp.cos(ang)], axis=-1)

    return jnp.concatenate([axis_embed(row), axis_embed(col)], axis=-1)


def _relayout_w_in(w_in_l):
    rw = w_in_l[:, :RW_COLS]
    g0 = RW_COLS
    q, k, v = w_in_l[:, g0:g0 + 128], w_in_l[:, g0 + 128:g0 + 256], w_in_l[:, g0 + 256:g0 + 512]
    gates = w_in_l[:, g0 + 512:g0 + 544]
    og = w_in_l[:, g0 + 544:g0 + 800]
    d0 = g0 + 800
    qkvz = w_in_l[:, d0:d0 + 1536]
    dgates = w_in_l[:, d0 + 1536:d0 + 1560]
    z = lambda n: jnp.zeros((D_MODEL, n), w_in_l.dtype)
    return jnp.concatenate([rw, qkvz, q, k, v, og, gates, z(96), dgates, z(104)], axis=1).astype(BF16)


def _rwkv_params(mu, w0, w2, a0, a2, g2, kk, ka, rk, ln_w, ln_b):
    rows = [w0[0], w0[1], a0[0], a0[1], kk, ka, rk, ln_w, ln_b]
    vec = jnp.stack(rows + [jnp.zeros_like(kk)] * 7)
    zero = jnp.zeros((64, RW_DIM), F32)
    w2c = jnp.stack([jnp.concatenate([w2[0], zero]), jnp.concatenate([zero, w2[1]])])
    a2c = jnp.stack([jnp.concatenate([a2[0], zero]), jnp.concatenate([zero, a2[1]])])
    return {"mu": mu, "vec": vec, "w2": w2c, "a2": a2c, "g2": g2}


def _gdn_params(conv, a_log, dt_bias, norm_w):
    cw = jnp.concatenate([conv, jnp.zeros((5, 3 * DN_DIM), F32)], axis=0)
    vec = jnp.zeros((8, DN_DIM), F32).at[0].set(jnp.tile(norm_w, DN_HEADS))
    gvec = jnp.zeros((8, LANES), F32)
    gvec = gvec.at[0, 0:12].set(a_log.reshape(12)).at[1, 0:12].set(dt_bias.reshape(12))
    return {"cw": cw, "vec": vec, "gvec": gvec}


def _gla_params(gk2, gk_b, norm_w):
    g = jnp.zeros((2, LANES, LANES), F32)
    g = g.at[0, 0:16].set(gk2[0]).at[1, 16:32].set(gk2[1])
    vec = jnp.zeros((8, GLA_DIM), F32)
    vec = vec.at[0, :GLA_KDIM].set(gk_b[0]).at[1, :GLA_KDIM].set(gk_b[1]).at[2].set(jnp.tile(norm_w, GLA_HEADS))
    return {"gk2": g, "vec": vec}


def _pair_state_in(s):
    b = s.shape[0]
    st = jnp.swapaxes(s, -1, -2).reshape(b, 2, 3, 2, HEAD, HEAD)
    st = jnp.transpose(st, (0, 2, 1, 3, 4, 5)).reshape(b, 3, 4, HEAD, HEAD)
    bd = jnp.einsum("bjgvk,gh->bjgvhk", st, jnp.eye(4, dtype=s.dtype))
    return bd.reshape(b, 3, QUAD, QUAD)


def _pair_state_out(st):
    b = st.shape[0]
    x = st.reshape(b, 3, 4, HEAD, 4, HEAD)
    diag = jnp.stack([x[:, :, g, :, g, :] for g in range(4)], axis=2)
    diag = diag.reshape(b, 3, 2, 2, HEAD, HEAD)
    diag = jnp.transpose(diag, (0, 2, 1, 3, 5, 4))
    return diag.reshape(b, 2, 6, HEAD, HEAD)


def _gla_state_in(s):
    b = s.shape[0]
    st = jnp.swapaxes(s, -1, -2)
    eye = jnp.eye(GLA_HEADS, dtype=s.dtype)
    bd = jnp.einsum("bdhvk,hg->bdhvgk", st, eye)
    return bd.reshape(b, 2, GLA_DIM, GLA_KDIM)


def _gla_state_out(st):
    b = st.shape[0]
    x = st.reshape(b, 2, GLA_HEADS, GLA_DV, GLA_HEADS, GLA_DK)
    diag = jnp.stack([x[:, :, h, :, h, :] for h in range(GLA_HEADS)], axis=2)
    return jnp.swapaxes(diag, -1, -2)


def kernel(x_prompt, x_sample, c, state_rwkv, state_gla, state_delta, c_ctx, ada_w, ada_b, norm1_w, norm2_w, final_norm_w, w_in, w_out, rwkv_mu, rwkv_w0, rwkv_w2, rwkv_a0, rwkv_a2, rwkv_g2, rwkv_kk, rwkv_ka, rwkv_rk, rwkv_ln_w, rwkv_ln_b, gla_gk2, gla_gk_b, gla_norm_w, dn_conv, dn_a_log, dn_dt_bias, dn_norm_w, moe_wg, moe_bg, moe_we, moe_be, moe_w_gate, moe_w_up, moe_w_down):
    n_ctx, t_ctx, d = x_prompt.shape
    n_lat, t_lat, _ = x_sample.shape
    depth = ada_w.shape[0]
    n_ctx_tok = n_ctx * t_ctx
    n_lat_tok = n_lat * t_lat
    assert d == D_MODEL and n_ctx_tok % 1024 == 0 and t_lat % 1024 == 0 and t_ctx % 256 == 0 and n_lat <= 7
    assert n_ctx_tok % t_lat == 0

    xs = x_sample + _pos_embed(t_lat, d)[None].astype(x_sample.dtype)
    x = jnp.concatenate([x_prompt.reshape(n_ctx_tok, d), xs.reshape(n_lat_tok, d)], axis=0)

    cond8 = jnp.zeros((8, d), F32).at[0].set(c_ctx).at[1:1 + n_lat].set(c)
    mod = _ada_call(cond8, ada_w, ada_b)
    mod3 = mod.reshape(depth * 8, 1, 6 * d)

    w_gate = moe_w_gate.reshape(depth * N_EXPERTS, d, D_EXPERT)
    w_up = moe_w_up.reshape(depth * N_EXPERTS, d, D_EXPERT)
    w_down = moe_w_down.reshape(depth * N_EXPERTS, D_EXPERT, d)

    lat_base = n_ctx_tok // t_lat
    rw_states, gla_states, dn_states = [], [], []
    for l in range(depth):
        p = _in_call(x, mod3, l, norm1_w[l], _relayout_w_in(w_in[l]), n_ctx_tok, t_lat)
        rw_p = _rwkv_params(rwkv_mu[l], rwkv_w0[l], rwkv_w2[l], rwkv_a0[l], rwkv_a2[l], rwkv_g2[l], rwkv_kk[l],
                            rwkv_ka[l], rwkv_rk[l], rwkv_ln_w[l], rwkv_ln_b[l])
        dn_p = _gdn_params(dn_conv[l], dn_a_log[l], dn_dt_bias[l], dn_norm_w[l])
        gla_p = _gla_params(gla_gk2[l], gla_gk_b[l], gla_norm_w[l])

        y_rw_c, s_rw = _rwkv_call(p, 0, n_ctx, t_ctx, rw_p, None)
        y_rw_l, _ = _rwkv_call(p, lat_base, n_lat, t_lat, rw_p, _pair_state_in(state_rwkv[:, l]))
        y_dn_c, s_dn = _gdn_call(p, 0, n_ctx, t_ctx, dn_p, None)
        y_dn_l, _ = _gdn_call(p, lat_base, n_lat, t_lat, dn_p, _pair_state_in(state_delta[:, l]))
        y_gla_c, s_gla = _gla_call(p, 0, n_ctx, t_ctx, gla_p, None)
        y_gla_l, _ = _gla_call(p, lat_base, n_lat, t_lat, gla_p, _gla_state_in(state_gla[:, l]))
        rw_states.append(_pair_state_out(s_rw))
        dn_states.append(_pair_state_out(s_dn))
        gla_states.append(_gla_state_out(s_gla))

        wr = jnp.zeros((d, LANES), F32)
        wr = wr.at[:, :N_GROUPS].set(moe_wg[l]).at[:, N_GROUPS:N_GROUPS + N_EXPERTS].set(moe_we[l].reshape(d, N_EXPERTS))
        br = jnp.zeros((1, LANES), F32)
        br = br.at[0, :N_GROUPS].set(moe_bg[l]).at[0, N_GROUPS:N_GROUPS + N_EXPERTS].set(moe_be[l].reshape(N_EXPERTS))
        x1, h2, gates = _out_call(jnp.concatenate([y_rw_c, y_rw_l]), jnp.concatenate([y_gla_c, y_gla_l]),
                                  jnp.concatenate([y_dn_c, y_dn_l]), x, mod3, l, norm2_w[l],
                                  w_out[l].astype(BF16), wr, br, n_ctx_tok, t_lat)
        x = _moe_call(h2, gates, x1, mod3, l, w_gate, w_up, w_down, n_ctx_tok, t_lat)

    y_prompt = _norm_call(x, final_norm_w, 0, n_ctx_tok // 256).reshape(n_ctx, t_ctx, d)
    y_sample = _norm_call(x, final_norm_w, n_ctx_tok // 256, n_lat_tok // 256).reshape(n_lat, t_lat, d)
    new_rw = jnp.stack(rw_states, axis=1).astype(x_prompt.dtype)
    new_gla = jnp.stack(gla_states, axis=1).astype(x_prompt.dtype)
    new_dn = jnp.stack(dn_states, axis=1).astype(x_prompt.dtype)
    return (y_prompt, y_sample, new_rw, new_gla, new_dn)
```

```python
import functools
import math

import numpy as np
import jax
import jax.numpy as jnp
from jax import lax
from jax.experimental import pallas as pl
from jax.experimental.pallas import tpu as pltpu

F32 = jnp.float32
BF16 = jnp.bfloat16
HI = lax.Precision.HIGHEST

D_MODEL = 1024
DEPTH = 4
GRID_W = 64
POS_BASE = 10000.0
RMS_EPS = 1e-6
L2_EPS = 1e-6

RW_HEADS = 6
RW_DIM = 384
RW_COLS = 1536
RW_LN_EPS = 64e-5
GLA_HEADS = 4
GLA_DK = 32
GLA_DV = 64
GLA_KDIM = 128
GLA_DIM = 256
GLA_GATE_RANK = 16
GLA_GATE_NORM = 16.0
DN_HEADS = 6
DN_DIM = 384
N_GROUPS = 4
EXPERTS_PER_GROUP = 4
N_EXPERTS = 16
D_EXPERT = 512

LANES = 128
CHUNK = 64
HEAD = 64
PAIRS_W = 384
DN_OFF = RW_COLS
GLA_OFF = DN_OFF + 4 * PAIRS_W
GLA_PCOLS = 896
DN_GATE_OFF = GLA_OFF + GLA_PCOLS
P_COLS = DN_GATE_OFF + LANES
VMEM_LIMIT = 56 * 1024 * 1024
SEQS_PER_STEP = 2


def _seq_buffering(rows):
    return pl.Buffered(1) if rows * PAIRS_W * 4 > (2 << 20) else None


def _mm(a, b):
    return jnp.dot(a.astype(BF16), b.astype(BF16), preferred_element_type=F32)


def _mm_nt(a, b):
    return lax.dot_general(a.astype(BF16), b.astype(BF16), (((1,), (1,)), ((), ())),
                           preferred_element_type=F32)


def _mm_tn(a, b):
    return lax.dot_general(a.astype(BF16), b.astype(BF16), (((0,), (0,)), ((), ())),
                           preferred_element_type=F32)


def _mmh(a, b):
    return jnp.dot(a, b, precision=HI, preferred_element_type=F32)


def _mmh_nt(a, b):
    return lax.dot_general(a, b, (((1,), (1,)), ((), ())), precision=HI, preferred_element_type=F32)


def _iota(shape, dim):
    return lax.broadcasted_iota(jnp.int32, shape, dim)


def _sigmoid(x):
    return 1.0 / (1.0 + jnp.exp(-x))


def _silu(x):
    return x * _sigmoid(x)


def _softplus(x):
    return jnp.maximum(x, 0.0) + jnp.log(1.0 + jnp.exp(-jnp.abs(x)))


def _log_sigmoid(x):
    return -_softplus(-x)


def _tri(n, d, strict):
    r = _iota((n, n), 0)
    c = _iota((n, n), 1)
    if d == 0:
        return (c < r) if strict else (c <= r)
    return (c > r) if strict else (c >= r)


QUAD = 4 * HEAD


def _quad_masks():
    t = _iota((CHUNK, QUAD), 0)
    lane = _iota((CHUNK, QUAD), 1)
    s = lane & (CHUNK - 1)
    fwd = lane < 2 * HEAD
    bwd = jnp.logical_not(fwd)
    strict = (fwd & (s < t)) | (bwd & (s > t))
    incl = strict | (s == t)
    return incl, strict, t, s


def _same_head():
    return (_iota((QUAD, QUAD), 0) >> 6) == (_iota((QUAD, QUAD), 1) >> 6)


def _qstack(a):
    grp = _iota(a.shape, 1) >> 6
    return jnp.concatenate([jnp.where(grp == v, a, 0.0) for v in range(4)], axis=0)


def _inv_unit_tri_many(ls, t, s):
    same_blk = lambda width: _shr(t, width) == _shr(s, width)
    mul = lambda a, b: _mm(a, _qstack(b))
    n0 = [jnp.where(same_blk(8), -l, 0.0) for l in ls]
    p1 = [mul(x, x) for x in n0]
    inv = [jnp.where(s == t, 1.0, 0.0) + x for x in n0]
    p2 = [mul(x, x) for x in p1]
    inv = [x + mul(x, p) for x, p in zip(inv, p1)]
    inv = [x + mul(x, p) for x, p in zip(inv, p2)]
    for width in (8, 16, 32):
        blk = same_blk(2 * width) & jnp.logical_not(same_blk(width))
        half = [mul(x, jnp.where(blk, l, 0.0)) for x, l in zip(inv, ls)]
        inv = [x - mul(h, x) for x, h in zip(inv, half)]
    return inv


def _split2(x):
    hi = x.astype(BF16)
    return hi, (x - hi.astype(F32)).astype(BF16)


def _head_sums(xs, bm):
    n_tiles = xs[0].shape[1] // LANES
    tiles = [x[:, j * LANES:(j + 1) * LANES] for x in xs for j in range(n_tiles)]
    parts = [p for tl in tiles for p in _split2(tl)]
    out = jnp.dot(jnp.concatenate(parts, axis=0), bm, preferred_element_type=F32)
    res = []
    for i in range(len(xs)):
        cols = []
        for j in range(n_tiles):
            base = (i * n_tiles + j) * 2 * CHUNK
            cols.append(out[base:base + CHUNK] + out[base + CHUNK:base + 2 * CHUNK])
        res.append(jnp.concatenate(cols, axis=1))
    return res


def _cumsum_rows(x, d):
    row = _iota(x.shape, 0)
    k = 1
    while k < CHUNK:
        if d == 0:
            x = x + jnp.where(row >= k, pltpu.roll(x, k, 0), 0.0)
        else:
            x = x + jnp.where(row < CHUNK - k, pltpu.roll(x, CHUNK - k, 0), 0.0)
        k *= 2
    return x


def _shr(x, width):
    return x >> (width.bit_length() - 1)


def _block_ones(n, width):
    r = _iota((n, n), 0)
    c = _iota((n, n), 1)
    return jnp.where(_shr(r, width) == _shr(c, width), 1.0, 0.0).astype(F32)


def _shifted(ref, off, start, c, n_chunks, t_len):
    pc = ref[pl.ds(off + start, CHUNK), :]
    pb = ref[pl.ds(pl.multiple_of(off + jnp.maximum(start - 8, 0), 8), 8), :]
    nb = ref[pl.ds(pl.multiple_of(off + jnp.minimum(start + CHUNK, t_len - 8), 8), 8), :]
    carry_p = jnp.where(c > 0, pb[7:8, :], 0.0)
    carry_n = jnp.where(c < n_chunks - 1, nb[0:1, :], 0.0)
    row = _iota(pc.shape, 0)
    prev = jnp.where(row == 0, carry_p, pltpu.roll(pc, 1, 0))
    nxt = jnp.where(row == CHUNK - 1, carry_n, pltpu.roll(pc, CHUNK - 1, 0))
    return pc, prev, nxt


def _ada_kernel(c_ref, w_ref, b_ref, o_ref):
    c = c_ref[...]
    o_ref[0] = _mm(_silu(c), w_ref[0]) + b_ref[0]


def _ada_call(cond8, ada_w, ada_b):
    nl = ada_w.shape[0]
    return pl.pallas_call(
        _ada_kernel,
        grid=(nl, 6),
        in_specs=[pl.BlockSpec((8, D_MODEL), lambda l, j: (0, 0)),
                  pl.BlockSpec((1, D_MODEL, D_MODEL), lambda l, j: (l, 0, j)),
                  pl.BlockSpec((1, 1, D_MODEL), lambda l, j: (l, 0, j))],
        out_specs=pl.BlockSpec((1, 8, D_MODEL), lambda l, j: (l, 0, j)),
        out_shape=jax.ShapeDtypeStruct((nl, 8, 6 * D_MODEL), F32),
        compiler_params=pltpu.CompilerParams(dimension_semantics=("arbitrary", "arbitrary"),
                                             vmem_limit_bytes=VMEM_LIMIT),
        name="ada_mod",
    )(cond8, ada_w, ada_b.reshape(nl, 1, 6 * D_MODEL))


def _cond_row(i, tile, n_ctx_tok, lat_t):
    tok = i * tile
    return jnp.where(tok < n_ctx_tok, 0, 1 + (tok - n_ctx_tok) // lat_t)


def _in_kernel(x_ref, sh_ref, sc_ref, nw_ref, w_ref, o_ref):
    x = x_ref[...]
    y = x * lax.rsqrt(jnp.mean(x * x, axis=-1, keepdims=True) + RMS_EPS) * nw_ref[...]
    h = y * (1.0 + sc_ref[0]) + sh_ref[0]
    o_ref[...] = jnp.dot(h.astype(BF16), w_ref[...], preferred_element_type=F32)


def _in_call(x, mod3, layer, nw, w_in_p, n_ctx_tok, lat_t, tile=256):
    n = x.shape[0]
    row = lambda i: layer * 8 + _cond_row(i, tile, n_ctx_tok, lat_t)
    return pl.pallas_call(
        _in_kernel,
        grid=(n // tile,),
        in_specs=[pl.BlockSpec((tile, D_MODEL), lambda i: (i, 0)),
                  pl.BlockSpec((1, 1, D_MODEL), lambda i: (row(i), 0, 0)),
                  pl.BlockSpec((1, 1, D_MODEL), lambda i: (row(i), 0, 1)),
                  pl.BlockSpec((1, D_MODEL), lambda i: (0, 0)),
                  pl.BlockSpec((D_MODEL, P_COLS), lambda i: (0, 0))],
        out_specs=pl.BlockSpec((tile, P_COLS), lambda i: (i, 0)),
        out_shape=jax.ShapeDtypeStruct((n, P_COLS), F32),
        compiler_params=pltpu.CompilerParams(dimension_semantics=("arbitrary",), vmem_limit_bytes=VMEM_LIMIT),
        name="in_proj",
    )(x, mod3, mod3, nw.reshape(1, D_MODEL), w_in_p)


EXP_M05 = math.exp(-0.5)


def _rwkv_kernel(*refs, t_len, n_sub, has_init):
    if has_init:
        (r_ref, k_ref, v_ref, x3_ref, mu_ref, vec_ref, w2_ref, a2_ref, g2_ref, s0_ref,
         y_ref, sfin_ref, xs_ref, yd_ref, st_ref) = refs
    else:
        (r_ref, k_ref, v_ref, x3_ref, mu_ref, vec_ref, w2_ref, a2_ref, g2_ref,
         y_ref, sfin_ref, xs_ref, yd_ref, st_ref) = refs
    n_chunks = t_len // CHUNK
    vec = vec_ref[...]
    bm = _block_ones(LANES, HEAD).astype(BF16)

    def shift_body(c, carry):
        start = pl.multiple_of(c * CHUNK, CHUNK)
        for sub in range(n_sub):
            for idx, ref in enumerate((r_ref, k_ref, v_ref, x3_ref)):
                pc, prev, nxt = _shifted(ref, sub * t_len, start, c, n_chunks, t_len)
                mu0 = mu_ref[0:1, idx * PAIRS_W:(idx + 1) * PAIRS_W]
                mu1 = mu_ref[1:2, idx * PAIRS_W:(idx + 1) * PAIRS_W]
                xs_ref[idx, pl.ds(sub * t_len + start, CHUNK), :] = pc + mu0 * (prev - pc) + mu1 * (nxt - pc)
        return carry

    lax.fori_loop(0, n_chunks, shift_body, 0)

    if has_init:
        st_ref[...] = s0_ref[...]
    else:
        st_ref[...] = jnp.zeros_like(st_ref)

    def prep(d, start):
        k = xs_ref[1, pl.ds(start, CHUNK), :]
        x3 = xs_ref[3, pl.ds(start, CHUNK), :]
        xw, xa = x3[:, 0:LANES], x3[:, LANES:2 * LANES]
        w_raw = vec[d:d + 1] + _mm(jnp.tanh(xw), w2_ref[d])
        lw = -EXP_M05 * _sigmoid(w_raw)
        a = _sigmoid(vec[2 + d:3 + d] + _mm(xa, a2_ref[d]))
        kmod = k * (1.0 + (a - 1.0) * vec[5:6])
        cum = _cumsum_rows(lw, d)
        half = 0.5 * jnp.sum(lw, axis=0, keepdims=True)
        return k * vec[4:5], a, kmod, cum, lw, half

    def seq_step(sub, i):
        off = sub * t_len
        starts = (pl.multiple_of(off + i * CHUNK, CHUNK), pl.multiple_of(off + (n_chunks - 1 - i) * CHUNK, CHUNK))
        pre = [prep(d, starts[d]) for d in range(2)]
        norms = _head_sums([pre[0][0] * pre[0][0], pre[1][0] * pre[1][0]], bm)
        fac = []
        for d in range(2):
            kk, a, kmod, cum, lw, half = pre[d]
            r = xs_ref[0, pl.ds(starts[d], CHUNK), :]
            kk = kk * lax.rsqrt(norms[d] + L2_EPS)
            em = jnp.exp(half)
            e_up = jnp.exp(half - cum)
            rh = r * jnp.exp(cum - half)
            ch = kk * jnp.exp(cum - lw - half)
            bh = kk * a * e_up
            kh = kmod * e_up
            v = xs_ref[2, pl.ds(starts[d], CHUNK), :]
            fac.append(dict(rh=rh, ch=ch, bh=bh, kh=kh, c0=ch * em, r0=rh * em, bt=bh * em, kt=kh * em, v=v,
                            dec=jnp.broadcast_to(em * em, (CHUNK, PAIRS_W))))
        return starts, fac

    def seq_body(i, carry):
        subs = [seq_step(sub, i) for sub in range(n_sub)]
        chains = [(sub, j) for sub in range(n_sub) for j in range(3)]

        def quad(c, name):
            fac, ln = subs[c[0]][1], slice(c[1] * LANES, (c[1] + 1) * LANES)
            return jnp.concatenate([fac[0][name][:, ln], fac[1][name][:, ln]], axis=1)

        incl, strict, t_idx, s_idx = _quad_masks()
        same_head = _same_head()
        v = [quad(c, "v") for c in chains]
        g = [_mm_nt(jnp.concatenate([quad(c, "ch"), quad(c, "rh")], axis=0),
                    jnp.concatenate([_qstack(quad(c, "bh")), _qstack(quad(c, "kh"))], axis=0))
             for c in chains]
        l_cb = [jnp.where(strict, x[:CHUNK, :QUAD], 0.0) for x in g]
        l_ck = [jnp.where(strict, x[:CHUNK, QUAD:], 0.0) for x in g]
        m_rb = [jnp.where(incl, x[CHUNK:, :QUAD], 0.0) for x in g]
        m_rk = [jnp.where(incl, x[CHUNK:, QUAD:], 0.0) for x in g]
        lv_y0 = [_mm(jnp.concatenate([a, b], axis=0), _qstack(x)) for a, b, x in zip(l_ck, m_rk, v)]
        t2 = _inv_unit_tri_many(l_cb, t_idx, s_idx)
        wu = [_mm(t, jnp.concatenate([_qstack(quad(c, "c0")), _qstack(x[:CHUNK])], axis=1))
              for t, c, x in zip(t2, chains, lv_y0)]
        st = [st_ref[c[0], c[1]] for c in chains]
        wr = [_mm_nt(jnp.concatenate([x[:, :QUAD], quad(c, "r0")], axis=0), s)
              for x, c, s in zip(wu, chains, st)]
        u = [-x[:, QUAD:] - y[:CHUNK] for x, y in zip(wu, wr)]
        mu_ = [_mm(m, _qstack(x)) for m, x in zip(m_rb, u)]
        upd = [_mm_tn(jnp.concatenate([x, vv], axis=0), jnp.concatenate([quad(c, "bt"), quad(c, "kt")], axis=0))
               for x, vv, c in zip(u, v, chains)]
        ys = [a[CHUNK:] + b[CHUNK:] + m for a, b, m in zip(lv_y0, wr, mu_)]
        for c, s, x in zip(chains, st, upd):
            st_ref[c[0], c[1]] = s * quad(c, "dec")[0:1] + jnp.where(same_head, x, 0.0)
        for sub in range(n_sub):
            for d in range(2):
                yd_ref[d, pl.ds(subs[sub][0][d], CHUNK), :] = jnp.concatenate(
                    [ys[sub * 3 + j][:, d * LANES:(d + 1) * LANES] for j in range(3)], axis=1)
        return carry

    lax.fori_loop(0, n_chunks, seq_body, 0)
    sfin_ref[...] = st_ref[...]

    def post_body(c, carry):
        start = pl.multiple_of(c * CHUNK, CHUNK)
        r = xs_ref[0, pl.ds(start, CHUNK), :]
        k = xs_ref[1, pl.ds(start, CHUNK), :]
        v = xs_ref[2, pl.ds(start, CHUNK), :]
        xg = xs_ref[3, pl.ds(start, CHUNK), 2 * LANES:3 * LANES]
        y = yd_ref[0, pl.ds(start, CHUNK), :] + yd_ref[1, pl.ds(start, CHUNK), :]
        mean, rk = _head_sums([y, r * k * vec[6:7]], bm)
        yc = y - mean * (1.0 / HEAD)
        var = _head_sums([yc * yc], bm)[0] * (1.0 / HEAD)
        yn = yc * lax.rsqrt(var + RW_LN_EPS) * vec[7:8] + vec[8:9]
        g = _mm(_sigmoid(xg), g2_ref[...])
        y_ref[pl.ds(start, CHUNK), :] = (yn + rk * v) * g
        return carry

    lax.fori_loop(0, n_sub * n_chunks, post_body, 0)


def _rwkv_call(p, base_blk, n_seq, t_len, prm, s0):
    has_init = s0 is not None
    n_sub = SEQS_PER_STEP
    assert n_seq % n_sub == 0 and base_blk % n_sub == 0
    rows = n_sub * t_len

    def pspec(col_blk):
        return pl.BlockSpec((rows, PAIRS_W), lambda b: (base_blk // n_sub + b, col_blk),
                            pipeline_mode=_seq_buffering(rows))

    full = lambda shape: pl.BlockSpec(shape, lambda b: (0,) * len(shape))
    in_specs = [pspec(0), pspec(1), pspec(2), pspec(3),
                full((2, RW_COLS)), full((16, PAIRS_W)), full((2, LANES, PAIRS_W)), full((2, LANES, PAIRS_W)),
                full((LANES, PAIRS_W))]
    args = [p, p, p, p, prm["mu"], prm["vec"], prm["w2"], prm["a2"], prm["g2"]]
    if has_init:
        in_specs.append(pl.BlockSpec((n_sub, 3, QUAD, QUAD), lambda b: (b, 0, 0, 0)))
        args.append(s0)
    y, sfin = pl.pallas_call(
        functools.partial(_rwkv_kernel, t_len=t_len, n_sub=n_sub, has_init=has_init),
        grid=(n_seq // n_sub,),
        in_specs=in_specs,
        out_specs=[pl.BlockSpec((rows, PAIRS_W), lambda b: (b, 0)),
                   pl.BlockSpec((n_sub, 3, QUAD, QUAD), lambda b: (b, 0, 0, 0))],
        out_shape=[jax.ShapeDtypeStruct((n_seq * t_len, RW_DIM), F32),
                   jax.ShapeDtypeStruct((n_seq, 3, QUAD, QUAD), F32)],
        scratch_shapes=[pltpu.VMEM((4, rows, PAIRS_W), F32),
                        pltpu.VMEM((2, rows, PAIRS_W), F32),
                        pltpu.VMEM((n_sub, 3, QUAD, QUAD), F32)],
        compiler_params=pltpu.CompilerParams(dimension_semantics=("arbitrary",), vmem_limit_bytes=VMEM_LIMIT),
        name="rwkv7_mix",
    )(*args)
    return y, sfin


def _gdn_kernel(*refs, t_len, n_sub, has_init):
    if has_init:
        (q_ref, k_ref, v_ref, z_ref, gt_ref, cw_ref, vec_ref, gvec_ref, s0_ref,
         y_ref, sfin_ref, xs_ref, yd_ref, st_ref) = refs
    else:
        (q_ref, k_ref, v_ref, z_ref, gt_ref, cw_ref, vec_ref, gvec_ref,
         y_ref, sfin_ref, xs_ref, yd_ref, st_ref) = refs
    n_chunks = t_len // CHUNK
    vec = vec_ref[...]
    gvec = gvec_ref[...]
    bm = _block_ones(LANES, HEAD).astype(BF16)

    def conv_body(c, carry):
        start = pl.multiple_of(c * CHUNK, CHUNK)
        for sub in range(n_sub):
            for idx, ref in enumerate((q_ref, k_ref, v_ref)):
                pc, prev, nxt = _shifted(ref, sub * t_len, start, c, n_chunks, t_len)
                cw = cw_ref[:, idx * PAIRS_W:(idx + 1) * PAIRS_W]
                xs_ref[idx, pl.ds(sub * t_len + start, CHUNK), :] = _silu(
                    cw[0:1] * prev + cw[1:2] * pc + cw[2:3] * nxt)
        return carry

    lax.fori_loop(0, n_chunks, conv_body, 0)

    if has_init:
        st_ref[...] = s0_ref[...]
    else:
        st_ref[...] = jnp.zeros_like(st_ref)

    def lane_bcast(tile, lane, width):
        return jnp.broadcast_to(tile[:, lane:lane + 1], (tile.shape[0], width))

    def per_head(tile, first_lane):
        return jnp.concatenate([lane_bcast(tile, first_lane + h, HEAD) for h in range(DN_HEADS)], axis=1)

    def seq_step(sub, i):
        off = sub * t_len
        starts = (pl.multiple_of(off + i * CHUNK, CHUNK), pl.multiple_of(off + (n_chunks - 1 - i) * CHUNK, CHUNK))
        qs = [xs_ref[0, pl.ds(starts[d], CHUNK), :] for d in range(2)]
        ks = [xs_ref[1, pl.ds(starts[d], CHUNK), :] for d in range(2)]
        norms = _head_sums([qs[0] * qs[0], ks[0] * ks[0], qs[1] * qs[1], ks[1] * ks[1]], bm)
        fac, gam_tiles = [], []
        for d in range(2):
            gt = gt_ref[pl.ds(starts[d], CHUNK), :]
            q = qs[d] * lax.rsqrt(norms[2 * d] + L2_EPS) * (HEAD ** -0.5)
            k = ks[d] * lax.rsqrt(norms[2 * d + 1] + L2_EPS)
            v = xs_ref[2, pl.ds(starts[d], CHUNK), :]
            g_tile = -jnp.exp(gvec[0:1]) * _softplus(gt + gvec[1:2])
            gam_tile = _cumsum_rows(g_tile, d)
            gam = per_head(gam_tile, 6 * d)
            beta = per_head(_sigmoid(gt), 12 + 6 * d)
            last = gam[CHUNK - 1:CHUNK] if d == 0 else gam[0:1]
            eg = jnp.exp(gam)
            kb = k * beta
            fac.append(dict(q=q, k=k, kb=kb, vb=v * beta, kbe=kb * eg, qe=q * eg, ko=k * jnp.exp(last - gam), gam=gam,
                            el=jnp.broadcast_to(jnp.exp(last), (CHUNK, PAIRS_W))))
            gam_tiles.append(gam_tile)
        gam_rows = [jnp.transpose(gam_tiles[d]) for d in range(2)]
        incl = _quad_masks()[0]
        pairs = []
        for j in range(3):
            ln = slice(j * LANES, (j + 1) * LANES)
            gam_c = jnp.concatenate([fac[0]["gam"][:, ln], fac[1]["gam"][:, ln]], axis=1)
            gam_r = jnp.concatenate([gam_rows[d][6 * d + 2 * j + hh:6 * d + 2 * j + hh + 1, :]
                                     for d in range(2) for hh in range(2)], axis=1)
            pairs.append(jnp.exp(jnp.where(incl, gam_c - gam_r, -jnp.inf)))
        return starts, fac, pairs

    def seq_body(i, carry):
        subs = [seq_step(sub, i) for sub in range(n_sub)]
        chains = [(sub, j) for sub in range(n_sub) for j in range(3)]

        def quad(c, name):
            fac, ln = subs[c[0]][1], slice(c[1] * LANES, (c[1] + 1) * LANES)
            return jnp.concatenate([fac[0][name][:, ln], fac[1][name][:, ln]], axis=1)

        incl, strict, t_idx, s_idx = _quad_masks()
        same_head = _same_head()
        pair = [subs[c[0]][2][c[1]] for c in chains]
        g = [_mm_nt(jnp.concatenate([quad(c, "kb"), quad(c, "q")], axis=0), _qstack(quad(c, "k")))
             for c in chains]
        a2 = [jnp.where(strict, x[:CHUNK] * p, 0.0) for x, p in zip(g, pair)]
        qk = [x[CHUNK:] * p for x, p in zip(g, pair)]
        t2 = _inv_unit_tri_many(a2, t_idx, s_idx)
        uw = [_mm(t, jnp.concatenate([_qstack(quad(c, "vb")), _qstack(quad(c, "kbe"))], axis=1))
              for t, c in zip(t2, chains)]
        st = [st_ref[c[0], c[1]] for c in chains]
        wq = [_mm_nt(jnp.concatenate([x[:, QUAD:], quad(c, "qe")], axis=0), s)
              for x, c, s in zip(uw, chains, st)]
        vn = [x[:, :QUAD] - y[:CHUNK] for x, y in zip(uw, wq)]
        qv = [_mm(m, _qstack(x)) for m, x in zip(qk, vn)]
        upd = [_mm_tn(x, quad(c, "ko")) for x, c in zip(vn, chains)]
        os_ = [y[CHUNK:] + m for y, m in zip(wq, qv)]
        for c, s, x in zip(chains, st, upd):
            st_ref[c[0], c[1]] = s * quad(c, "el")[0:1] + jnp.where(same_head, x, 0.0)
        for sub in range(n_sub):
            for d in range(2):
                yd_ref[d, pl.ds(subs[sub][0][d], CHUNK), :] = jnp.concatenate(
                    [os_[sub * 3 + j][:, d * LANES:(d + 1) * LANES] for j in range(3)], axis=1)
        return carry

    lax.fori_loop(0, n_chunks, seq_body, 0)
    sfin_ref[...] = st_ref[...]

    def post_body(c, carry):
        start = pl.multiple_of(c * CHUNK, CHUNK)
        o = yd_ref[0, pl.ds(start, CHUNK), :] + yd_ref[1, pl.ds(start, CHUNK), :]
        z = z_ref[pl.ds(start, CHUNK), :]
        ms = _head_sums([o * o], bm)[0] * (1.0 / HEAD)
        y_ref[pl.ds(start, CHUNK), :] = o * lax.rsqrt(ms + RMS_EPS) * vec[0:1] * _silu(z)
        return carry

    lax.fori_loop(0, n_sub * n_chunks, post_body, 0)


def _gdn_call(p, base_blk, n_seq, t_len, prm, s0):
    has_init = s0 is not None
    dn_blk = DN_OFF // PAIRS_W
    n_sub = SEQS_PER_STEP
    assert n_seq % n_sub == 0 and base_blk % n_sub == 0
    rows = n_sub * t_len

    def pspec(col_blk):
        return pl.BlockSpec((rows, PAIRS_W), lambda b: (base_blk // n_sub + b, dn_blk + col_blk),
                            pipeline_mode=_seq_buffering(rows))

    full = lambda shape: pl.BlockSpec(shape, lambda b: (0,) * len(shape))
    in_specs = [pspec(0), pspec(1), pspec(2), pspec(3),
                pl.BlockSpec((rows, LANES), lambda b: (base_blk // n_sub + b, DN_GATE_OFF // LANES)),
                full((8, 3 * PAIRS_W)), full((8, PAIRS_W)), full((8, LANES))]
    args = [p, p, p, p, p, prm["cw"], prm["vec"], prm["gvec"]]
    if has_init:
        in_specs.append(pl.BlockSpec((n_sub, 3, QUAD, QUAD), lambda b: (b, 0, 0, 0)))
        args.append(s0)
    y, sfin = pl.pallas_call(
        functools.partial(_gdn_kernel, t_len=t_len, n_sub=n_sub, has_init=has_init),
        grid=(n_seq // n_sub,),
        in_specs=in_specs,
        out_specs=[pl.BlockSpec((rows, PAIRS_W), lambda b: (b, 0)),
                   pl.BlockSpec((n_sub, 3, QUAD, QUAD), lambda b: (b, 0, 0, 0))],
        out_shape=[jax.ShapeDtypeStruct((n_seq * t_len, DN_DIM), F32),
                   jax.ShapeDtypeStruct((n_seq, 3, QUAD, QUAD), F32)],
        scratch_shapes=[pltpu.VMEM((3, rows, PAIRS_W), F32),
                        pltpu.VMEM((2, rows, PAIRS_W), F32),
                        pltpu.VMEM((n_sub, 3, QUAD, QUAD), F32)],
        compiler_params=pltpu.CompilerParams(dimension_semantics=("arbitrary",), vmem_limit_bytes=VMEM_LIMIT),
        name="gdn_mix",
    )(*args)
    return y, sfin


def _gla_kernel(*refs, t_len, n_sub, has_init):
    if has_init:
        (q_ref, k_ref, v_ref, og_ref, gt_ref, gk2_ref, vec_ref, s0_ref,
         y_ref, sfin_ref, yd_ref, st_ref) = refs
    else:
        (q_ref, k_ref, v_ref, og_ref, gt_ref, gk2_ref, vec_ref,
         y_ref, sfin_ref, yd_ref, st_ref) = refs
    n_chunks = t_len // CHUNK
    vec = vec_ref[...]
    sr = _iota((GLA_DIM, GLA_KDIM), 0)
    sc = _iota((GLA_DIM, GLA_KDIM), 1)
    st_mask = _shr(sr, GLA_DV) == _shr(sc, GLA_DK)

    if has_init:
        st_ref[...] = s0_ref[...]
    else:
        st_ref[...] = jnp.zeros_like(st_ref)

    def seq_body(i, carry):
        chains = [(sub, d) for sub in range(n_sub) for d in range(2)]
        starts = [pl.multiple_of(sub * t_len + (i if d == 0 else n_chunks - 1 - i) * CHUNK, CHUNK)
                  for sub, d in chains]
        t_idx = _iota((CHUNK, GLA_DIM), 0)
        s_idx = _iota((CHUNK, GLA_DIM), 1) & (CHUNK - 1)
        k_lane = _shr(_iota((CHUNK, GLA_KDIM), 1), GLA_DK)
        q = [q_ref[pl.ds(s, CHUNK), :] * (GLA_DK ** -0.5) for s in starts]
        k = [k_ref[pl.ds(s, CHUNK), :] for s in starts]
        v = [v_ref[pl.ds(s, CHUNK), :] for s in starts]
        gk = [_log_sigmoid(_mm(gt_ref[pl.ds(s, CHUNK), :], gk2_ref[c[1]]) + vec[c[1]:c[1] + 1, :GLA_KDIM])
              * (1.0 / GLA_GATE_NORM) for s, c in zip(starts, chains)]
        b = [_cumsum_rows(x, c[1]) for x, c in zip(gk, chains)]
        tot = [jnp.sum(x, axis=0, keepdims=True) for x in gk]
        kt = [kk * jnp.exp(0.5 * tt - bb) for kk, tt, bb in zip(k, tot, b)]
        k_stack = [jnp.concatenate([jnp.where(k_lane == h, x, 0.0) for h in range(GLA_HEADS)], axis=0) for x in kt]
        scores = [_mm_nt(qq * jnp.exp(bb - 0.5 * tt), ks) for qq, bb, tt, ks in zip(q, b, tot, k_stack)]
        scores = [jnp.where((s_idx <= t_idx) if c[1] == 0 else (s_idx >= t_idx), x, 0.0)
                  for x, c in zip(scores, chains)]
        st = [st_ref[c[0], c[1]] for c in chains]
        o_intra = [_mm(x, _qstack(vv)) for x, vv in zip(scores, v)]
        o_inter = [_mm_nt(qq * jnp.exp(bb), s) for qq, bb, s in zip(q, b, st)]
        upd = [_mm_tn(vv, kk * jnp.exp(tt - bb)) for vv, kk, tt, bb in zip(v, k, tot, b)]
        for idx, c in enumerate(chains):
            st_ref[c[0], c[1]] = st[idx] * jnp.exp(tot[idx]) + jnp.where(st_mask, upd[idx], 0.0)
            yd_ref[c[1], pl.ds(starts[idx], CHUNK), :] = o_intra[idx] + o_inter[idx]
        return carry

    lax.fori_loop(0, n_chunks, seq_body, 0)
    sfin_ref[...] = st_ref[...]

    bm = _block_ones(LANES, GLA_DV).astype(BF16)

    def post_body(c, carry):
        start = pl.multiple_of(c * CHUNK, CHUNK)
        o = yd_ref[0, pl.ds(start, CHUNK), :] + yd_ref[1, pl.ds(start, CHUNK), :]
        og = og_ref[pl.ds(start, CHUNK), :]
        ms = _head_sums([o * o], bm)[0] * (1.0 / GLA_DV)
        y_ref[pl.ds(start, CHUNK), :] = o * lax.rsqrt(ms + RMS_EPS) * vec[2:3] * _silu(og)
        return carry

    lax.fori_loop(0, n_sub * n_chunks, post_body, 0)


def _gla_call(p, base_blk, n_seq, t_len, prm, s0):
    has_init = s0 is not None
    t128 = GLA_OFF // LANES
    t256 = GLA_OFF // GLA_DIM
    n_sub = SEQS_PER_STEP
    assert n_seq % n_sub == 0 and base_blk % n_sub == 0
    rows = n_sub * t_len
    blk = lambda b: base_blk // n_sub + b
    in_specs = [pl.BlockSpec((rows, LANES), lambda b: (blk(b), t128)),
                pl.BlockSpec((rows, LANES), lambda b: (blk(b), t128 + 1)),
                pl.BlockSpec((rows, GLA_DIM), lambda b: (blk(b), t256 + 1)),
                pl.BlockSpec((rows, GLA_DIM), lambda b: (blk(b), t256 + 2)),
                pl.BlockSpec((rows, LANES), lambda b: (blk(b), t128 + 6)),
                pl.BlockSpec((2, LANES, LANES), lambda b: (0, 0, 0)),
                pl.BlockSpec((8, GLA_DIM), lambda b: (0, 0))]
    args = [p, p, p, p, p, prm["gk2"], prm["vec"]]
    if has_init:
        in_specs.append(pl.BlockSpec((n_sub, 2, GLA_DIM, GLA_KDIM), lambda b: (b, 0, 0, 0)))
        args.append(s0)
    y, sfin = pl.pallas_call(
        functools.partial(_gla_kernel, t_len=t_len, n_sub=n_sub, has_init=has_init),
        grid=(n_seq // n_sub,),
        in_specs=in_specs,
        out_specs=[pl.BlockSpec((rows, GLA_DIM), lambda b: (b, 0)),
                   pl.BlockSpec((n_sub, 2, GLA_DIM, GLA_KDIM), lambda b: (b, 0, 0, 0))],
        out_shape=[jax.ShapeDtypeStruct((n_seq * t_len, GLA_DIM), F32),
                   jax.ShapeDtypeStruct((n_seq, 2, GLA_DIM, GLA_KDIM), F32)],
        scratch_shapes=[pltpu.VMEM((2, rows, GLA_DIM), F32),
                        pltpu.VMEM((n_sub, 2, GLA_DIM, GLA_KDIM), F32)],
        compiler_params=pltpu.CompilerParams(dimension_semantics=("arbitrary",), vmem_limit_bytes=VMEM_LIMIT),
        name="gla_mix",
    )(*args)
    return y, sfin


def _out_kernel(yr_ref, yg_ref, yd_ref, x_ref, g1_ref, sh_ref, sc_ref, nw_ref, wo_ref, wr_ref, br_ref,
                x1_ref, h2_ref, gate_ref):
    mix = (_mm(yr_ref[...], wo_ref[0:RW_DIM, :]) + _mm(yg_ref[...], wo_ref[RW_DIM:RW_DIM + GLA_DIM, :])
           + _mm(yd_ref[...], wo_ref[RW_DIM + GLA_DIM:, :]))
    x1 = x_ref[...] + g1_ref[0] * mix
    x1_ref[...] = x1
    h2 = x1 * lax.rsqrt(jnp.mean(x1 * x1, axis=-1, keepdims=True) + RMS_EPS) * nw_ref[...]
    h2 = h2 * (1.0 + sc_ref[0]) + sh_ref[0]
    h2_ref[...] = h2
    logits = _mmh(h2, wr_ref[...]) + br_ref[...]
    lane_i = _iota(logits.shape, 1)
    lane = lane_i.astype(F32)
    neg = -jnp.inf
    big = 1e9
    lg = jnp.where(lane_i < N_GROUPS, logits, neg)
    gmax = jnp.max(lg, axis=1, keepdims=True)
    gsel = jnp.min(jnp.where(lg == gmax, lane, big), axis=1, keepdims=True)
    p_grp = 1.0 / jnp.sum(jnp.exp(lg - gmax), axis=1, keepdims=True)
    e_lane = lane_i - N_GROUPS
    e_grp = _shr(jnp.maximum(e_lane, 0), EXPERTS_PER_GROUP).astype(F32)
    in_grp = (e_lane >= 0) & (e_lane < N_EXPERTS) & (e_grp == gsel)
    le = jnp.where(in_grp, logits, neg)
    v1 = jnp.max(le, axis=1, keepdims=True)
    i1 = jnp.min(jnp.where(le == v1, lane, big), axis=1, keepdims=True)
    le2 = jnp.where(lane == i1, neg, le)
    v2 = jnp.max(le2, axis=1, keepdims=True)
    i2 = jnp.min(jnp.where(le2 == v2, lane, big), axis=1, keepdims=True)
    e2 = jnp.exp(v2 - v1)
    p1 = p_grp / (1.0 + e2)
    p2 = p_grp * e2 / (1.0 + e2)
    gate_ref[...] = jnp.where(lane == i1, p1, jnp.where(lane == i2, p2, 0.0))


def _out_call(y_rw, y_gla, y_dn, x, mod3, layer, nw, w_out, wr, br, n_ctx_tok, lat_t, tile=256):
    n = x.shape[0]
    row = lambda i: layer * 8 + _cond_row(i, tile, n_ctx_tok, lat_t)
    def modspec(part):
        return pl.BlockSpec((1, 1, D_MODEL), lambda i: (row(i), 0, part))

    return pl.pallas_call(
        _out_kernel,
        grid=(n // tile,),
        in_specs=[pl.BlockSpec((tile, RW_DIM), lambda i: (i, 0)),
                  pl.BlockSpec((tile, GLA_DIM), lambda i: (i, 0)),
                  pl.BlockSpec((tile, DN_DIM), lambda i: (i, 0)),
                  pl.BlockSpec((tile, D_MODEL), lambda i: (i, 0)),
                  modspec(2), modspec(3), modspec(4),
                  pl.BlockSpec((1, D_MODEL), lambda i: (0, 0)),
                  pl.BlockSpec((D_MODEL, D_MODEL), lambda i: (0, 0)),
                  pl.BlockSpec((D_MODEL, LANES), lambda i: (0, 0)),
                  pl.BlockSpec((1, LANES), lambda i: (0, 0))],
        out_specs=[pl.BlockSpec((tile, D_MODEL), lambda i: (i, 0)),
                   pl.BlockSpec((tile, D_MODEL), lambda i: (i, 0)),
                   pl.BlockSpec((tile, LANES), lambda i: (i, 0))],
        out_shape=[jax.ShapeDtypeStruct((n, D_MODEL), F32),
                   jax.ShapeDtypeStruct((n, D_MODEL), F32),
                   jax.ShapeDtypeStruct((n, LANES), F32)],
        compiler_params=pltpu.CompilerParams(dimension_semantics=("arbitrary",), vmem_limit_bytes=VMEM_LIMIT),
        name="out_proj_router",
    )(y_rw, y_gla, y_dn, x, mod3, mod3, mod3, nw.reshape(1, D_MODEL), w_out, wr, br)


def _moe_kernel(h_ref, gate_ref, x_ref, g2_ref, wg_ref, wu_ref, wd_ref, o_ref, acc_ref):
    e = pl.program_id(1)

    @pl.when(e == 0)
    def _():
        acc_ref[...] = jnp.zeros_like(acc_ref)

    h = h_ref[...].astype(BF16)
    gates = gate_ref[...]
    lane = _iota(gates.shape, 1)
    gcol = jnp.sum(jnp.where(lane == N_GROUPS + e, gates, 0.0), axis=1, keepdims=True)
    hid = _silu(jnp.dot(h, wg_ref[0].astype(BF16), preferred_element_type=F32))
    hid = hid * jnp.dot(h, wu_ref[0].astype(BF16), preferred_element_type=F32) * gcol
    acc_ref[...] += _mm(hid, wd_ref[0])

    @pl.when(e == N_EXPERTS - 1)
    def _():
        o_ref[...] = x_ref[...] + g2_ref[0] * acc_ref[...]


def _moe_call(h2, gates, x1, mod3, layer, w_gate, w_up, w_down, n_ctx_tok, lat_t, tile=1024):
    n = h2.shape[0]
    row = lambda i: layer * 8 + _cond_row(i, tile, n_ctx_tok, lat_t)
    return pl.pallas_call(
        _moe_kernel,
        grid=(n // tile, N_EXPERTS),
        in_specs=[pl.BlockSpec((tile, D_MODEL), lambda i, e: (i, 0)),
                  pl.BlockSpec((tile, LANES), lambda i, e: (i, 0)),
                  pl.BlockSpec((tile, D_MODEL), lambda i, e: (i, 0)),
                  pl.BlockSpec((1, 1, D_MODEL), lambda i, e: (row(i), 0, 5)),
                  pl.BlockSpec((1, D_MODEL, D_EXPERT), lambda i, e: (layer * N_EXPERTS + e, 0, 0)),
                  pl.BlockSpec((1, D_MODEL, D_EXPERT), lambda i, e: (layer * N_EXPERTS + e, 0, 0)),
                  pl.BlockSpec((1, D_EXPERT, D_MODEL), lambda i, e: (layer * N_EXPERTS + e, 0, 0))],
        out_specs=pl.BlockSpec((tile, D_MODEL), lambda i, e: (i, 0)),
        out_shape=jax.ShapeDtypeStruct((n, D_MODEL), F32),
        scratch_shapes=[pltpu.VMEM((tile, D_MODEL), F32)],
        compiler_params=pltpu.CompilerParams(dimension_semantics=("arbitrary", "arbitrary"),
                                             vmem_limit_bytes=VMEM_LIMIT),
        name="moe_dense",
    )(h2, gates, x1, mod3, w_gate, w_up, w_down)


def _norm_kernel(x_ref, w_ref, o_ref):
    x = x_ref[...]
    o_ref[...] = x * lax.rsqrt(jnp.mean(x * x, axis=-1, keepdims=True) + RMS_EPS) * w_ref[...]


def _norm_call(x, w, base_tile, n_tiles, tile=256):
    return pl.pallas_call(
        _norm_kernel,
        grid=(n_tiles,),
        in_specs=[pl.BlockSpec((tile, D_MODEL), lambda i: (base_tile + i, 0)),
                  pl.BlockSpec((1, D_MODEL), lambda i: (0, 0))],
        out_specs=pl.BlockSpec((tile, D_MODEL), lambda i: (i, 0)),
        out_shape=jax.ShapeDtypeStruct((n_tiles * tile, D_MODEL), F32),
        compiler_params=pltpu.CompilerParams(dimension_semantics=("arbitrary",), vmem_limit_bytes=VMEM_LIMIT),
        name="final_norm",
    )(x, w.reshape(1, D_MODEL))


def _pos_embed(n_tokens, dim):
    rows = n_tokens // GRID_W
    row = jnp.broadcast_to(jnp.arange(rows)[:, None], (rows, GRID_W)).reshape(-1)
    col = jnp.broadcast_to(jnp.arange(GRID_W)[None, :], (rows, GRID_W)).reshape(-1)
    quarter = dim // 4
    omega = 1.0 / (POS_BASE ** (jnp.arange(quarter, dtype=F32) / quarter))

    def axis_embed(pos):
        ang = pos.astype(F32)[:, None] * omega[None, :]
        return jnp.concatenate([jnp.sin(ang), jnp.cos(ang)], axis=-1)

    return jnp.concatenate([axis_embed(row), axis_embed(col)], axis=-1)


def _relayout_w_in(w_in_l):
    rw = w_in_l[:, :RW_COLS]
    g0 = RW_COLS
    q, k, v = w_in_l[:, g0:g0 + 128], w_in_l[:, g0 + 128:g0 + 256], w_in_l[:, g0 + 256:g0 + 512]
    gates = w_in_l[:, g0 + 512:g0 + 544]
    og = w_in_l[:, g0 + 544:g0 + 800]
    d0 = g0 + 800
    qkvz = w_in_l[:, d0:d0 + 1536]
    dgates = w_in_l[:, d0 + 1536:d0 + 1560]
    z = lambda n: jnp.zeros((D_MODEL, n), w_in_l.dtype)
    return jnp.concatenate([rw, qkvz, q, k, v, og, gates, z(96), dgates, z(104)], axis=1).astype(BF16)


def _rwkv_params(mu, w0, w2, a0, a2, g2, kk, ka, rk, ln_w, ln_b):
    rows = [w0[0], w0[1], a0[0], a0[1], kk, ka, rk, ln_w, ln_b]
    vec = jnp.stack(rows + [jnp.zeros_like(kk)] * 7)
    zero = jnp.zeros((64, RW_DIM), F32)
    w2c = jnp.stack([jnp.concatenate([w2[0], zero]), jnp.concatenate([zero, w2[1]])])
    a2c = jnp.stack([jnp.concatenate([a2[0], zero]), jnp.concatenate([zero, a2[1]])])
    return {"mu": mu, "vec": vec, "w2": w2c, "a2": a2c, "g2": g2}


def _gdn_params(conv, a_log, dt_bias, norm_w):
    cw = jnp.concatenate([conv, jnp.zeros((5, 3 * DN_DIM), F32)], axis=0)
    vec = jnp.zeros((8, DN_DIM), F32).at[0].set(jnp.tile(norm_w, DN_HEADS))
    gvec = jnp.zeros((8, LANES), F32)
    gvec = gvec.at[0, 0:12].set(a_log.reshape(12)).at[1, 0:12].set(dt_bias.reshape(12))
    return {"cw": cw, "vec": vec, "gvec": gvec}


def _gla_params(gk2, gk_b, norm_w):
    g = jnp.zeros((2, LANES, LANES), F32)
    g = g.at[0, 0:16].set(gk2[0]).at[1, 16:32].set(gk2[1])
    vec = jnp.zeros((8, GLA_DIM), F32)
    vec = vec.at[0, :GLA_KDIM].set(gk_b[0]).at[1, :GLA_KDIM].set(gk_b[1]).at[2].set(jnp.tile(norm_w, GLA_HEADS))
    return {"gk2": g, "vec": vec}


def _pair_state_in(s):
    b = s.shape[0]
    st = jnp.swapaxes(s, -1, -2).reshape(b, 2, 3, 2, HEAD, HEAD)
    st = jnp.transpose(st, (0, 2, 1, 3, 4, 5)).reshape(b, 3, 4, HEAD, HEAD)
    bd = jnp.einsum("bjgvk,gh->bjgvhk", st, jnp.eye(4, dtype=s.dtype))
    return bd.reshape(b, 3, QUAD, QUAD)


def _pair_state_out(st):
    b = st.shape[0]
    x = st.reshape(b, 3, 4, HEAD, 4, HEAD)
    diag = jnp.stack([x[:, :, g, :, g, :] for g in range(4)], axis=2)
    diag = diag.reshape(b, 3, 2, 2, HEAD, HEAD)
    diag = jnp.transpose(diag, (0, 2, 1, 3, 5, 4))
    return diag.reshape(b, 2, 6, HEAD, HEAD)


def _gla_state_in(s):
    b = s.shape[0]
    st = jnp.swapaxes(s, -1, -2)
    eye = jnp.eye(GLA_HEADS, dtype=s.dtype)
    bd = jnp.einsum("bdhvk,hg->bdhvgk", st, eye)
    return bd.reshape(b, 2, GLA_DIM, GLA_KDIM)


def _gla_state_out(st):
    b = st.shape[0]
    x = st.reshape(b, 2, GLA_HEADS, GLA_DV, GLA_HEADS, GLA_DK)
    diag = jnp.stack([x[:, :, h, :, h, :] for h in range(GLA_HEADS)], axis=2)
    return jnp.swapaxes(diag, -1, -2)


def kernel(x_prompt, x_sample, c, state_rwkv, state_gla, state_delta, c_ctx, ada_w, ada_b, norm1_w, norm2_w, final_norm_w, w_in, w_out, rwkv_mu, rwkv_w0, rwkv_w2, rwkv_a0, rwkv_a2, rwkv_g2, rwkv_kk, rwkv_ka, rwkv_rk, rwkv_ln_w, rwkv_ln_b, gla_gk2, gla_gk_b, gla_norm_w, dn_conv, dn_a_log, dn_dt_bias, dn_norm_w, moe_wg, moe_bg, moe_we, moe_be, moe_w_gate, moe_w_up, moe_w_down):
    n_ctx, t_ctx, d = x_prompt.shape
    n_lat, t_lat, _ = x_sample.shape
    depth = ada_w.shape[0]
    n_ctx_tok = n_ctx * t_ctx
    n_lat_tok = n_lat * t_lat
    assert d == D_MODEL and n_ctx_tok % 1024 == 0 and t_lat % 1024 == 0 and t_ctx % 256 == 0 and n_lat <= 7
    assert n_ctx_tok % t_lat == 0

    xs = x_sample + _pos_embed(t_lat, d)[None].astype(x_sample.dtype)
    x = jnp.concatenate([x_prompt.reshape(n_ctx_tok, d), xs.reshape(n_lat_tok, d)], axis=0)

    cond8 = jnp.zeros((8, d), F32).at[0].set(c_ctx).at[1:1 + n_lat].set(c)
    mod = _ada_call(cond8, ada_w, ada_b)
    mod3 = mod.reshape(depth * 8, 1, 6 * d)

    w_gate = moe_w_gate.reshape(depth * N_EXPERTS, d, D_EXPERT)
    w_up = moe_w_up.reshape(depth * N_EXPERTS, d, D_EXPERT)
    w_down = moe_w_down.reshape(depth * N_EXPERTS, D_EXPERT, d)

    lat_base = n_ctx_tok // t_lat
    rw_states, gla_states, dn_states = [], [], []
    for l in range(depth):
        p = _in_call(x, mod3, l, norm1_w[l], _relayout_w_in(w_in[l]), n_ctx_tok, t_lat)
        rw_p = _rwkv_params(rwkv_mu[l], rwkv_w0[l], rwkv_w2[l], rwkv_a0[l], rwkv_a2[l], rwkv_g2[l], rwkv_kk[l],
                            rwkv_ka[l], rwkv_rk[l], rwkv_ln_w[l], rwkv_ln_b[l])
        dn_p = _gdn_params(dn_conv[l], dn_a_log[l], dn_dt_bias[l], dn_norm_w[l])
        gla_p = _gla_params(gla_gk2[l], gla_gk_b[l], gla_norm_w[l])

        y_rw_c, s_rw = _rwkv_call(p, 0, n_ctx, t_ctx, rw_p, None)
        y_rw_l, _ = _rwkv_call(p, lat_base, n_lat, t_lat, rw_p, _pair_state_in(state_rwkv[:, l]))
        y_dn_c, s_dn = _gdn_call(p, 0, n_ctx, t_ctx, dn_p, None)
        y_dn_l, _ = _gdn_call(p, lat_base, n_lat, t_lat, dn_p, _pair_state_in(state_delta[:, l]))
        y_gla_c, s_gla = _gla_call(p, 0, n_ctx, t_ctx, gla_p, None)
        y_gla_l, _ = _gla_call(p, lat_base, n_lat, t_lat, gla_p, _gla_state_in(state_gla[:, l]))
        rw_states.append(_pair_state_out(s_rw))
        dn_states.append(_pair_state_out(s_dn))
        gla_states.append(_gla_state_out(s_gla))

        wr = jnp.zeros((d, LANES), F32)
        wr = wr.at[:, :N_GROUPS].set(moe_wg[l]).at[:, N_GROUPS:N_GROUPS + N_EXPERTS].set(moe_we[l].reshape(d, N_EXPERTS))
        br = jnp.zeros((1, LANES), F32)
        br = br.at[0, :N_GROUPS].set(moe_bg[l]).at[0, N_GROUPS:N_GROUPS + N_EXPERTS].set(moe_be[l].reshape(N_EXPERTS))
        x1, h2, gates = _out_call(jnp.concatenate([y_rw_c, y_rw_l]), jnp.concatenate([y_gla_c, y_gla_l]),
                                  jnp.concatenate([y_dn_c, y_dn_l]), x, mod3, l, norm2_w[l],
                                  w_out[l].astype(BF16), wr, br, n_ctx_tok, t_lat)
        x = _moe_call(h2, gates, x1, mod3, l, w_gate, w_up, w_down, n_ctx_tok, t_lat)

    y_prompt = _norm_call(x, final_norm_w, 0, n_ctx_tok // 256).reshape(n_ctx, t_ctx, d)
    y_sample = _norm_call(x, final_norm_w, n_ctx_tok // 256, n_lat_tok // 256).reshape(n_lat, t_lat, d)
    new_rw = jnp.stack(rw_states, axis=1).astype(x_prompt.dtype)
    new_gla = jnp.stack(gla_states, axis=1).astype(x_prompt.dtype)
    new_dn = jnp.stack(dn_states, axis=1).astype(x_prompt.dtype)
    return (y_prompt, y_sample, new_rw, new_gla, new_dn)
```

```python
import functools
import math

import numpy as np
import jax
import jax.numpy as jnp
from jax import lax
from jax.experimental import pallas as pl
from jax.experimental.pallas import tpu as pltpu

F32 = jnp.float32
BF16 = jnp.bfloat16
HI = lax.Precision.HIGHEST

D_MODEL = 1024
DEPTH = 4
GRID_W = 64
POS_BASE = 10000.0
RMS_EPS = 1e-6
L2_EPS = 1e-6

RW_HEADS = 6
RW_DIM = 384
RW_COLS = 1536
RW_LN_EPS = 64e-5
GLA_HEADS = 4
GLA_DK = 32
GLA_DV = 64
GLA_KDIM = 128
GLA_DIM = 256
GLA_GATE_RANK = 16
GLA_GATE_NORM = 16.0
DN_HEADS = 6
DN_DIM = 384
N_GROUPS = 4
EXPERTS_PER_GROUP = 4
N_EXPERTS = 16
D_EXPERT = 512

LANES = 128
CHUNK = 64
HEAD = 64
PAIRS_W = 384
DN_OFF = RW_COLS
GLA_OFF = DN_OFF + 4 * PAIRS_W
GLA_PCOLS = 896
DN_GATE_OFF = GLA_OFF + GLA_PCOLS
P_COLS = DN_GATE_OFF + LANES
VMEM_LIMIT = 56 * 1024 * 1024
SEQS_PER_STEP = 2


def _seq_buffering(rows):
    return pl.Buffered(1) if rows * PAIRS_W * 4 > (2 << 20) else None


def _mm(a, b):
    return jnp.dot(a.astype(BF16), b.astype(BF16), preferred_element_type=F32)


def _mm_nt(a, b):
    return lax.dot_general(a.astype(BF16), b.astype(BF16), (((1,), (1,)), ((), ())),
                           preferred_element_type=F32)


def _mm_tn(a, b):
    return lax.dot_general(a.astype(BF16), b.astype(BF16), (((0,), (0,)), ((), ())),
                           preferred_element_type=F32)


def _mmh(a, b):
    return jnp.dot(a, b, precision=HI, preferred_element_type=F32)


def _mmh_nt(a, b):
    return lax.dot_general(a, b, (((1,), (1,)), ((), ())), precision=HI, preferred_element_type=F32)


def _iota(shape, dim):
    return lax.broadcasted_iota(jnp.int32, shape, dim)


def _sigmoid(x):
    return 1.0 / (1.0 + jnp.exp(-x))


def _silu(x):
    return x * _sigmoid(x)


def _softplus(x):
    return jnp.maximum(x, 0.0) + jnp.log(1.0 + jnp.exp(-jnp.abs(x)))


def _log_sigmoid(x):
    return -_softplus(-x)


def _tri(n, d, strict):
    r = _iota((n, n), 0)
    c = _iota((n, n), 1)
    if d == 0:
        return (c < r) if strict else (c <= r)
    return (c > r) if strict else (c >= r)


QUAD = 4 * HEAD


def _quad_masks():
    t = _iota((CHUNK, QUAD), 0)
    lane = _iota((CHUNK, QUAD), 1)
    s = lane & (CHUNK - 1)
    fwd = lane < 2 * HEAD
    bwd = jnp.logical_not(fwd)
    strict = (fwd & (s < t)) | (bwd & (s > t))
    incl = strict | (s == t)
    return incl, strict, t, s


def _same_head():
    return (_iota((QUAD, QUAD), 0) >> 6) == (_iota((QUAD, QUAD), 1) >> 6)


def _qstack(a):
    grp = _iota(a.shape, 1) >> 6
    return jnp.concatenate([jnp.where(grp == v, a, 0.0) for v in range(4)], axis=0)


def _inv_unit_tri_many(ls, t, s):
    same_blk = lambda width: _shr(t, width) == _shr(s, width)
    mul = lambda a, b: _mm(a, _qstack(b))
    n0 = [jnp.where(same_blk(8), -l, 0.0) for l in ls]
    p1 = [mul(x, x) for x in n0]
    inv = [jnp.where(s == t, 1.0, 0.0) + x for x in n0]
    p2 = [mul(x, x) for x in p1]
    inv = [x + mul(x, p) for x, p in zip(inv, p1)]
    inv = [x + mul(x, p) for x, p in zip(inv, p2)]
    for width in (8, 16, 32):
        blk = same_blk(2 * width) & jnp.logical_not(same_blk(width))
        half = [mul(x, jnp.where(blk, l, 0.0)) for x, l in zip(inv, ls)]
        inv = [x - mul(h, x) for x, h in zip(inv, half)]
    return inv


def _split2(x):
    hi = x.astype(BF16)
    return hi, (x - hi.astype(F32)).astype(BF16)


def _head_sums(xs, bm):
    n_tiles = xs[0].shape[1] // LANES
    tiles = [x[:, j * LANES:(j + 1) * LANES] for x in xs for j in range(n_tiles)]
    parts = [p for tl in tiles for p in _split2(tl)]
    out = jnp.dot(jnp.concatenate(parts, axis=0), bm, preferred_element_type=F32)
    res = []
    for i in range(len(xs)):
        cols = []
        for j in range(n_tiles):
            base = (i * n_tiles + j) * 2 * CHUNK
            cols.append(out[base:base + CHUNK] + out[base + CHUNK:base + 2 * CHUNK])
        res.append(jnp.concatenate(cols, axis=1))
    return res


def _cumsum_rows(x, d):
    row = _iota(x.shape, 0)
    k = 1
    while k < CHUNK:
        if d == 0:
            x = x + jnp.where(row >= k, pltpu.roll(x, k, 0), 0.0)
        else:
            x = x + jnp.where(row < CHUNK - k, pltpu.roll(x, CHUNK - k, 0), 0.0)
        k *= 2
    return x


def _shr(x, width):
    return x >> (width.bit_length() - 1)


def _block_ones(n, width):
    r = _iota((n, n), 0)
    c = _iota((n, n), 1)
    return jnp.where(_shr(r, width) == _shr(c, width), 1.0, 0.0).astype(F32)


def _shifted(ref, off, start, c, n_chunks, t_len):
    pc = ref[pl.ds(off + start, CHUNK), :]
    pb = ref[pl.ds(pl.multiple_of(off + jnp.maximum(start - 8, 0), 8), 8), :]
    nb = ref[pl.ds(pl.multiple_of(off + jnp.minimum(start + CHUNK, t_len - 8), 8), 8), :]
    carry_p = jnp.where(c > 0, pb[7:8, :], 0.0)
    carry_n = jnp.where(c < n_chunks - 1, nb[0:1, :], 0.0)
    row = _iota(pc.shape, 0)
    prev = jnp.where(row == 0, carry_p, pltpu.roll(pc, 1, 0))
    nxt = jnp.where(row == CHUNK - 1, carry_n, pltpu.roll(pc, CHUNK - 1, 0))
    return pc, prev, nxt


def _ada_kernel(c_ref, w_ref, b_ref, o_ref):
    c = c_ref[...]
    o_ref[0] = _mm(_silu(c), w_ref[0]) + b_ref[0]


def _ada_call(cond8, ada_w, ada_b):
    nl = ada_w.shape[0]
    return pl.pallas_call(
        _ada_kernel,
        grid=(nl, 6),
        in_specs=[pl.BlockSpec((8, D_MODEL), lambda l, j: (0, 0)),
                  pl.BlockSpec((1, D_MODEL, D_MODEL), lambda l, j: (l, 0, j)),
                  pl.BlockSpec((1, 1, D_MODEL), lambda l, j: (l, 0, j))],
        out_specs=pl.BlockSpec((1, 8, D_MODEL), lambda l, j: (l, 0, j)),
        out_shape=jax.ShapeDtypeStruct((nl, 8, 6 * D_MODEL), F32),
        compiler_params=pltpu.CompilerParams(dimension_semantics=("arbitrary", "arbitrary"),
                                             vmem_limit_bytes=VMEM_LIMIT),
        name="ada_mod",
    )(cond8, ada_w, ada_b.reshape(nl, 1, 6 * D_MODEL))


def _cond_row(i, tile, n_ctx_tok, lat_t):
    tok = i * tile
    return jnp.where(tok < n_ctx_tok, 0, 1 + (tok - n_ctx_tok) // lat_t)


def _in_kernel(x_ref, sh_ref, sc_ref, nw_ref, w_ref, o_ref):
    x = x_ref[...]
    y = x * lax.rsqrt(jnp.mean(x * x, axis=-1, keepdims=True) + RMS_EPS) * nw_ref[...]
    h = y * (1.0 + sc_ref[0]) + sh_ref[0]
    o_ref[...] = jnp.dot(h.astype(BF16), w_ref[...], preferred_element_type=F32)


def _in_call(x, mod3, layer, nw, w_in_p, n_ctx_tok, lat_t, tile=256):
    n = x.shape[0]
    row = lambda i: layer * 8 + _cond_row(i, tile, n_ctx_tok, lat_t)
    return pl.pallas_call(
        _in_kernel,
        grid=(n // tile,),
        in_specs=[pl.BlockSpec((tile, D_MODEL), lambda i: (i, 0)),
                  pl.BlockSpec((1, 1, D_MODEL), lambda i: (row(i), 0, 0)),
                  pl.BlockSpec((1, 1, D_MODEL), lambda i: (row(i), 0, 1)),
                  pl.BlockSpec((1, D_MODEL), lambda i: (0, 0)),
                  pl.BlockSpec((D_MODEL, P_COLS), lambda i: (0, 0))],
        out_specs=pl.BlockSpec((tile, P_COLS), lambda i: (i, 0)),
        out_shape=jax.ShapeDtypeStruct((n, P_COLS), F32),
        compiler_params=pltpu.CompilerParams(dimension_semantics=("arbitrary",), vmem_limit_bytes=VMEM_LIMIT),
        name="in_proj",
    )(x, mod3, mod3, nw.reshape(1, D_MODEL), w_in_p)


EXP_M05 = math.exp(-0.5)


def _rwkv_kernel(*refs, t_len, n_sub, has_init):
    if has_init:
        (r_ref, k_ref, v_ref, x3_ref, mu_ref, vec_ref, w2_ref, a2_ref, g2_ref, s0_ref,
         y_ref, sfin_ref, xs_ref, yd_ref, st_ref) = refs
    else:
        (r_ref, k_ref, v_ref, x3_ref, mu_ref, vec_ref, w2_ref, a2_ref, g2_ref,
         y_ref, sfin_ref, xs_ref, yd_ref, st_ref) = refs
    n_chunks = t_len // CHUNK
    vec = vec_ref[...]
    bm = _block_ones(LANES, HEAD).astype(BF16)

    def shift_body(c, carry):
        start = pl.multiple_of(c * CHUNK, CHUNK)
        for sub in range(n_sub):
            for idx, ref in enumerate((r_ref, k_ref, v_ref, x3_ref)):
                pc, prev, nxt = _shifted(ref, sub * t_len, start, c, n_chunks, t_len)
                mu0 = mu_ref[0:1, idx * PAIRS_W:(idx + 1) * PAIRS_W]
                mu1 = mu_ref[1:2, idx * PAIRS_W:(idx + 1) * PAIRS_W]
                xs_ref[idx, pl.ds(sub * t_len + start, CHUNK), :] = pc + mu0 * (prev - pc) + mu1 * (nxt - pc)
        return carry

    lax.fori_loop(0, n_chunks, shift_body, 0)

    if has_init:
        st_ref[...] = s0_ref[...]
    else:
        st_ref[...] = jnp.zeros_like(st_ref)

    def prep(d, start):
        k = xs_ref[1, pl.ds(start, CHUNK), :]
        x3 = xs_ref[3, pl.ds(start, CHUNK), :]
        xw, xa = x3[:, 0:LANES], x3[:, LANES:2 * LANES]
        w_raw = vec[d:d + 1] + _mm(jnp.tanh(xw), w2_ref[d])
        lw = -EXP_M05 * _sigmoid(w_raw)
        a = _sigmoid(vec[2 + d:3 + d] + _mm(xa, a2_ref[d]))
        kmod = k * (1.0 + (a - 1.0) * vec[5:6])
        cum = _cumsum_rows(lw, d)
        half = 0.5 * jnp.sum(lw, axis=0, keepdims=True)
        return k * vec[4:5], a, kmod, cum, lw, half

    def seq_step(sub, i):
        off = sub * t_len
        starts = (pl.multiple_of(off + i * CHUNK, CHUNK), pl.multiple_of(off + (n_chunks - 1 - i) * CHUNK, CHUNK))
        pre = [prep(d, starts[d]) for d in range(2)]
        norms = _head_sums([pre[0][0] * pre[0][0], pre[1][0] * pre[1][0]], bm)
        fac = []
        for d in range(2):
            kk, a, kmod, cum, lw, half = pre[d]
            r = xs_ref[0, pl.ds(starts[d], CHUNK), :]
            kk = kk * lax.rsqrt(norms[d] + L2_EPS)
            em = jnp.exp(half)
            e_up = jnp.exp(half - cum)
            rh = r * jnp.exp(cum - half)
            ch = kk * jnp.exp(cum - lw - half)
            bh = kk * a * e_up
            kh = kmod * e_up
            v = xs_ref[2, pl.ds(starts[d], CHUNK), :]
            fac.append(dict(rh=rh, ch=ch, bh=bh, kh=kh, c0=ch * em, r0=rh * em, bt=bh * em, kt=kh * em, v=v,
                            dec=jnp.broadcast_to(em * em, (CHUNK, PAIRS_W))))
        return starts, fac

    def seq_body(i, carry):
        subs = [seq_step(sub, i) for sub in range(n_sub)]
        chains = [(sub, j) for sub in range(n_sub) for j in range(3)]

        def quad(c, name):
            fac, ln = subs[c[0]][1], slice(c[1] * LANES, (c[1] + 1) * LANES)
            return jnp.concatenate([fac[0][name][:, ln], fac[1][name][:, ln]], axis=1)

        incl, strict, t_idx, s_idx = _quad_masks()
        same_head = _same_head()
        v = [quad(c, "v") for c in chains]
        g = [_mm_nt(jnp.concatenate([quad(c, "ch"), quad(c, "rh")], axis=0),
                    jnp.concatenate([_qstack(quad(c, "bh")), _qstack(quad(c, "kh"))], axis=0))
             for c in chains]
        l_cb = [jnp.where(strict, x[:CHUNK, :QUAD], 0.0) for x in g]
        l_ck = [jnp.where(strict, x[:CHUNK, QUAD:], 0.0) for x in g]
        m_rb = [jnp.where(incl, x[CHUNK:, :QUAD], 0.0) for x in g]
        m_rk = [jnp.where(incl, x[CHUNK:, QUAD:], 0.0) for x in g]
        lv_y0 = [_mm(jnp.concatenate([a, b], axis=0), _qstack(x)) for a, b, x in zip(l_ck, m_rk, v)]
        t2 = _inv_unit_tri_many(l_cb, t_idx, s_idx)
        wu = [_mm(t, jnp.concatenate([_qstack(quad(c, "c0")), _qstack(x[:CHUNK])], axis=1))
              for t, c, x in zip(t2, chains, lv_y0)]
        st = [st_ref[c[0], c[1]] for c in chains]
        wr = [_mm_nt(jnp.concatenate([x[:, :QUAD], quad(c, "r0")], axis=0), s)
              for x, c, s in zip(wu, chains, st)]
        u = [-x[:, QUAD:] - y[:CHUNK] for x, y in zip(wu, wr)]
        mu_ = [_mm(m, _qstack(x)) for m, x in zip(m_rb, u)]
        upd = [_mm_tn(jnp.concatenate([x, vv], axis=0), jnp.concatenate([quad(c, "bt"), quad(c, "kt")], axis=0))
               for x, vv, c in zip(u, v, chains)]
        ys = [a[CHUNK:] + b[CHUNK:] + m for a, b, m in zip(lv_y0, wr, mu_)]
        for c, s, x in zip(chains, st, upd):
            st_ref[c[0], c[1]] = s * quad(c, "dec")[0:1] + jnp.where(same_head, x, 0.0)
        for sub in range(n_sub):
            for d in range(2):
                yd_ref[d, pl.ds(subs[sub][0][d], CHUNK), :] = jnp.concatenate(
                    [ys[sub * 3 + j][:, d * LANES:(d + 1) * LANES] for j in range(3)], axis=1)
        return carry

    lax.fori_loop(0, n_chunks, seq_body, 0)
    sfin_ref[...] = st_ref[...]

    def post_body(c, carry):
        start = pl.multiple_of(c * CHUNK, CHUNK)
        r = xs_ref[0, pl.ds(start, CHUNK), :]
        k = xs_ref[1, pl.ds(start, CHUNK), :]
        v = xs_ref[2, pl.ds(start, CHUNK), :]
        xg = xs_ref[3, pl.ds(start, CHUNK), 2 * LANES:3 * LANES]
        y = yd_ref[0, pl.ds(start, CHUNK), :] + yd_ref[1, pl.ds(start, CHUNK), :]
        mean, rk = _head_sums([y, r * k * vec[6:7]], bm)
        yc = y - mean * (1.0 / HEAD)
        var = _head_sums([yc * yc], bm)[0] * (1.0 / HEAD)
        yn = yc * lax.rsqrt(var + RW_LN_EPS) * vec[7:8] + vec[8:9]
        g = _mm(_sigmoid(xg), g2_ref[...])
        y_ref[pl.ds(start, CHUNK), :] = (yn + rk * v) * g
        return carry

    lax.fori_loop(0, n_sub * n_chunks, post_body, 0)


def _rwkv_call(p, base_blk, n_seq, t_len, prm, s0):
    has_init = s0 is not None
    n_sub = SEQS_PER_STEP
    assert n_seq % n_sub == 0 and base_blk % n_sub == 0
    rows = n_sub * t_len

    def pspec(col_blk):
        return pl.BlockSpec((rows, PAIRS_W), lambda b: (base_blk // n_sub + b, col_blk),
                            pipeline_mode=_seq_buffering(rows))

    full = lambda shape: pl.BlockSpec(shape, lambda b: (0,) * len(shape))
    in_specs = [pspec(0), pspec(1), pspec(2), pspec(3),
                full((2, RW_COLS)), full((16, PAIRS_W)), full((2, LANES, PAIRS_W)), full((2, LANES, PAIRS_W)),
                full((LANES, PAIRS_W))]
    args = [p, p, p, p, prm["mu"], prm["vec"], prm["w2"], prm["a2"], prm["g2"]]
    if has_init:
        in_specs.append(pl.BlockSpec((n_sub, 3, QUAD, QUAD), lambda b: (b, 0, 0, 0)))
        args.append(s0)
    y, sfin = pl.pallas_call(
        functools.partial(_rwkv_kernel, t_len=t_len, n_sub=n_sub, has_init=has_init),
        grid=(n_seq // n_sub,),
        in_specs=in_specs,
        out_specs=[pl.BlockSpec((rows, PAIRS_W), lambda b: (b, 0)),
                   pl.BlockSpec((n_sub, 3, QUAD, QUAD), lambda b: (b, 0, 0, 0))],
        out_shape=[jax.ShapeDtypeStruct((n_seq * t_len, RW_DIM), F32),
                   jax.ShapeDtypeStruct((n_seq, 3, QUAD, QUAD), F32)],
        scratch_shapes=[pltpu.VMEM((4, rows, PAIRS_W), F32),
                        pltpu.VMEM((2, rows, PAIRS_W), F32),
                        pltpu.VMEM((n_sub, 3, QUAD, QUAD), F32)],
        compiler_params=pltpu.CompilerParams(dimension_semantics=("arbitrary",), vmem_limit_bytes=VMEM_LIMIT),
        name="rwkv7_mix",
    )(*args)
    return y, sfin


def _gdn_kernel(*refs, t_len, n_sub, has_init):
    if has_init:
        (q_ref, k_ref, v_ref, z_ref, gt_ref, cw_ref, vec_ref, gvec_ref, s0_ref,
         y_ref, sfin_ref, xs_ref, yd_ref, st_ref) = refs
    else:
        (q_ref, k_ref, v_ref, z_ref, gt_ref, cw_ref, vec_ref, gvec_ref,
         y_ref, sfin_ref, xs_ref, yd_ref, st_ref) = refs
    n_chunks = t_len // CHUNK
    vec = vec_ref[...]
    gvec = gvec_ref[...]
    bm = _block_ones(LANES, HEAD).astype(BF16)

    def conv_body(c, carry):
        start = pl.multiple_of(c * CHUNK, CHUNK)
        for sub in range(n_sub):
            for idx, ref in enumerate((q_ref, k_ref, v_ref)):
                pc, prev, nxt = _shifted(ref, sub * t_len, start, c, n_chunks, t_len)
                cw = cw_ref[:, idx * PAIRS_W:(idx + 1) * PAIRS_W]
                xs_ref[idx, pl.ds(sub * t_len + start, CHUNK), :] = _silu(
                    cw[0:1] * prev + cw[1:2] * pc + cw[2:3] * nxt)
        return carry

    lax.fori_loop(0, n_chunks, conv_body, 0)

    if has_init:
        st_ref[...] = s0_ref[...]
    else:
        st_ref[...] = jnp.zeros_like(st_ref)

    def lane_bcast(tile, lane, width):
        return jnp.broadcast_to(tile[:, lane:lane + 1], (tile.shape[0], width))

    def per_head(tile, first_lane):
        return jnp.concatenate([lane_bcast(tile, first_lane + h, HEAD) for h in range(DN_HEADS)], axis=1)

    def seq_step(sub, i):
        off = sub * t_len
        starts = (pl.multiple_of(off + i * CHUNK, CHUNK), pl.multiple_of(off + (n_chunks - 1 - i) * CHUNK, CHUNK))
        qs = [xs_ref[0, pl.ds(starts[d], CHUNK), :] for d in range(2)]
        ks = [xs_ref[1, pl.ds(starts[d], CHUNK), :] for d in range(2)]
        norms = _head_sums([qs[0] * qs[0], ks[0] * ks[0], qs[1] * qs[1], ks[1] * ks[1]], bm)
        fac, gam_tiles = [], []
        for d in range(2):
            gt = gt_ref[pl.ds(starts[d], CHUNK), :]
            q = qs[d] * lax.rsqrt(norms[2 * d] + L2_EPS) * (HEAD ** -0.5)
            k = ks[d] * lax.rsqrt(norms[2 * d + 1] + L2_EPS)
            v = xs_ref[2, pl.ds(starts[d], CHUNK), :]
            g_tile = -jnp.exp(gvec[0:1]) * _softplus(gt + gvec[1:2])
            gam_tile = _cumsum_rows(g_tile, d)
            gam = per_head(gam_tile, 6 * d)
            beta = per_head(_sigmoid(gt), 12 + 6 * d)
            last = gam[CHUNK - 1:CHUNK] if d == 0 else gam[0:1]
            eg = jnp.exp(gam)
            kb = k * beta
            fac.append(dict(q=q, k=k, kb=kb, vb=v * beta, kbe=kb * eg, qe=q * eg, ko=k * jnp.exp(last - gam), gam=gam,
                            el=jnp.broadcast_to(jnp.exp(last), (CHUNK, PAIRS_W))))
            gam_tiles.append(gam_tile)
        gam_rows = [jnp.transpose(gam_tiles[d]) for d in range(2)]
        incl = _quad_masks()[0]
        pairs = []
        for j in range(3):
            ln = slice(j * LANES, (j + 1) * LANES)
            gam_c = jnp.concatenate([fac[0]["gam"][:, ln], fac[1]["gam"][:, ln]], axis=1)
            gam_r = jnp.concatenate([gam_rows[d][6 * d + 2 * j + hh:6 * d + 2 * j + hh + 1, :]
                                     for d in range(2) for hh in range(2)], axis=1)
            pairs.append(jnp.exp(jnp.where(incl, gam_c - gam_r, -jnp.inf)))
        return starts, fac, pairs

    def seq_body(i, carry):
        subs = [seq_step(sub, i) for sub in range(n_sub)]
        chains = [(sub, j) for sub in range(n_sub) for j in range(3)]

        def quad(c, name):
            fac, ln = subs[c[0]][1], slice(c[1] * LANES, (c[1] + 1) * LANES)
            return jnp.concatenate([fac[0][name][:, ln], fac[1][name][:, ln]], axis=1)

        incl, strict, t_idx, s_idx = _quad_masks()
        same_head = _same_head()
        pair = [subs[c[0]][2][c[1]] for c in chains]
        g = [_mm_nt(jnp.concatenate([quad(c, "kb"), quad(c, "q")], axis=0), _qstack(quad(c, "k")))
             for c in chains]
        a2 = [jnp.where(strict, x[:CHUNK] * p, 0.0) for x, p in zip(g, pair)]
        qk = [x[CHUNK:] * p for x, p in zip(g, pair)]
        t2 = _inv_unit_tri_many(a2, t_idx, s_idx)
        uw = [_mm(t, jnp.concatenate([_qstack(quad(c, "vb")), _qstack(quad(c, "kbe"))], axis=1))
              for t, c in zip(t2, chains)]
        st = [st_ref[c[0], c[1]] for c in chains]
        wq = [_mm_nt(jnp.concatenate([x[:, QUAD:], quad(c, "qe")], axis=0), s)
              for x, c, s in zip(uw, chains, st)]
        vn = [x[:, :QUAD] - y[:CHUNK] for x, y in zip(uw, wq)]
        qv = [_mm(m, _qstack(x)) for m, x in zip(qk, vn)]
        upd = [_mm_tn(x, quad(c, "ko")) for x, c in zip(vn, chains)]
        os_ = [y[CHUNK:] + m for y, m in zip(wq, qv)]
        for c, s, x in zip(chains, st, upd):
            st_ref[c[0], c[1]] = s * quad(c, "el")[0:1] + jnp.where(same_head, x, 0.0)
        for sub in range(n_sub):
            for d in range(2):
                yd_ref[d, pl.ds(subs[sub][0][d], CHUNK), :] = jnp.concatenate(
                    [os_[sub * 3 + j][:, d * LANES:(d + 1) * LANES] for j in range(3)], axis=1)
        return carry

    lax.fori_loop(0, n_chunks, seq_body, 0)
    sfin_ref[...] = st_ref[...]

    def post_body(c, carry):
        start = pl.multiple_of(c * CHUNK, CHUNK)
        o = yd_ref[0, pl.ds(start, CHUNK), :] + yd_ref[1, pl.ds(start, CHUNK), :]
        z = z_ref[pl.ds(start, CHUNK), :]
        ms = _head_sums([o * o], bm)[0] * (1.0 / HEAD)
        y_ref[pl.ds(start, CHUNK), :] = o * lax.rsqrt(ms + RMS_EPS) * vec[0:1] * _silu(z)
        return carry

    lax.fori_loop(0, n_sub * n_chunks, post_body, 0)


def _gdn_call(p, base_blk, n_seq, t_len, prm, s0):
    has_init = s0 is not None
    dn_blk = DN_OFF // PAIRS_W
    n_sub = SEQS_PER_STEP
    assert n_seq % n_sub == 0 and base_blk % n_sub == 0
    rows = n_sub * t_len

    def pspec(col_blk):
        return pl.BlockSpec((rows, PAIRS_W), lambda b: (base_blk // n_sub + b, dn_blk + col_blk),
                            pipeline_mode=_seq_buffering(rows))

    full = lambda shape: pl.BlockSpec(shape, lambda b: (0,) * len(shape))
    in_specs = [pspec(0), pspec(1), pspec(2), pspec(3),
                pl.BlockSpec((rows, LANES), lambda b: (base_blk // n_sub + b, DN_GATE_OFF // LANES)),
                full((8, 3 * PAIRS_W)), full((8, PAIRS_W)), full((8, LANES))]
    args = [p, p, p, p, p, prm["cw"], prm["vec"], prm["gvec"]]
    if has_init:
        in_specs.append(pl.BlockSpec((n_sub, 3, QUAD, QUAD), lambda b: (b, 0, 0, 0)))
        args.append(s0)
    y, sfin = pl.pallas_call(
        functools.partial(_gdn_kernel, t_len=t_len, n_sub=n_sub, has_init=has_init),
        grid=(n_seq // n_sub,),
        in_specs=in_specs,
        out_specs=[pl.BlockSpec((rows, PAIRS_W), lambda b: (b, 0)),
                   pl.BlockSpec((n_sub, 3, QUAD, QUAD), lambda b: (b, 0, 0, 0))],
        out_shape=[jax.ShapeDtypeStruct((n_seq * t_len, DN_DIM), F32),
                   jax.ShapeDtypeStruct((n_seq, 3, QUAD, QUAD), F32)],
        scratch_shapes=[pltpu.VMEM((3, rows, PAIRS_W), F32),
                        pltpu.VMEM((2, rows, PAIRS_W), F32),
                        pltpu.VMEM((n_sub, 3, QUAD, QUAD), F32)],
        compiler_params=pltpu.CompilerParams(dimension_semantics=("arbitrary",), vmem_limit_bytes=VMEM_LIMIT),
        name="gdn_mix",
    )(*args)
    return y, sfin


def _gla_kernel(*refs, t_len, n_sub, has_init):
    if has_init:
        (q_ref, k_ref, v_ref, og_ref, gt_ref, gk2_ref, vec_ref, s0_ref,
         y_ref, sfin_ref, yd_ref, st_ref) = refs
    else:
        (q_ref, k_ref, v_ref, og_ref, gt_ref, gk2_ref, vec_ref,
         y_ref, sfin_ref, yd_ref, st_ref) = refs
    n_chunks = t_len // CHUNK
    vec = vec_ref[...]
    sr = _iota((GLA_DIM, GLA_KDIM), 0)
    sc = _iota((GLA_DIM, GLA_KDIM), 1)
    st_mask = _shr(sr, GLA_DV) == _shr(sc, GLA_DK)

    if has_init:
        st_ref[...] = s0_ref[...]
    else:
        st_ref[...] = jnp.zeros_like(st_ref)

    def seq_body(i, carry):
        chains = [(sub, d) for sub in range(n_sub) for d in range(2)]
        starts = [pl.multiple_of(sub * t_len + (i if d == 0 else n_chunks - 1 - i) * CHUNK, CHUNK)
                  for sub, d in chains]
        t_idx = _iota((CHUNK, GLA_DIM), 0)
        s_idx = _iota((CHUNK, GLA_DIM), 1) & (CHUNK - 1)
        k_lane = _shr(_iota((CHUNK, GLA_KDIM), 1), GLA_DK)
        q = [q_ref[pl.ds(s, CHUNK), :] * (GLA_DK ** -0.5) for s in starts]
        k = [k_ref[pl.ds(s, CHUNK), :] for s in starts]
        v = [v_ref[pl.ds(s, CHUNK), :] for s in starts]
        gk = [_log_sigmoid(_mm(gt_ref[pl.ds(s, CHUNK), :], gk2_ref[c[1]]) + vec[c[1]:c[1] + 1, :GLA_KDIM])
              * (1.0 / GLA_GATE_NORM) for s, c in zip(starts, chains)]
        b = [_cumsum_rows(x, c[1]) for x, c in zip(gk, chains)]
        tot = [jnp.sum(x, axis=0, keepdims=True) for x in gk]
        kt = [kk * jnp.exp(0.5 * tt - bb) for kk, tt, bb in zip(k, tot, b)]
        k_stack = [jnp.concatenate([jnp.where(k_lane == h, x, 0.0) for h in range(GLA_HEADS)], axis=0) for x in kt]
        scores = [_mm_nt(qq * jnp.exp(bb - 0.5 * tt), ks) for qq, bb, tt, ks in zip(q, b, tot, k_stack)]
        scores = [jnp.where((s_idx <= t_idx) if c[1] == 0 else (s_idx >= t_idx), x, 0.0)
                  for x, c in zip(scores, chains)]
        st = [st_ref[c[0], c[1]] for c in chains]
        o_intra = [_mm(x, _qstack(vv)) for x, vv in zip(scores, v)]
        o_inter = [_mm_nt(qq * jnp.exp(bb), s) for qq, bb, s in zip(q, b, st)]
        upd = [_mm_tn(vv, kk * jnp.exp(tt - bb)) for vv, kk, tt, bb in zip(v, k, tot, b)]
        for idx, c in enumerate(chains):
            st_ref[c[0], c[1]] = st[idx] * jnp.exp(tot[idx]) + jnp.where(st_mask, upd[idx], 0.0)
            yd_ref[c[1], pl.ds(starts[idx], CHUNK), :] = o_intra[idx] + o_inter[idx]
        return carry

    lax.fori_loop(0, n_chunks, seq_body, 0)
    sfin_ref[...] = st_ref[...]

    bm = _block_ones(LANES, GLA_DV).astype(BF16)

    def post_body(c, carry):
        start = pl.multiple_of(c * CHUNK, CHUNK)
        o = yd_ref[0, pl.ds(start, CHUNK), :] + yd_ref[1, pl.ds(start, CHUNK), :]
        og = og_ref[pl.ds(start, CHUNK), :]
        ms = _head_sums([o * o], bm)[0] * (1.0 / GLA_DV)
        y_ref[pl.ds(start, CHUNK), :] = o * lax.rsqrt(ms + RMS_EPS) * vec[2:3] * _silu(og)
        return carry

    lax.fori_loop(0, n_sub * n_chunks, post_body, 0)


def _gla_call(p, base_blk, n_seq, t_len, prm, s0):
    has_init = s0 is not None
    t128 = GLA_OFF // LANES
    t256 = GLA_OFF // GLA_DIM
    n_sub = SEQS_PER_STEP
    assert n_seq % n_sub == 0 and base_blk % n_sub == 0
    rows = n_sub * t_len
    blk = lambda b: base_blk // n_sub + b
    in_specs = [pl.BlockSpec((rows, LANES), lambda b: (blk(b), t128)),
                pl.BlockSpec((rows, LANES), lambda b: (blk(b), t128 + 1)),
                pl.BlockSpec((rows, GLA_DIM), lambda b: (blk(b), t256 + 1)),
                pl.BlockSpec((rows, GLA_DIM), lambda b: (blk(b), t256 + 2)),
                pl.BlockSpec((rows, LANES), lambda b: (blk(b), t128 + 6)),
                pl.BlockSpec((2, LANES, LANES), lambda b: (0, 0, 0)),
                pl.BlockSpec((8, GLA_DIM), lambda b: (0, 0))]
    args = [p, p, p, p, p, prm["gk2"], prm["vec"]]
    if has_init:
        in_specs.append(pl.BlockSpec((n_sub, 2, GLA_DIM, GLA_KDIM), lambda b: (b, 0, 0, 0)))
        args.append(s0)
    y, sfin = pl.pallas_call(
        functools.partial(_gla_kernel, t_len=t_len, n_sub=n_sub, has_init=has_init),
        grid=(n_seq // n_sub,),
        in_specs=in_specs,
        out_specs=[pl.BlockSpec((rows, GLA_DIM), lambda b: (b, 0)),
                   pl.BlockSpec((n_sub, 2, GLA_DIM, GLA_KDIM), lambda b: (b, 0, 0, 0))],
        out_shape=[jax.ShapeDtypeStruct((n_seq * t_len, GLA_DIM), F32),
                   jax.ShapeDtypeStruct((n_seq, 2, GLA_DIM, GLA_KDIM), F32)],
        scratch_shapes=[pltpu.VMEM((2, rows, GLA_DIM), F32),
                        pltpu.VMEM((n_sub, 2, GLA_DIM, GLA_KDIM), F32)],
        compiler_params=pltpu.CompilerParams(dimension_semantics=("arbitrary",), vmem_limit_bytes=VMEM_LIMIT),
        name="gla_mix",
    )(*args)
    return y, sfin


def _out_kernel(yr_ref, yg_ref, yd_ref, x_ref, g1_ref, sh_ref, sc_ref, nw_ref, wo_ref, wr_ref, br_ref,
                x1_ref, h2_ref, grp_ref):
    mix = (_mm(yr_ref[...], wo_ref[0:RW_DIM, :]) + _mm(yg_ref[...], wo_ref[RW_DIM:RW_DIM + GLA_DIM, :])
           + _mm(yd_ref[...], wo_ref[RW_DIM + GLA_DIM:, :]))
    x1 = x_ref[...] + g1_ref[0] * mix
    x1_ref[...] = x1
    h2 = x1 * lax.rsqrt(jnp.mean(x1 * x1, axis=-1, keepdims=True) + RMS_EPS) * nw_ref[...]
    h2 = h2 * (1.0 + sc_ref[0]) + sh_ref[0]
    for s in range(ROW_SLABS):
        h2_ref[:, s, :] = h2[:, s * LANES:(s + 1) * LANES]
    logits = _mmh(h2, wr_ref[...]) + br_ref[...]
    lane_i = _iota(logits.shape, 1)
    lg = jnp.where(lane_i < N_GROUPS, logits, -jnp.inf)
    gmax = jnp.max(lg, axis=1, keepdims=True)
    gsel = jnp.min(jnp.where(lg == gmax, lane_i.astype(F32), 1e9), axis=1, keepdims=True)
    grp_ref[...] = jnp.broadcast_to(gsel, logits.shape).astype(jnp.int32)


def _out_call(y_rw, y_gla, y_dn, x, mod3, layer, nw, w_out, wr, br, n_ctx_tok, lat_t, tile=256):
    n = x.shape[0]
    row = lambda i: layer * 8 + _cond_row(i, tile, n_ctx_tok, lat_t)
    def modspec(part):
        return pl.BlockSpec((1, 1, D_MODEL), lambda i: (row(i), 0, part))

    return pl.pallas_call(
        _out_kernel,
        grid=(n // tile,),
        in_specs=[pl.BlockSpec((tile, RW_DIM), lambda i: (i, 0)),
                  pl.BlockSpec((tile, GLA_DIM), lambda i: (i, 0)),
                  pl.BlockSpec((tile, DN_DIM), lambda i: (i, 0)),
                  pl.BlockSpec((tile, D_MODEL), lambda i: (i, 0)),
                  modspec(2), modspec(3), modspec(4),
                  pl.BlockSpec((1, D_MODEL), lambda i: (0, 0)),
                  pl.BlockSpec((D_MODEL, D_MODEL), lambda i: (0, 0)),
                  pl.BlockSpec((D_MODEL, LANES), lambda i: (0, 0)),
                  pl.BlockSpec((1, LANES), lambda i: (0, 0))],
        out_specs=[pl.BlockSpec((tile, D_MODEL), lambda i: (i, 0)),
                   pl.BlockSpec((tile, ROW_SLABS, LANES), lambda i: (i, 0, 0)),
                   pl.BlockSpec((tile, LANES), lambda i: (i, 0))],
        out_shape=[jax.ShapeDtypeStruct((n, D_MODEL), F32),
                   jax.ShapeDtypeStruct((n, ROW_SLABS, LANES), F32),
                   jax.ShapeDtypeStruct((n, LANES), jnp.int32)],
        compiler_params=pltpu.CompilerParams(dimension_semantics=("arbitrary",), vmem_limit_bytes=VMEM_LIMIT),
        name="out_proj_router",
    )(y_rw, y_gla, y_dn, x, mod3, mod3, mod3, nw.reshape(1, D_MODEL), w_out, wr, br)


ROW_SLABS = D_MODEL // LANES
MOE_TILE = 512
DMA_TILE = 256


def _route_slots(grp, n_tok):
    gid = grp[:, 0]
    onehot = (gid[:, None] == jnp.arange(N_GROUPS, dtype=jnp.int32)[None, :]).astype(jnp.int32)
    csum = jnp.cumsum(onehot, axis=0)
    rank = jnp.sum(csum * onehot, axis=1) - 1
    counts = csum[-1]
    tiles_per = (counts + MOE_TILE - 1) // MOE_TILE
    tile_end = jnp.cumsum(tiles_per)
    offs = (tile_end - tiles_per) * MOE_TILE
    slot = jnp.sum(offs[None, :] * onehot, axis=1) + rank
    max_tiles = n_tok // MOE_TILE + N_GROUPS
    tile_group = jnp.sum((jnp.arange(max_tiles, dtype=jnp.int32)[:, None] >= tile_end[None, :]).astype(jnp.int32), axis=1)
    tile_group = jnp.minimum(tile_group, N_GROUPS - 1)
    return slot.astype(jnp.int32), tile_group.astype(jnp.int32), tile_end[-1:].astype(jnp.int32), max_tiles


def _dispatch_kernel(slot_ref, src_ref, init_ref, dst_ref, sem):
    del init_ref
    base = pl.program_id(0) * DMA_TILE

    def issue(r, carry):
        t = base + r
        pltpu.make_async_copy(src_ref.at[t], dst_ref.at[slot_ref[t]], sem).start()
        return carry

    lax.fori_loop(0, DMA_TILE, issue, 0)

    def drain(r, carry):
        pltpu.make_async_copy(src_ref.at[0], dst_ref.at[0], sem).wait()
        return carry

    lax.fori_loop(0, DMA_TILE, drain, 0)


def _dispatch_call(slot, rows, n_rows_out):
    n = rows.shape[0]
    init = jnp.zeros((n_rows_out, ROW_SLABS, LANES), F32)
    return pl.pallas_call(
        _dispatch_kernel,
        grid_spec=pltpu.PrefetchScalarGridSpec(
            num_scalar_prefetch=1,
            grid=(n // DMA_TILE,),
            in_specs=[pl.BlockSpec(memory_space=pl.ANY), pl.BlockSpec(memory_space=pl.ANY)],
            out_specs=pl.BlockSpec(memory_space=pl.ANY),
            scratch_shapes=[pltpu.SemaphoreType.DMA(())]),
        out_shape=jax.ShapeDtypeStruct((n_rows_out, ROW_SLABS, LANES), F32),
        input_output_aliases={2: 0},
        compiler_params=pltpu.CompilerParams(dimension_semantics=("arbitrary",), vmem_limit_bytes=VMEM_LIMIT,
                                             has_side_effects=True),
        name="moe_dispatch",
    )(slot, rows, init)


def _moe_group_kernel(tg_ref, nt_ref, x_ref, wr_ref, br_ref, wg_ref, wu_ref, wd_ref, o_ref, acc_ref, gate_ref,
                      xb_ref):
    i = pl.program_id(0)
    e = pl.program_id(1)

    @pl.when(i < nt_ref[0])
    def _():
        @pl.when(e == 0)
        def _():
            acc_ref[...] = jnp.zeros_like(acc_ref)
            x = jnp.concatenate([x_ref[:, s, :] for s in range(ROW_SLABS)], axis=1)
            xb_ref[...] = x.astype(BF16)
            logits = _mmh(x, wr_ref[...]) + br_ref[...]
            lane_i = _iota(logits.shape, 1)
            lane = lane_i.astype(F32)
            grp = tg_ref[i]
            lg = jnp.where(lane_i < N_GROUPS, logits, -jnp.inf)
            gmax = jnp.max(lg, axis=1, keepdims=True)
            l_sel = jnp.sum(jnp.where(lane_i == grp, logits, 0.0), axis=1, keepdims=True)
            p_grp = jnp.exp(l_sel - gmax) / jnp.sum(jnp.exp(lg - gmax), axis=1, keepdims=True)
            first = N_GROUPS + grp * EXPERTS_PER_GROUP
            le = jnp.where((lane_i >= first) & (lane_i < first + EXPERTS_PER_GROUP), logits, -jnp.inf)
            v1 = jnp.max(le, axis=1, keepdims=True)
            i1 = jnp.min(jnp.where(le == v1, lane, 1e9), axis=1, keepdims=True)
            le2 = jnp.where(lane == i1, -jnp.inf, le)
            v2 = jnp.max(le2, axis=1, keepdims=True)
            i2 = jnp.min(jnp.where(le2 == v2, lane, 1e9), axis=1, keepdims=True)
            e2 = jnp.exp(v2 - v1)
            p1 = p_grp / (1.0 + e2)
            p2 = p_grp * e2 / (1.0 + e2)
            gate_ref[...] = jnp.where(lane == i1, p1, jnp.where(lane == i2, p2, 0.0))

        h = xb_ref[...]
        gates = gate_ref[...]
        lane_g = _iota(gates.shape, 1)
        gcol = jnp.sum(jnp.where(lane_g == N_GROUPS + tg_ref[i] * EXPERTS_PER_GROUP + e, gates, 0.0),
                       axis=1, keepdims=True)
        hid = _silu(jnp.dot(h, wg_ref[0].astype(BF16), preferred_element_type=F32))
        hid = hid * jnp.dot(h, wu_ref[0].astype(BF16), preferred_element_type=F32) * gcol
        acc_ref[...] += _mm(hid, wd_ref[0])

        @pl.when(e == EXPERTS_PER_GROUP - 1)
        def _():
            acc = acc_ref[...]
            for s in range(ROW_SLABS):
                o_ref[:, s, :] = acc[:, s * LANES:(s + 1) * LANES]

    @pl.when((i >= nt_ref[0]) & (e == EXPERTS_PER_GROUP - 1))
    def _():
        o_ref[...] = jnp.zeros_like(o_ref)


def _moe_group_call(tile_group, n_tiles, max_tiles, xs, layer, wr, br, w_gate, w_up, w_down):
    n_rows = xs.shape[0]
    tile_idx = lambda i, nt: jnp.minimum(i, nt[0] - 1)
    expert = lambda i, e, tg, nt: (layer * N_EXPERTS + tg[tile_idx(i, nt)] * EXPERTS_PER_GROUP
                                   + jnp.where(i < nt[0], e, EXPERTS_PER_GROUP - 1))
    wspec = lambda shape: pl.BlockSpec(shape, lambda i, e, tg, nt: (expert(i, e, tg, nt), 0, 0))
    return pl.pallas_call(
        _moe_group_kernel,
        grid_spec=pltpu.PrefetchScalarGridSpec(
            num_scalar_prefetch=2,
            grid=(max_tiles, EXPERTS_PER_GROUP),
            in_specs=[pl.BlockSpec((MOE_TILE, ROW_SLABS, LANES), lambda i, e, tg, nt: (tile_idx(i, nt), 0, 0)),
                      pl.BlockSpec((D_MODEL, LANES), lambda i, e, tg, nt: (0, 0)),
                      pl.BlockSpec((1, LANES), lambda i, e, tg, nt: (0, 0)),
                      wspec((1, D_MODEL, D_EXPERT)), wspec((1, D_MODEL, D_EXPERT)), wspec((1, D_EXPERT, D_MODEL))],
            out_specs=pl.BlockSpec((MOE_TILE, ROW_SLABS, LANES), lambda i, e, tg, nt: (i, 0, 0)),
            scratch_shapes=[pltpu.VMEM((MOE_TILE, D_MODEL), F32),
                            pltpu.VMEM((MOE_TILE, LANES), F32),
                            pltpu.VMEM((MOE_TILE, D_MODEL), BF16)]),
        out_shape=jax.ShapeDtypeStruct((n_rows, ROW_SLABS, LANES), F32),
        compiler_params=pltpu.CompilerParams(dimension_semantics=("arbitrary", "arbitrary"),
                                             vmem_limit_bytes=VMEM_LIMIT),
        name="moe_grouped",
    )(tile_group, n_tiles, xs, wr, br, w_gate, w_up, w_down)


def _combine_kernel(slot_ref, ys_ref, x_ref, g2_ref, o_ref, buf_ref, sem):
    base = pl.program_id(0) * DMA_TILE

    def issue(r, carry):
        pltpu.make_async_copy(ys_ref.at[slot_ref[base + r]], buf_ref.at[r], sem).start()
        return carry

    lax.fori_loop(0, DMA_TILE, issue, 0)

    def drain(r, carry):
        pltpu.make_async_copy(ys_ref.at[0], buf_ref.at[0], sem).wait()
        return carry

    lax.fori_loop(0, DMA_TILE, drain, 0)
    for s in range(ROW_SLABS):
        ln = slice(s * LANES, (s + 1) * LANES)
        o_ref[:, ln] = x_ref[:, ln] + g2_ref[0][:, ln] * buf_ref[:, s, :]


def _combine_call(slot, ys, x1, mod3, layer, n_ctx_tok, lat_t):
    n = x1.shape[0]
    row = lambda i: layer * 8 + _cond_row(i, DMA_TILE, n_ctx_tok, lat_t)
    return pl.pallas_call(
        _combine_kernel,
        grid_spec=pltpu.PrefetchScalarGridSpec(
            num_scalar_prefetch=1,
            grid=(n // DMA_TILE,),
            in_specs=[pl.BlockSpec(memory_space=pl.ANY),
                      pl.BlockSpec((DMA_TILE, D_MODEL), lambda i, sl: (i, 0)),
                      pl.BlockSpec((1, 1, D_MODEL), lambda i, sl: (row(i), 0, 5))],
            out_specs=pl.BlockSpec((DMA_TILE, D_MODEL), lambda i, sl: (i, 0)),
            scratch_shapes=[pltpu.VMEM((DMA_TILE, ROW_SLABS, LANES), F32), pltpu.SemaphoreType.DMA(())]),
        out_shape=jax.ShapeDtypeStruct((n, D_MODEL), F32),
        compiler_params=pltpu.CompilerParams(dimension_semantics=("arbitrary",), vmem_limit_bytes=VMEM_LIMIT),
        name="moe_combine",
    )(slot, ys, x1, mod3)


def _norm_kernel(x_ref, w_ref, o_ref):
    x = x_ref[...]
    o_ref[...] = x * lax.rsqrt(jnp.mean(x * x, axis=-1, keepdims=True) + RMS_EPS) * w_ref[...]


def _norm_call(x, w, base_tile, n_tiles, tile=256):
    return pl.pallas_call(
        _norm_kernel,
        grid=(n_tiles,),
        in_specs=[pl.BlockSpec((tile, D_MODEL), lambda i: (base_tile + i, 0)),
                  pl.BlockSpec((1, D_MODEL), lambda i: (0, 0))],
        out_specs=pl.BlockSpec((tile, D_MODEL), lambda i: (i, 0)),
        out_shape=jax.ShapeDtypeStruct((n_tiles * tile, D_MODEL), F32),
        compiler_params=pltpu.CompilerParams(dimension_semantics=("arbitrary",), vmem_limit_bytes=VMEM_LIMIT),
        name="final_norm",
    )(x, w.reshape(1, D_MODEL))


def _pos_embed(n_tokens, dim):
    rows = n_tokens // GRID_W
    row = jnp.broadcast_to(jnp.arange(rows)[:, None], (rows, GRID_W)).reshape(-1)
    col = jnp.broadcast_to(jnp.arange(GRID_W)[None, :], (rows, GRID_W)).reshape(-1)
    quarter = dim // 4
    omega = 1.0 / (POS_BASE ** (jnp.arange(quarter, dtype=F32) / quarter))

    def axis_embed(pos):
        ang = pos.astype(F32)[:, None] * omega[None, :]
        return jnp.concatenate([jnp.sin(ang), jnp.cos(ang)], axis=-1)

    return jnp.concatenate([axis_embed(row), axis_embed(col)], axis=-1)


def _relayout_w_in(w_in_l):
    rw = w_in_l[:, :RW_COLS]
    g0 = RW_COLS
    q, k, v = w_in_l[:, g0:g0 + 128], w_in_l[:, g0 + 128:g0 + 256], w_in_l[:, g0 + 256:g0 + 512]
    gates = w_in_l[:, g0 + 512:g0 + 544]
    og = w_in_l[:, g0 + 544:g0 + 800]
    d0 = g0 + 800
    qkvz = w_in_l[:, d0:d0 + 1536]
    dgates = w_in_l[:, d0 + 1536:d0 + 1560]
    z = lambda n: jnp.zeros((D_MODEL, n), w_in_l.dtype)
    return jnp.concatenate([rw, qkvz, q, k, v, og, gates, z(96), dgates, z(104)], axis=1).astype(BF16)


def _rwkv_params(mu, w0, w2, a0, a2, g2, kk, ka, rk, ln_w, ln_b):
    rows = [w0[0], w0[1], a0[0], a0[1], kk, ka, rk, ln_w, ln_b]
    vec = jnp.stack(rows + [jnp.zeros_like(kk)] * 7)
    zero = jnp.zeros((64, RW_DIM), F32)
    w2c = jnp.stack([jnp.concatenate([w2[0], zero]), jnp.concatenate([zero, w2[1]])])
    a2c = jnp.stack([jnp.concatenate([a2[0], zero]), jnp.concatenate([zero, a2[1]])])
    return {"mu": mu, "vec": vec, "w2": w2c, "a2": a2c, "g2": g2}


def _gdn_params(conv, a_log, dt_bias, norm_w):
    cw = jnp.concatenate([conv, jnp.zeros((5, 3 * DN_DIM), F32)], axis=0)
    vec = jnp.zeros((8, DN_DIM), F32).at[0].set(jnp.tile(norm_w, DN_HEADS))
    gvec = jnp.zeros((8, LANES), F32)
    gvec = gvec.at[0, 0:12].set(a_log.reshape(12)).at[1, 0:12].set(dt_bias.reshape(12))
    return {"cw": cw, "vec": vec, "gvec": gvec}


def _gla_params(gk2, gk_b, norm_w):
    g = jnp.zeros((2, LANES, LANES), F32)
    g = g.at[0, 0:16].set(gk2[0]).at[1, 16:32].set(gk2[1])
    vec = jnp.zeros((8, GLA_DIM), F32)
    vec = vec.at[0, :GLA_KDIM].set(gk_b[0]).at[1, :GLA_KDIM].set(gk_b[1]).at[2].set(jnp.tile(norm_w, GLA_HEADS))
    return {"gk2": g, "vec": vec}


def _pair_state_in(s):
    b = s.shape[0]
    st = jnp.swapaxes(s, -1, -2).reshape(b, 2, 3, 2, HEAD, HEAD)
    st = jnp.transpose(st, (0, 2, 1, 3, 4, 5)).reshape(b, 3, 4, HEAD, HEAD)
    bd = jnp.einsum("bjgvk,gh->bjgvhk", st, jnp.eye(4, dtype=s.dtype))
    return bd.reshape(b, 3, QUAD, QUAD)


def _pair_state_out(st):
    b = st.shape[0]
    x = st.reshape(b, 3, 4, HEAD, 4, HEAD)
    diag = jnp.stack([x[:, :, g, :, g, :] for g in range(4)], axis=2)
    diag = diag.reshape(b, 3, 2, 2, HEAD, HEAD)
    diag = jnp.transpose(diag, (0, 2, 1, 3, 5, 4))
    return diag.reshape(b, 2, 6, HEAD, HEAD)


def _gla_state_in(s):
    b = s.shape[0]
    st = jnp.swapaxes(s, -1, -2)
    eye = jnp.eye(GLA_HEADS, dtype=s.dtype)
    bd = jnp.einsum("bdhvk,hg->bdhvgk", st, eye)
    return bd.reshape(b, 2, GLA_DIM, GLA_KDIM)


def _gla_state_out(st):
    b = st.shape[0]
    x = st.reshape(b, 2, GLA_HEADS, GLA_DV, GLA_HEADS, GLA_DK)
    diag = jnp.stack([x[:, :, h, :, h, :] for h in range(GLA_HEADS)], axis=2)
    return jnp.swapaxes(diag, -1, -2)


def kernel(x_prompt, x_sample, c, state_rwkv, state_gla, state_delta, c_ctx, ada_w, ada_b, norm1_w, norm2_w, final_norm_w, w_in, w_out, rwkv_mu, rwkv_w0, rwkv_w2, rwkv_a0, rwkv_a2, rwkv_g2, rwkv_kk, rwkv_ka, rwkv_rk, rwkv_ln_w, rwkv_ln_b, gla_gk2, gla_gk_b, gla_norm_w, dn_conv, dn_a_log, dn_dt_bias, dn_norm_w, moe_wg, moe_bg, moe_we, moe_be, moe_w_gate, moe_w_up, moe_w_down):
    n_ctx, t_ctx, d = x_prompt.shape
    n_lat, t_lat, _ = x_sample.shape
    depth = ada_w.shape[0]
    n_ctx_tok = n_ctx * t_ctx
    n_lat_tok = n_lat * t_lat
    assert d == D_MODEL and n_ctx_tok % 1024 == 0 and t_lat % 1024 == 0 and t_ctx % 256 == 0 and n_lat <= 7
    assert n_ctx_tok % t_lat == 0

    xs = x_sample + _pos_embed(t_lat, d)[None].astype(x_sample.dtype)
    x = jnp.concatenate([x_prompt.reshape(n_ctx_tok, d), xs.reshape(n_lat_tok, d)], axis=0)

    cond8 = jnp.zeros((8, d), F32).at[0].set(c_ctx).at[1:1 + n_lat].set(c)
    mod = _ada_call(cond8, ada_w, ada_b)
    mod3 = mod.reshape(depth * 8, 1, 6 * d)

    w_gate = moe_w_gate.reshape(depth * N_EXPERTS, d, D_EXPERT)
    w_up = moe_w_up.reshape(depth * N_EXPERTS, d, D_EXPERT)
    w_down = moe_w_down.reshape(depth * N_EXPERTS, D_EXPERT, d)

    lat_base = n_ctx_tok // t_lat
    rw_states, gla_states, dn_states = [], [], []
    for l in range(depth):
        p = _in_call(x, mod3, l, norm1_w[l], _relayout_w_in(w_in[l]), n_ctx_tok, t_lat)
        rw_p = _rwkv_params(rwkv_mu[l], rwkv_w0[l], rwkv_w2[l], rwkv_a0[l], rwkv_a2[l], rwkv_g2[l], rwkv_kk[l],
                            rwkv_ka[l], rwkv_rk[l], rwkv_ln_w[l], rwkv_ln_b[l])
        dn_p = _gdn_params(dn_conv[l], dn_a_log[l], dn_dt_bias[l], dn_norm_w[l])
        gla_p = _gla_params(gla_gk2[l], gla_gk_b[l], gla_norm_w[l])

        y_rw_c, s_rw = _rwkv_call(p, 0, n_ctx, t_ctx, rw_p, None)
        y_rw_l, _ = _rwkv_call(p, lat_base, n_lat, t_lat, rw_p, _pair_state_in(state_rwkv[:, l]))
        y_dn_c, s_dn = _gdn_call(p, 0, n_ctx, t_ctx, dn_p, None)
        y_dn_l, _ = _gdn_call(p, lat_base, n_lat, t_lat, dn_p, _pair_state_in(state_delta[:, l]))
        y_gla_c, s_gla = _gla_call(p, 0, n_ctx, t_ctx, gla_p, None)
        y_gla_l, _ = _gla_call(p, lat_base, n_lat, t_lat, gla_p, _gla_state_in(state_gla[:, l]))
        rw_states.append(_pair_state_out(s_rw))
        dn_states.append(_pair_state_out(s_dn))
        gla_states.append(_gla_state_out(s_gla))

        wr = jnp.zeros((d, LANES), F32)
        wr = wr.at[:, :N_GROUPS].set(moe_wg[l]).at[:, N_GROUPS:N_GROUPS + N_EXPERTS].set(moe_we[l].reshape(d, N_EXPERTS))
        br = jnp.zeros((1, LANES), F32)
        br = br.at[0, :N_GROUPS].set(moe_bg[l]).at[0, N_GROUPS:N_GROUPS + N_EXPERTS].set(moe_be[l].reshape(N_EXPERTS))
        x1, h2_rows, grp = _out_call(jnp.concatenate([y_rw_c, y_rw_l]), jnp.concatenate([y_gla_c, y_gla_l]),
                                     jnp.concatenate([y_dn_c, y_dn_l]), x, mod3, l, norm2_w[l],
                                     w_out[l].astype(BF16), wr, br, n_ctx_tok, t_lat)
        slot, tile_group, n_tiles, max_tiles = _route_slots(grp, n_ctx_tok + n_lat_tok)
        xs_sorted = _dispatch_call(slot, h2_rows, max_tiles * MOE_TILE)
        ys_sorted = _moe_group_call(tile_group, n_tiles, max_tiles, xs_sorted, l, wr, br, w_gate, w_up, w_down)
        x = _combine_call(slot, ys_sorted, x1, mod3, l, n_ctx_tok, t_lat)

    y_prompt = _norm_call(x, final_norm_w, 0, n_ctx_tok // 256).reshape(n_ctx, t_ctx, d)
    y_sample = _norm_call(x, final_norm_w, n_ctx_tok // 256, n_lat_tok // 256).reshape(n_lat, t_lat, d)
    new_rw = jnp.stack(rw_states, axis=1).astype(x_prompt.dtype)
    new_gla = jnp.stack(gla_states, axis=1).astype(x_prompt.dtype)
    new_dn = jnp.stack(dn_states, axis=1).astype(x_prompt.dtype)
    return (y_prompt, y_sample, new_rw, new_gla, new_dn)
```

```python
import functools
import math

import jax
import jax.numpy as jnp
from jax import lax
from jax.experimental import pallas as pl
from jax.experimental.pallas import tpu as pltpu

F32 = jnp.float32
BF16 = jnp.bfloat16

D_MODEL = 1024
DEPTH = 4
GRID_W = 64
POS_BASE = 10000.0
RMS_EPS = 1e-6
L2_EPS = 1e-6

RW_HEADS = 6
RW_DIM = 384
RW_COLS = 1536
RW_LN_EPS = 64e-5
GLA_HEADS = 4
GLA_DK = 32
GLA_DV = 64
GLA_KDIM = 128
GLA_DIM = 256
GLA_GATE_RANK = 16
GLA_GATE_NORM = 16.0
DN_HEADS = 6
DN_DIM = 384
N_GROUPS = 4
EXPERTS_PER_GROUP = 4
N_EXPERTS = 16
D_EXPERT = 512

LANES = 128
CHUNK = 64
HEAD = 64
PAIRS_W = 384
DN_OFF = RW_COLS
GLA_OFF = DN_OFF + 4 * PAIRS_W
GLA_PCOLS = 896
DN_GATE_OFF = GLA_OFF + GLA_PCOLS
P_COLS = DN_GATE_OFF + LANES
VMEM_LIMIT = 56 * 1024 * 1024
SEQS_PER_STEP = 2


def _seq_buffering(rows):
    return pl.Buffered(1) if rows * PAIRS_W * 4 > (2 << 20) else None


def _mm(a, b):
    return jnp.dot(a.astype(BF16), b.astype(BF16), preferred_element_type=F32)


def _mm_nt(a, b):
    return lax.dot_general(a.astype(BF16), b.astype(BF16), (((1,), (1,)), ((), ())),
                           preferred_element_type=F32)


def _mm_tn(a, b):
    return lax.dot_general(a.astype(BF16), b.astype(BF16), (((0,), (0,)), ((), ())),
                           preferred_element_type=F32)


def _iota(shape, dim):
    return lax.broadcasted_iota(jnp.int32, shape, dim)


def _sigmoid(x):
    return 1.0 / (1.0 + jnp.exp(-x))


def _silu(x):
    return x * _sigmoid(x)


def _softplus(x):
    return jnp.maximum(x, 0.0) + jnp.log(1.0 + jnp.exp(-jnp.abs(x)))


def _log_sigmoid(x):
    return -_softplus(-x)


QUAD = 4 * HEAD


def _quad_masks():
    t = _iota((CHUNK, QUAD), 0)
    lane = _iota((CHUNK, QUAD), 1)
    s = lane & (CHUNK - 1)
    fwd = lane < 2 * HEAD
    bwd = jnp.logical_not(fwd)
    strict = (fwd & (s < t)) | (bwd & (s > t))
    incl = strict | (s == t)
    return incl, strict, t, s


def _same_head():
    return (_iota((QUAD, QUAD), 0) >> 6) == (_iota((QUAD, QUAD), 1) >> 6)


def _qstack(a):
    a = a.astype(BF16)
    grp = _iota(a.shape, 1) >> 6
    return jnp.concatenate([jnp.where(grp == v, a, jnp.zeros_like(a)) for v in range(4)], axis=0)


def _inv_unit_tri_many(ls, t, s):
    same_blk = lambda width: _shr(t, width) == _shr(s, width)
    mul = lambda a, b: _mm(a, _qstack(b))
    n0 = [jnp.where(same_blk(8), -l, 0.0) for l in ls]
    p1 = [mul(x, x) for x in n0]
    inv = [jnp.where(s == t, 1.0, 0.0) + x for x in n0]
    p1_stacked = [_qstack(x) for x in p1]
    p2 = [_mm(x, xs) for x, xs in zip(p1, p1_stacked)]
    inv = [x + _mm(x, ps) for x, ps in zip(inv, p1_stacked)]
    inv = [x + mul(x, p) for x, p in zip(inv, p2)]
    for width in (8, 16, 32):
        blk = same_blk(2 * width) & jnp.logical_not(same_blk(width))
        half = [mul(x, jnp.where(blk, l, 0.0)) for x, l in zip(inv, ls)]
        inv = [x - mul(h, x) for x, h in zip(inv, half)]
    return inv


def _emit_pair_states(st_ref, sfin_ref, n_sub):
    for sub in range(n_sub):
        for j in range(3):
            for d in range(2):
                for hh in range(2):
                    g = 2 * d + hh
                    sfin_ref[sub, d, 2 * j + hh] = st_ref[sub, j, g * HEAD:(g + 1) * HEAD, g * HEAD:(g + 1) * HEAD]


def _split2(x):
    hi = x.astype(BF16)
    return hi, (x - hi.astype(F32)).astype(BF16)


def _mm_split(x, w):
    x_hi, x_lo = _split2(x)
    w_hi, w_lo = _split2(w)
    rows = x.shape[0]
    top = jnp.dot(jnp.concatenate([x_hi, x_lo], axis=0), w_hi, preferred_element_type=F32)
    return top[:rows] + top[rows:] + jnp.dot(x_hi, w_lo, preferred_element_type=F32)


def _head_sums(xs, bm):
    n_tiles = xs[0].shape[1] // LANES
    tiles = [x[:, j * LANES:(j + 1) * LANES] for x in xs for j in range(n_tiles)]
    parts = [p for tl in tiles for p in _split2(tl)]
    out = jnp.dot(jnp.concatenate(parts, axis=0), bm, preferred_element_type=F32)
    res = []
    for i in range(len(xs)):
        cols = []
        for j in range(n_tiles):
            base = (i * n_tiles + j) * 2 * CHUNK
            cols.append(out[base:base + CHUNK] + out[base + CHUNK:base + 2 * CHUNK])
        res.append(jnp.concatenate(cols, axis=1))
    return res


def _cumsum_rows(x, d):
    row = _iota(x.shape, 0)
    k = 1
    while k < CHUNK:
        if d == 0:
            x = x + jnp.where(row >= k, pltpu.roll(x, k, 0), 0.0)
        else:
            x = x + jnp.where(row < CHUNK - k, pltpu.roll(x, CHUNK - k, 0), 0.0)
        k *= 2
    return x


def _shr(x, width):
    return x >> (width.bit_length() - 1)


def _block_ones(n, width):
    r = _iota((n, n), 0)
    c = _iota((n, n), 1)
    return jnp.where(_shr(r, width) == _shr(c, width), 1.0, 0.0).astype(F32)


def _shifted(ref, off, start, c, n_chunks, t_len):
    pc = ref[pl.ds(off + start, CHUNK), :]
    pb = ref[pl.ds(pl.multiple_of(off + jnp.maximum(start - 8, 0), 8), 8), :]
    nb = ref[pl.ds(pl.multiple_of(off + jnp.minimum(start + CHUNK, t_len - 8), 8), 8), :]
    carry_p = jnp.where(c > 0, pb[7:8, :], 0.0)
    carry_n = jnp.where(c < n_chunks - 1, nb[0:1, :], 0.0)
    row = _iota(pc.shape, 0)
    prev = jnp.where(row == 0, carry_p, pltpu.roll(pc, 1, 0))
    nxt = jnp.where(row == CHUNK - 1, carry_n, pltpu.roll(pc, CHUNK - 1, 0))
    return pc, prev, nxt


def _ada_kernel(c_ref, w_ref, b_ref, o_ref):
    c = c_ref[...]
    o_ref[0] = _mm(_silu(c), w_ref[0]) + b_ref[0]


def _ada_call(cond8, ada_w, ada_b):
    nl = ada_w.shape[0]
    return pl.pallas_call(
        _ada_kernel,
        grid=(nl, 6),
        in_specs=[pl.BlockSpec((8, D_MODEL), lambda l, j: (0, 0)),
                  pl.BlockSpec((1, D_MODEL, D_MODEL), lambda l, j: (l, 0, j)),
                  pl.BlockSpec((1, 1, D_MODEL), lambda l, j: (l, 0, j))],
        out_specs=pl.BlockSpec((1, 8, D_MODEL), lambda l, j: (l, 0, j)),
        out_shape=jax.ShapeDtypeStruct((nl, 8, 6 * D_MODEL), F32),
        compiler_params=pltpu.CompilerParams(dimension_semantics=("arbitrary", "arbitrary"),
                                             vmem_limit_bytes=VMEM_LIMIT),
        name="ada_mod",
    )(cond8, ada_w, ada_b.reshape(nl, 1, 6 * D_MODEL))


def _cond_row(i, tile, n_ctx_tok, lat_t):
    tok = i * tile
    return jnp.where(tok < n_ctx_tok, 0, 1 + (tok - n_ctx_tok) // lat_t)


def _in_kernel(x_ref, sh_ref, sc_ref, nw_ref, w_ref, o_ref):
    x = x_ref[...]
    y = x * lax.rsqrt(jnp.mean(x * x, axis=-1, keepdims=True) + RMS_EPS) * nw_ref[...]
    h = y * (1.0 + sc_ref[0]) + sh_ref[0]
    o_ref[...] = jnp.dot(h.astype(BF16), w_ref[...], preferred_element_type=F32)


def _in_call(x, mod3, layer, nw, w_in_p, n_ctx_tok, lat_t, tile=256):
    n = x.shape[0]
    row = lambda i: layer * 8 + _cond_row(i, tile, n_ctx_tok, lat_t)
    return pl.pallas_call(
        _in_kernel,
        grid=(n // tile,),
        in_specs=[pl.BlockSpec((tile, D_MODEL), lambda i: (i, 0)),
                  pl.BlockSpec((1, 1, D_MODEL), lambda i: (row(i), 0, 0)),
                  pl.BlockSpec((1, 1, D_MODEL), lambda i: (row(i), 0, 1)),
                  pl.BlockSpec((1, D_MODEL), lambda i: (0, 0)),
                  pl.BlockSpec((D_MODEL, P_COLS), lambda i: (0, 0))],
        out_specs=pl.BlockSpec((tile, P_COLS), lambda i: (i, 0)),
        out_shape=jax.ShapeDtypeStruct((n, P_COLS), F32),
        compiler_params=pltpu.CompilerParams(dimension_semantics=("arbitrary",), vmem_limit_bytes=VMEM_LIMIT),
        name="in_proj",
    )(x, mod3, mod3, nw.reshape(1, D_MODEL), w_in_p)


EXP_M05 = math.exp(-0.5)


def _rwkv_kernel(*refs, t_len, n_sub, has_init):
    if has_init:
        (r_ref, k_ref, v_ref, x3_ref, mu_ref, vec_ref, w2_ref, a2_ref, g2_ref, s0_ref,
         y_ref, sfin_ref, xs_ref, yd_ref, st_ref) = refs
    else:
        (r_ref, k_ref, v_ref, x3_ref, mu_ref, vec_ref, w2_ref, a2_ref, g2_ref,
         y_ref, sfin_ref, xs_ref, yd_ref, st_ref) = refs
    n_chunks = t_len // CHUNK
    vec = vec_ref[...]
    bm = _block_ones(LANES, HEAD).astype(BF16)

    def shift_body(c, carry):
        start = pl.multiple_of(c * CHUNK, CHUNK)
        for sub in range(n_sub):
            for idx, ref in enumerate((r_ref, k_ref, v_ref, x3_ref)):
                pc, prev, nxt = _shifted(ref, sub * t_len, start, c, n_chunks, t_len)
                mu0 = mu_ref[0:1, idx * PAIRS_W:(idx + 1) * PAIRS_W]
                mu1 = mu_ref[1:2, idx * PAIRS_W:(idx + 1) * PAIRS_W]
                xs_ref[idx, pl.ds(sub * t_len + start, CHUNK), :] = pc + mu0 * (prev - pc) + mu1 * (nxt - pc)
        return carry

    lax.fori_loop(0, n_chunks, shift_body, 0)

    if has_init:
        st_ref[...] = s0_ref[...]
    else:
        st_ref[...] = jnp.zeros_like(st_ref)

    def prep(d, start):
        k = xs_ref[1, pl.ds(start, CHUNK), :]
        x3 = xs_ref[3, pl.ds(start, CHUNK), :]
        xw, xa = x3[:, 0:LANES], x3[:, LANES:2 * LANES]
        w_raw = vec[d:d + 1] + _mm(jnp.tanh(xw), w2_ref[d])
        lw = -EXP_M05 * _sigmoid(w_raw)
        a = _sigmoid(vec[2 + d:3 + d] + _mm(xa, a2_ref[d]))
        kmod = k * (1.0 + (a - 1.0) * vec[5:6])
        cum = _cumsum_rows(lw, d)
        half = 0.5 * jnp.sum(lw, axis=0, keepdims=True)
        return k * vec[4:5], a, kmod, cum, lw, half

    def seq_step(sub, i):
        off = sub * t_len
        starts = (pl.multiple_of(off + i * CHUNK, CHUNK), pl.multiple_of(off + (n_chunks - 1 - i) * CHUNK, CHUNK))
        pre = [prep(d, starts[d]) for d in range(2)]
        norms = _head_sums([pre[0][0] * pre[0][0], pre[1][0] * pre[1][0]], bm)
        fac = []
        for d in range(2):
            kk, a, kmod, cum, lw, half = pre[d]
            r = xs_ref[0, pl.ds(starts[d], CHUNK), :]
            kk = kk * lax.rsqrt(norms[d] + L2_EPS)
            em = jnp.exp(half)
            e_up = jnp.exp(half - cum)
            rh = r * jnp.exp(cum - half)
            ch = kk * jnp.exp(cum - lw - half)
            bh = kk * a * e_up
            kh = kmod * e_up
            v = xs_ref[2, pl.ds(starts[d], CHUNK), :]
            fac.append(dict(rh=rh, ch=ch, bh=bh, kh=kh, c0=ch * em, r0=rh * em, bt=bh * em, kt=kh * em, v=v,
                            dec=jnp.broadcast_to(em * em, (CHUNK, PAIRS_W))))
        return starts, fac

    def seq_body(i, carry):
        subs = [seq_step(sub, i) for sub in range(n_sub)]
        chains = [(sub, j) for sub in range(n_sub) for j in range(3)]

        def quad(c, name):
            fac, ln = subs[c[0]][1], slice(c[1] * LANES, (c[1] + 1) * LANES)
            return jnp.concatenate([fac[0][name][:, ln], fac[1][name][:, ln]], axis=1)

        incl, strict, t_idx, s_idx = _quad_masks()
        same_head = _same_head()
        v = [quad(c, "v") for c in chains]
        g = [_mm_nt(jnp.concatenate([quad(c, "ch"), quad(c, "rh")], axis=0),
                    jnp.concatenate([_qstack(quad(c, "bh")), _qstack(quad(c, "kh"))], axis=0))
             for c in chains]
        l_cb = [jnp.where(strict, x[:CHUNK, :QUAD], 0.0) for x in g]
        l_ck = [jnp.where(strict, x[:CHUNK, QUAD:], 0.0) for x in g]
        m_rb = [jnp.where(incl, x[CHUNK:, :QUAD], 0.0) for x in g]
        m_rk = [jnp.where(incl, x[CHUNK:, QUAD:], 0.0) for x in g]
        lv_y0 = [_mm(jnp.concatenate([a, b], axis=0), _qstack(x)) for a, b, x in zip(l_ck, m_rk, v)]
        t2 = _inv_unit_tri_many(l_cb, t_idx, s_idx)
        wu = [_mm(t, jnp.concatenate([_qstack(quad(c, "c0")), _qstack(x[:CHUNK])], axis=1))
              for t, c, x in zip(t2, chains, lv_y0)]
        st = [st_ref[c[0], c[1]] for c in chains]
        wr = [_mm_nt(jnp.concatenate([x[:, :QUAD], quad(c, "r0")], axis=0), s)
              for x, c, s in zip(wu, chains, st)]
        u = [-x[:, QUAD:] - y[:CHUNK] for x, y in zip(wu, wr)]
        mu_ = [_mm(m, _qstack(x)) for m, x in zip(m_rb, u)]
        upd = [_mm_tn(jnp.concatenate([x, vv], axis=0), jnp.concatenate([quad(c, "bt"), quad(c, "kt")], axis=0))
               for x, vv, c in zip(u, v, chains)]
        ys = [a[CHUNK:] + b[CHUNK:] + m for a, b, m in zip(lv_y0, wr, mu_)]
        for c, s, x in zip(chains, st, upd):
            st_ref[c[0], c[1]] = s * quad(c, "dec")[0:1] + jnp.where(same_head, x, 0.0)
        for sub in range(n_sub):
            for d in range(2):
                yd_ref[d, pl.ds(subs[sub][0][d], CHUNK), :] = jnp.concatenate(
                    [ys[sub * 3 + j][:, d * LANES:(d + 1) * LANES] for j in range(3)], axis=1)
        return carry

    lax.fori_loop(0, n_chunks, seq_body, 0)
    _emit_pair_states(st_ref, sfin_ref, n_sub)

    def post_body(c, carry):
        start = pl.multiple_of(c * CHUNK, CHUNK)
        r = xs_ref[0, pl.ds(start, CHUNK), :]
        k = xs_ref[1, pl.ds(start, CHUNK), :]
        v = xs_ref[2, pl.ds(start, CHUNK), :]
        xg = xs_ref[3, pl.ds(start, CHUNK), 2 * LANES:3 * LANES]
        y = yd_ref[0, pl.ds(start, CHUNK), :] + yd_ref[1, pl.ds(start, CHUNK), :]
        mean, rk = _head_sums([y, r * k * vec[6:7]], bm)
        yc = y - mean * (1.0 / HEAD)
        var = _head_sums([yc * yc], bm)[0] * (1.0 / HEAD)
        yn = yc * lax.rsqrt(var + RW_LN_EPS) * vec[7:8] + vec[8:9]
        g = _mm(_sigmoid(xg), g2_ref[...])
        y_ref[pl.ds(start, CHUNK), :] = (yn + rk * v) * g
        return carry

    lax.fori_loop(0, n_sub * n_chunks, post_body, 0)


def _rwkv_call(p, base_blk, n_seq, t_len, prm, s0):
    has_init = s0 is not None
    n_sub = SEQS_PER_STEP
    assert n_seq % n_sub == 0 and base_blk % n_sub == 0
    rows = n_sub * t_len

    def pspec(col_blk):
        return pl.BlockSpec((rows, PAIRS_W), lambda b: (base_blk // n_sub + b, col_blk),
                            pipeline_mode=_seq_buffering(rows))

    full = lambda shape: pl.BlockSpec(shape, lambda b: (0,) * len(shape))
    in_specs = [pspec(0), pspec(1), pspec(2), pspec(3),
                full((2, RW_COLS)), full((16, PAIRS_W)), full((2, LANES, PAIRS_W)), full((2, LANES, PAIRS_W)),
                full((LANES, PAIRS_W))]
    args = [p, p, p, p, prm["mu"], prm["vec"], prm["w2"], prm["a2"], prm["g2"]]
    if has_init:
        in_specs.append(pl.BlockSpec((n_sub, 3, QUAD, QUAD), lambda b: (b, 0, 0, 0)))
        args.append(s0)
    y, sfin = pl.pallas_call(
        functools.partial(_rwkv_kernel, t_len=t_len, n_sub=n_sub, has_init=has_init),
        grid=(n_seq // n_sub,),
        in_specs=in_specs,
        out_specs=[pl.BlockSpec((rows, PAIRS_W), lambda b: (b, 0)),
                   pl.BlockSpec((n_sub, 2, 2 * 3, HEAD, HEAD), lambda b: (b, 0, 0, 0, 0))],
        out_shape=[jax.ShapeDtypeStruct((n_seq * t_len, RW_DIM), F32),
                   jax.ShapeDtypeStruct((n_seq, 2, 2 * 3, HEAD, HEAD), F32)],
        scratch_shapes=[pltpu.VMEM((4, rows, PAIRS_W), F32),
                        pltpu.VMEM((2, rows, PAIRS_W), F32),
                        pltpu.VMEM((n_sub, 3, QUAD, QUAD), F32)],
        compiler_params=pltpu.CompilerParams(dimension_semantics=("arbitrary",), vmem_limit_bytes=VMEM_LIMIT),
        name="rwkv7_mix",
    )(*args)
    return y, sfin


def _gdn_kernel(*refs, t_len, n_sub, has_init):
    if has_init:
        (q_ref, k_ref, v_ref, z_ref, gt_ref, cw_ref, vec_ref, gvec_ref, s0_ref,
         y_ref, sfin_ref, xs_ref, yd_ref, st_ref) = refs
    else:
        (q_ref, k_ref, v_ref, z_ref, gt_ref, cw_ref, vec_ref, gvec_ref,
         y_ref, sfin_ref, xs_ref, yd_ref, st_ref) = refs
    n_chunks = t_len // CHUNK
    vec = vec_ref[...]
    gvec = gvec_ref[...]
    bm = _block_ones(LANES, HEAD).astype(BF16)

    def conv_body(c, carry):
        start = pl.multiple_of(c * CHUNK, CHUNK)
        for sub in range(n_sub):
            for idx, ref in enumerate((q_ref, k_ref, v_ref)):
                pc, prev, nxt = _shifted(ref, sub * t_len, start, c, n_chunks, t_len)
                cw = cw_ref[:, idx * PAIRS_W:(idx + 1) * PAIRS_W]
                xs_ref[idx, pl.ds(sub * t_len + start, CHUNK), :] = _silu(
                    cw[0:1] * prev + cw[1:2] * pc + cw[2:3] * nxt)
        return carry

    lax.fori_loop(0, n_chunks, conv_body, 0)

    if has_init:
        st_ref[...] = s0_ref[...]
    else:
        st_ref[...] = jnp.zeros_like(st_ref)

    def lane_bcast(tile, lane, width):
        return jnp.broadcast_to(tile[:, lane:lane + 1], (tile.shape[0], width))

    def per_head(tile, first_lane):
        return jnp.concatenate([lane_bcast(tile, first_lane + h, HEAD) for h in range(DN_HEADS)], axis=1)

    def seq_step(sub, i):
        off = sub * t_len
        starts = (pl.multiple_of(off + i * CHUNK, CHUNK), pl.multiple_of(off + (n_chunks - 1 - i) * CHUNK, CHUNK))
        qs = [xs_ref[0, pl.ds(starts[d], CHUNK), :] for d in range(2)]
        ks = [xs_ref[1, pl.ds(starts[d], CHUNK), :] for d in range(2)]
        norms = _head_sums([qs[0] * qs[0], ks[0] * ks[0], qs[1] * qs[1], ks[1] * ks[1]], bm)
        fac, gam_tiles = [], []
        for d in range(2):
            gt = gt_ref[pl.ds(starts[d], CHUNK), :]
            q = qs[d] * lax.rsqrt(norms[2 * d] + L2_EPS) * (HEAD ** -0.5)
            k = ks[d] * lax.rsqrt(norms[2 * d + 1] + L2_EPS)
            v = xs_ref[2, pl.ds(starts[d], CHUNK), :]
            g_tile = -jnp.exp(gvec[0:1]) * _softplus(gt + gvec[1:2])
            gam_tile = _cumsum_rows(g_tile, d)
            gam = per_head(gam_tile, 6 * d)
            beta = per_head(_sigmoid(gt), 12 + 6 * d)
            last = gam[CHUNK - 1:CHUNK] if d == 0 else gam[0:1]
            eg = jnp.exp(gam)
            kb = k * beta
            fac.append(dict(q=q, k=k, kb=kb, vb=v * beta, kbe=kb * eg, qe=q * eg, ko=k * jnp.exp(last - gam), gam=gam,
                            el=jnp.broadcast_to(jnp.exp(last), (CHUNK, PAIRS_W))))
            gam_tiles.append(gam_tile)
        gam_rows = [jnp.transpose(gam_tiles[d]) for d in range(2)]
        incl = _quad_masks()[0]
        pairs = []
        for j in range(3):
            ln = slice(j * LANES, (j + 1) * LANES)
            gam_c = jnp.concatenate([fac[0]["gam"][:, ln], fac[1]["gam"][:, ln]], axis=1)
            gam_r = jnp.concatenate([gam_rows[d][6 * d + 2 * j + hh:6 * d + 2 * j + hh + 1, :]
                                     for d in range(2) for hh in range(2)], axis=1)
            pairs.append(jnp.exp(jnp.where(incl, gam_c - gam_r, -jnp.inf)))
        return starts, fac, pairs

    def seq_body(i, carry):
        subs = [seq_step(sub, i) for sub in range(n_sub)]
        chains = [(sub, j) for sub in range(n_sub) for j in range(3)]

        def quad(c, name):
            fac, ln = subs[c[0]][1], slice(c[1] * LANES, (c[1] + 1) * LANES)
            return jnp.concatenate([fac[0][name][:, ln], fac[1][name][:, ln]], axis=1)

        incl, strict, t_idx, s_idx = _quad_masks()
        same_head = _same_head()
        pair = [subs[c[0]][2][c[1]] for c in chains]
        g = [_mm_nt(jnp.concatenate([quad(c, "kb"), quad(c, "q")], axis=0), _qstack(quad(c, "k")))
             for c in chains]
        a2 = [jnp.where(strict, x[:CHUNK] * p, 0.0) for x, p in zip(g, pair)]
        qk = [x[CHUNK:] * p for x, p in zip(g, pair)]
        t2 = _inv_unit_tri_many(a2, t_idx, s_idx)
        uw = [_mm(t, jnp.concatenate([_qstack(quad(c, "vb")), _qstack(quad(c, "kbe"))], axis=1))
              for t, c in zip(t2, chains)]
        st = [st_ref[c[0], c[1]] for c in chains]
        wq = [_mm_nt(jnp.concatenate([x[:, QUAD:], quad(c, "qe")], axis=0), s)
              for x, c, s in zip(uw, chains, st)]
        vn = [x[:, :QUAD] - y[:CHUNK] for x, y in zip(uw, wq)]
        qv = [_mm(m, _qstack(x)) for m, x in zip(qk, vn)]
        upd = [_mm_tn(x, quad(c, "ko")) for x, c in zip(vn, chains)]
        os_ = [y[CHUNK:] + m for y, m in zip(wq, qv)]
        for c, s, x in zip(chains, st, upd):
            st_ref[c[0], c[1]] = s * quad(c, "el")[0:1] + jnp.where(same_head, x, 0.0)
        for sub in range(n_sub):
            for d in range(2):
                yd_ref[d, pl.ds(subs[sub][0][d], CHUNK), :] = jnp.concatenate(
                    [os_[sub * 3 + j][:, d * LANES:(d + 1) * LANES] for j in range(3)], axis=1)
        return carry

    lax.fori_loop(0, n_chunks, seq_body, 0)
    _emit_pair_states(st_ref, sfin_ref, n_sub)

    def post_body(c, carry):
        start = pl.multiple_of(c * CHUNK, CHUNK)
        o = yd_ref[0, pl.ds(start, CHUNK), :] + yd_ref[1, pl.ds(start, CHUNK), :]
        z = z_ref[pl.ds(start, CHUNK), :]
        ms = _head_sums([o * o], bm)[0] * (1.0 / HEAD)
        y_ref[pl.ds(start, CHUNK), :] = o * lax.rsqrt(ms + RMS_EPS) * vec[0:1] * _silu(z)
        return carry

    lax.fori_loop(0, n_sub * n_chunks, post_body, 0)


def _gdn_call(p, base_blk, n_seq, t_len, prm, s0):
    has_init = s0 is not None
    dn_blk = DN_OFF // PAIRS_W
    n_sub = SEQS_PER_STEP
    assert n_seq % n_sub == 0 and base_blk % n_sub == 0
    rows = n_sub * t_len

    def pspec(col_blk):
        return pl.BlockSpec((rows, PAIRS_W), lambda b: (base_blk // n_sub + b, dn_blk + col_blk),
                            pipeline_mode=_seq_buffering(rows))

    full = lambda shape: pl.BlockSpec(shape, lambda b: (0,) * len(shape))
    in_specs = [pspec(0), pspec(1), pspec(2), pspec(3),
                pl.BlockSpec((rows, LANES), lambda b: (base_blk // n_sub + b, DN_GATE_OFF // LANES)),
                full((8, 3 * PAIRS_W)), full((8, PAIRS_W)), full((8, LANES))]
    args = [p, p, p, p, p, prm["cw"], prm["vec"], prm["gvec"]]
    if has_init:
        in_specs.append(pl.BlockSpec((n_sub, 3, QUAD, QUAD), lambda b: (b, 0, 0, 0)))
        args.append(s0)
    y, sfin = pl.pallas_call(
        functools.partial(_gdn_kernel, t_len=t_len, n_sub=n_sub, has_init=has_init),
        grid=(n_seq // n_sub,),
        in_specs=in_specs,
        out_specs=[pl.BlockSpec((rows, PAIRS_W), lambda b: (b, 0)),
                   pl.BlockSpec((n_sub, 2, 2 * 3, HEAD, HEAD), lambda b: (b, 0, 0, 0, 0))],
        out_shape=[jax.ShapeDtypeStruct((n_seq * t_len, DN_DIM), F32),
                   jax.ShapeDtypeStruct((n_seq, 2, 2 * 3, HEAD, HEAD), F32)],
        scratch_shapes=[pltpu.VMEM((3, rows, PAIRS_W), F32),
                        pltpu.VMEM((2, rows, PAIRS_W), F32),
                        pltpu.VMEM((n_sub, 3, QUAD, QUAD), F32)],
        compiler_params=pltpu.CompilerParams(dimension_semantics=("arbitrary",), vmem_limit_bytes=VMEM_LIMIT),
        name="gdn_mix",
    )(*args)
    return y, sfin


def _gla_kernel(*refs, t_len, n_sub, has_init):
    if has_init:
        (q_ref, k_ref, v_ref, og_ref, gt_ref, gk2_ref, vec_ref, s0_ref,
         y_ref, sfin_ref, yd_ref, st_ref) = refs
    else:
        (q_ref, k_ref, v_ref, og_ref, gt_ref, gk2_ref, vec_ref,
         y_ref, sfin_ref, yd_ref, st_ref) = refs
    n_chunks = t_len // CHUNK
    vec = vec_ref[...]
    sr = _iota((GLA_DIM, GLA_KDIM), 0)
    sc = _iota((GLA_DIM, GLA_KDIM), 1)
    st_mask = _shr(sr, GLA_DV) == _shr(sc, GLA_DK)

    if has_init:
        st_ref[...] = s0_ref[...]
    else:
        st_ref[...] = jnp.zeros_like(st_ref)

    def seq_body(i, carry):
        chains = [(sub, d) for sub in range(n_sub) for d in range(2)]
        starts = [pl.multiple_of(sub * t_len + (i if d == 0 else n_chunks - 1 - i) * CHUNK, CHUNK)
                  for sub, d in chains]
        t_idx = _iota((CHUNK, GLA_DIM), 0)
        s_idx = _iota((CHUNK, GLA_DIM), 1) & (CHUNK - 1)
        k_lane = _shr(_iota((CHUNK, GLA_KDIM), 1), GLA_DK)
        q = [q_ref[pl.ds(s, CHUNK), :] * (GLA_DK ** -0.5) for s in starts]
        k = [k_ref[pl.ds(s, CHUNK), :] for s in starts]
        v = [v_ref[pl.ds(s, CHUNK), :] for s in starts]
        gk = [_log_sigmoid(_mm(gt_ref[pl.ds(s, CHUNK), :], gk2_ref[c[1]]) + vec[c[1]:c[1] + 1, :GLA_KDIM])
              * (1.0 / GLA_GATE_NORM) for s, c in zip(starts, chains)]
        b = [_cumsum_rows(x, c[1]) for x, c in zip(gk, chains)]
        tot = [jnp.sum(x, axis=0, keepdims=True) for x in gk]
        kt = [kk * jnp.exp(0.5 * tt - bb) for kk, tt, bb in zip(k, tot, b)]
        k_stack = [jnp.concatenate([jnp.where(k_lane == h, x, 0.0) for h in range(GLA_HEADS)], axis=0) for x in kt]
        scores = [_mm_nt(qq * jnp.exp(bb - 0.5 * tt), ks) for qq, bb, tt, ks in zip(q, b, tot, k_stack)]
        scores = [jnp.where((s_idx <= t_idx) if c[1] == 0 else (s_idx >= t_idx), x, 0.0)
                  for x, c in zip(scores, chains)]
        st = [st_ref[c[0], c[1]] for c in chains]
        o_intra = [_mm(x, _qstack(vv)) for x, vv in zip(scores, v)]
        o_inter = [_mm_nt(qq * jnp.exp(bb), s) for qq, bb, s in zip(q, b, st)]
        upd = [_mm_tn(vv, kk * jnp.exp(tt - bb)) for vv, kk, tt, bb in zip(v, k, tot, b)]
        for idx, c in enumerate(chains):
            st_ref[c[0], c[1]] = st[idx] * jnp.exp(tot[idx]) + jnp.where(st_mask, upd[idx], 0.0)
            yd_ref[c[1], pl.ds(starts[idx], CHUNK), :] = o_intra[idx] + o_inter[idx]
        return carry

    lax.fori_loop(0, n_chunks, seq_body, 0)
    for sub in range(n_sub):
        for d in range(2):
            for h in range(GLA_HEADS):
                sfin_ref[sub, d, h] = st_ref[sub, d, h * GLA_DV:(h + 1) * GLA_DV, h * GLA_DK:(h + 1) * GLA_DK]

    bm = _block_ones(LANES, GLA_DV).astype(BF16)

    def post_body(c, carry):
        start = pl.multiple_of(c * CHUNK, CHUNK)
        o = yd_ref[0, pl.ds(start, CHUNK), :] + yd_ref[1, pl.ds(start, CHUNK), :]
        og = og_ref[pl.ds(start, CHUNK), :]
        ms = _head_sums([o * o], bm)[0] * (1.0 / GLA_DV)
        y_ref[pl.ds(start, CHUNK), :] = o * lax.rsqrt(ms + RMS_EPS) * vec[2:3] * _silu(og)
        return carry

    lax.fori_loop(0, n_sub * n_chunks, post_body, 0)


def _gla_call(p, base_blk, n_seq, t_len, prm, s0):
    has_init = s0 is not None
    t128 = GLA_OFF // LANES
    t256 = GLA_OFF // GLA_DIM
    n_sub = SEQS_PER_STEP
    assert n_seq % n_sub == 0 and base_blk % n_sub == 0
    rows = n_sub * t_len
    blk = lambda b: base_blk // n_sub + b
    in_specs = [pl.BlockSpec((rows, LANES), lambda b: (blk(b), t128)),
                pl.BlockSpec((rows, LANES), lambda b: (blk(b), t128 + 1)),
                pl.BlockSpec((rows, GLA_DIM), lambda b: (blk(b), t256 + 1)),
                pl.BlockSpec((rows, GLA_DIM), lambda b: (blk(b), t256 + 2)),
                pl.BlockSpec((rows, LANES), lambda b: (blk(b), t128 + 6)),
                pl.BlockSpec((2, LANES, LANES), lambda b: (0, 0, 0)),
                pl.BlockSpec((8, GLA_DIM), lambda b: (0, 0))]
    args = [p, p, p, p, p, prm["gk2"], prm["vec"]]
    if has_init:
        in_specs.append(pl.BlockSpec((n_sub, 2, GLA_DIM, GLA_KDIM), lambda b: (b, 0, 0, 0)))
        args.append(s0)
    y, sfin = pl.pallas_call(
        functools.partial(_gla_kernel, t_len=t_len, n_sub=n_sub, has_init=has_init),
        grid=(n_seq // n_sub,),
        in_specs=in_specs,
        out_specs=[pl.BlockSpec((rows, GLA_DIM), lambda b: (b, 0)),
                   pl.BlockSpec((n_sub, 2, GLA_HEADS, GLA_DV, GLA_DK), lambda b: (b, 0, 0, 0, 0))],
        out_shape=[jax.ShapeDtypeStruct((n_seq * t_len, GLA_DIM), F32),
                   jax.ShapeDtypeStruct((n_seq, 2, GLA_HEADS, GLA_DV, GLA_DK), F32)],
        scratch_shapes=[pltpu.VMEM((2, rows, GLA_DIM), F32),
                        pltpu.VMEM((n_sub, 2, GLA_DIM, GLA_KDIM), F32)],
        compiler_params=pltpu.CompilerParams(dimension_semantics=("arbitrary",), vmem_limit_bytes=VMEM_LIMIT),
        name="gla_mix",
    )(*args)
    return y, sfin


def _out_kernel(yr_ref, yg_ref, yd_ref, x_ref, g1_ref, sh_ref, sc_ref, nw_ref, wo_ref, wr_ref, br_ref,
                x1_ref, h2_ref, grp_ref):
    mix = (_mm(yr_ref[...], wo_ref[0:RW_DIM, :]) + _mm(yg_ref[...], wo_ref[RW_DIM:RW_DIM + GLA_DIM, :])
           + _mm(yd_ref[...], wo_ref[RW_DIM + GLA_DIM:, :]))
    x1 = x_ref[...] + g1_ref[0] * mix
    x1_ref[...] = x1
    h2 = x1 * lax.rsqrt(jnp.mean(x1 * x1, axis=-1, keepdims=True) + RMS_EPS) * nw_ref[...]
    h2 = h2 * (1.0 + sc_ref[0]) + sh_ref[0]
    for s in range(ROW_SLABS):
        h2_ref[:, s, :] = h2[:, s * LANES:(s + 1) * LANES]
    logits = _mm_split(h2, wr_ref[...]) + br_ref[...]
    lane_i = _iota(logits.shape, 1)
    lg = jnp.where(lane_i < N_GROUPS, logits, -jnp.inf)
    gmax = jnp.max(lg, axis=1, keepdims=True)
    gsel = jnp.min(jnp.where(lg == gmax, lane_i.astype(F32), 1e9), axis=1, keepdims=True)
    grp_ref[...] = jnp.broadcast_to(gsel, logits.shape).astype(jnp.int32)


def _out_call(y_rw, y_gla, y_dn, x, mod3, layer, nw, w_out, wr, br, n_ctx_tok, lat_t, tile=256):
    n = x.shape[0]
    row = lambda i: layer * 8 + _cond_row(i, tile, n_ctx_tok, lat_t)
    def modspec(part):
        return pl.BlockSpec((1, 1, D_MODEL), lambda i: (row(i), 0, part))

    return pl.pallas_call(
        _out_kernel,
        grid=(n // tile,),
        in_specs=[pl.BlockSpec((tile, RW_DIM), lambda i: (i, 0)),
                  pl.BlockSpec((tile, GLA_DIM), lambda i: (i, 0)),
                  pl.BlockSpec((tile, DN_DIM), lambda i: (i, 0)),
                  pl.BlockSpec((tile, D_MODEL), lambda i: (i, 0)),
                  modspec(2), modspec(3), modspec(4),
                  pl.BlockSpec((1, D_MODEL), lambda i: (0, 0)),
                  pl.BlockSpec((D_MODEL, D_MODEL), lambda i: (0, 0)),
                  pl.BlockSpec((D_MODEL, LANES), lambda i: (0, 0)),
                  pl.BlockSpec((1, LANES), lambda i: (0, 0))],
        out_specs=[pl.BlockSpec((tile, D_MODEL), lambda i: (i, 0)),
                   pl.BlockSpec((tile, ROW_SLABS, LANES), lambda i: (i, 0, 0)),
                   pl.BlockSpec((tile, LANES), lambda i: (i, 0))],
        out_shape=[jax.ShapeDtypeStruct((n, D_MODEL), F32),
                   jax.ShapeDtypeStruct((n, ROW_SLABS, LANES), F32),
                   jax.ShapeDtypeStruct((n, LANES), jnp.int32)],
        compiler_params=pltpu.CompilerParams(dimension_semantics=("arbitrary",), vmem_limit_bytes=VMEM_LIMIT),
        name="out_proj_router",
    )(y_rw, y_gla, y_dn, x, mod3, mod3, mod3, nw.reshape(1, D_MODEL), w_out, wr, br)


ROW_SLABS = D_MODEL // LANES
MOE_TILE = 512
DMA_TILE = 256


def _route_slots(grp, n_tok):
    gid = grp[:, 0]
    onehot = (gid[:, None] == jnp.arange(N_GROUPS, dtype=jnp.int32)[None, :]).astype(jnp.int32)
    csum = jnp.cumsum(onehot, axis=0)
    rank = jnp.sum(csum * onehot, axis=1) - 1
    counts = csum[-1]
    tiles_per = (counts + MOE_TILE - 1) // MOE_TILE
    tile_end = jnp.cumsum(tiles_per)
    offs = (tile_end - tiles_per) * MOE_TILE
    slot = jnp.sum(offs[None, :] * onehot, axis=1) + rank
    max_tiles = n_tok // MOE_TILE + N_GROUPS
    tile_group = jnp.sum((jnp.arange(max_tiles, dtype=jnp.int32)[:, None] >= tile_end[None, :]).astype(jnp.int32), axis=1)
    tile_group = jnp.minimum(tile_group, N_GROUPS - 1)
    return slot.astype(jnp.int32), tile_group.astype(jnp.int32), tile_end[-1:].astype(jnp.int32), max_tiles


def _row_maps(slot, n_tok, n_rows):
    tok = jnp.full((n_rows,), -1, jnp.int32).at[slot].set(jnp.arange(n_tok, dtype=jnp.int32))
    pad = tok < 0
    spare = n_tok + jnp.cumsum(pad.astype(jnp.int32)) - 1
    return jnp.where(pad, 0, tok), jnp.where(pad, spare, tok)


def _moe_group_kernel(tg_ref, nt_ref, src_ref, dst_ref, h_ref, init_ref, wr_ref, br_ref, wg_ref, wu_ref, wd_ref,
                      y_ref, xbuf_ref, obuf_ref, acc_ref, gate_ref, xb_ref, gsem, ssem):
    del init_ref
    i = pl.program_id(0)
    e = pl.program_id(1)
    n_tiles = nt_ref[0]
    buf = lax.rem(i, 2)

    def start_gather(tile, b):
        def body(r, carry):
            pltpu.make_async_copy(h_ref.at[src_ref[tile * MOE_TILE + r]], xbuf_ref.at[b, r], gsem.at[b]).start()
            return carry
        lax.fori_loop(0, MOE_TILE, body, 0, unroll=8)

    def wait_gather(b):
        pltpu.make_async_copy(h_ref.at[pl.ds(0, MOE_TILE)], xbuf_ref.at[b], gsem.at[b]).wait()

    def start_scatter(tile, b):
        def body(r, carry):
            pltpu.make_async_copy(obuf_ref.at[b, r], y_ref.at[dst_ref[tile * MOE_TILE + r]], ssem.at[b]).start()
            return carry
        lax.fori_loop(0, MOE_TILE, body, 0, unroll=8)

    def wait_scatter(b):
        pltpu.make_async_copy(obuf_ref.at[b], y_ref.at[pl.ds(0, MOE_TILE)], ssem.at[b]).wait()

    @pl.when(i < n_tiles)
    def _():
        @pl.when(e == 0)
        def _():
            @pl.when(i == 0)
            def _():
                start_gather(0, 0)

            wait_gather(buf)

            @pl.when(i + 1 < n_tiles)
            def _():
                start_gather(i + 1, 1 - buf)

            acc_ref[...] = jnp.zeros_like(acc_ref)
            x = jnp.concatenate([xbuf_ref[buf, :, s, :] for s in range(ROW_SLABS)], axis=1)
            xb_ref[...] = x.astype(BF16)
            logits = _mm_split(x, wr_ref[...]) + br_ref[...]
            lane_i = _iota(logits.shape, 1)
            lane = lane_i.astype(F32)
            grp = tg_ref[i]
            lg = jnp.where(lane_i < N_GROUPS, logits, -jnp.inf)
            gmax = jnp.max(lg, axis=1, keepdims=True)
            l_sel = jnp.sum(jnp.where(lane_i == grp, logits, 0.0), axis=1, keepdims=True)
            p_grp = jnp.exp(l_sel - gmax) / jnp.sum(jnp.exp(lg - gmax), axis=1, keepdims=True)
            first = N_GROUPS + grp * EXPERTS_PER_GROUP
            le = jnp.where((lane_i >= first) & (lane_i < first + EXPERTS_PER_GROUP), logits, -jnp.inf)
            v1 = jnp.max(le, axis=1, keepdims=True)
            i1 = jnp.min(jnp.where(le == v1, lane, 1e9), axis=1, keepdims=True)
            le2 = jnp.where(lane == i1, -jnp.inf, le)
            v2 = jnp.max(le2, axis=1, keepdims=True)
            i2 = jnp.min(jnp.where(le2 == v2, lane, 1e9), axis=1, keepdims=True)
            e2 = jnp.exp(v2 - v1)
            p1 = p_grp / (1.0 + e2)
            p2 = p_grp * e2 / (1.0 + e2)
            gate_ref[...] = jnp.where(lane == i1, p1, jnp.where(lane == i2, p2, 0.0))

        h = xb_ref[...]
        gates = gate_ref[...]
        lane_g = _iota(gates.shape, 1)
        gcol = jnp.sum(jnp.where(lane_g == N_GROUPS + tg_ref[i] * EXPERTS_PER_GROUP + e, gates, 0.0),
                       axis=1, keepdims=True)
        hid = _silu(jnp.dot(h, wg_ref[0].astype(BF16), preferred_element_type=F32))
        hid = hid * jnp.dot(h, wu_ref[0].astype(BF16), preferred_element_type=F32) * gcol
        acc_ref[...] += _mm(hid, wd_ref[0])

        @pl.when(e == EXPERTS_PER_GROUP - 1)
        def _():
            @pl.when(i >= 2)
            def _():
                wait_scatter(buf)

            acc = acc_ref[...]
            for s in range(ROW_SLABS):
                obuf_ref[buf, :, s, :] = acc[:, s * LANES:(s + 1) * LANES]
            start_scatter(i, buf)

            @pl.when(i == n_tiles - 1)
            def _():
                wait_scatter(buf)

                @pl.when(i >= 1)
                def _():
                    wait_scatter(1 - buf)


def _moe_group_call(tile_group, n_tiles, max_tiles, src_row, dst_row, h_rows, layer, wr, br, w_gate, w_up, w_down):
    n_rows = max_tiles * MOE_TILE
    tile_idx = lambda i, nt: jnp.minimum(i, nt[0] - 1)
    expert = lambda i, e, tg, nt: (layer * N_EXPERTS + tg[tile_idx(i, nt)] * EXPERTS_PER_GROUP
                                   + jnp.where(i < nt[0], e, EXPERTS_PER_GROUP - 1))
    wspec = lambda shape: pl.BlockSpec(shape, lambda i, e, tg, nt, sr, ds: (expert(i, e, tg, nt), 0, 0))
    init = jnp.zeros((n_rows, ROW_SLABS, LANES), F32)
    return pl.pallas_call(
        _moe_group_kernel,
        grid_spec=pltpu.PrefetchScalarGridSpec(
            num_scalar_prefetch=4,
            grid=(max_tiles, EXPERTS_PER_GROUP),
            in_specs=[pl.BlockSpec(memory_space=pl.ANY), pl.BlockSpec(memory_space=pl.ANY),
                      pl.BlockSpec((D_MODEL, LANES), lambda i, e, tg, nt, sr, ds: (0, 0)),
                      pl.BlockSpec((1, LANES), lambda i, e, tg, nt, sr, ds: (0, 0)),
                      wspec((1, D_MODEL, D_EXPERT)), wspec((1, D_MODEL, D_EXPERT)), wspec((1, D_EXPERT, D_MODEL))],
            out_specs=pl.BlockSpec(memory_space=pl.ANY),
            scratch_shapes=[pltpu.VMEM((2, MOE_TILE, ROW_SLABS, LANES), F32),
                            pltpu.VMEM((2, MOE_TILE, ROW_SLABS, LANES), F32),
                            pltpu.VMEM((MOE_TILE, D_MODEL), F32),
                            pltpu.VMEM((MOE_TILE, LANES), F32),
                            pltpu.VMEM((MOE_TILE, D_MODEL), BF16),
                            pltpu.SemaphoreType.DMA((2,)),
                            pltpu.SemaphoreType.DMA((2,))]),
        out_shape=jax.ShapeDtypeStruct((n_rows, ROW_SLABS, LANES), F32),
        input_output_aliases={5: 0},
        compiler_params=pltpu.CompilerParams(dimension_semantics=("arbitrary", "arbitrary"),
                                             vmem_limit_bytes=VMEM_LIMIT),
        name="moe_grouped",
    )(tile_group, n_tiles, src_row, dst_row, h_rows, init, wr, br, w_gate, w_up, w_down)


def _combine_kernel(y_ref, x_ref, g2_ref, o_ref):
    for s in range(ROW_SLABS):
        ln = slice(s * LANES, (s + 1) * LANES)
        o_ref[:, ln] = x_ref[:, ln] + g2_ref[0][:, ln] * y_ref[:, s, :]


def _combine_call(y_rows, x1, mod3, layer, n_ctx_tok, lat_t):
    n = x1.shape[0]
    row = lambda i: layer * 8 + _cond_row(i, DMA_TILE, n_ctx_tok, lat_t)
    return pl.pallas_call(
        _combine_kernel,
        grid=(n // DMA_TILE,),
        in_specs=[pl.BlockSpec((DMA_TILE, ROW_SLABS, LANES), lambda i: (i, 0, 0)),
                  pl.BlockSpec((DMA_TILE, D_MODEL), lambda i: (i, 0)),
                  pl.BlockSpec((1, 1, D_MODEL), lambda i: (row(i), 0, 5))],
        out_specs=pl.BlockSpec((DMA_TILE, D_MODEL), lambda i: (i, 0)),
        out_shape=jax.ShapeDtypeStruct((n, D_MODEL), F32),
        compiler_params=pltpu.CompilerParams(dimension_semantics=("arbitrary",), vmem_limit_bytes=VMEM_LIMIT),
        name="moe_combine",
    )(y_rows, x1, mod3)


def _norm_kernel(x_ref, w_ref, o_ref):
    x = x_ref[...]
    o_ref[...] = x * lax.rsqrt(jnp.mean(x * x, axis=-1, keepdims=True) + RMS_EPS) * w_ref[...]


def _norm_call(x, w, base_tile, n_tiles, tile=256):
    return pl.pallas_call(
        _norm_kernel,
        grid=(n_tiles,),
        in_specs=[pl.BlockSpec((tile, D_MODEL), lambda i: (base_tile + i, 0)),
                  pl.BlockSpec((1, D_MODEL), lambda i: (0, 0))],
        out_specs=pl.BlockSpec((tile, D_MODEL), lambda i: (i, 0)),
        out_shape=jax.ShapeDtypeStruct((n_tiles * tile, D_MODEL), F32),
        compiler_params=pltpu.CompilerParams(dimension_semantics=("arbitrary",), vmem_limit_bytes=VMEM_LIMIT),
        name="final_norm",
    )(x, w.reshape(1, D_MODEL))


def _pos_embed(n_tokens, dim):
    rows = n_tokens // GRID_W
    row = jnp.broadcast_to(jnp.arange(rows)[:, None], (rows, GRID_W)).reshape(-1)
    col = jnp.broadcast_to(jnp.arange(GRID_W)[None, :], (rows, GRID_W)).reshape(-1)
    quarter = dim // 4
    omega = 1.0 / (POS_BASE ** (jnp.arange(quarter, dtype=F32) / quarter))

    def axis_embed(pos):
        ang = pos.astype(F32)[:, None] * omega[None, :]
        return jnp.concatenate([jnp.sin(ang), jnp.cos(ang)], axis=-1)

    return jnp.concatenate([axis_embed(row), axis_embed(col)], axis=-1)


def _relayout_w_in(w_in_l):
    rw = w_in_l[:, :RW_COLS]
    g0 = RW_COLS
    q, k, v = w_in_l[:, g0:g0 + 128], w_in_l[:, g0 + 128:g0 + 256], w_in_l[:, g0 + 256:g0 + 512]
    gates = w_in_l[:, g0 + 512:g0 + 544]
    og = w_in_l[:, g0 + 544:g0 + 800]
    d0 = g0 + 800
    qkvz = w_in_l[:, d0:d0 + 1536]
    dgates = w_in_l[:, d0 + 1536:d0 + 1560]
    z = lambda n: jnp.zeros((D_MODEL, n), w_in_l.dtype)
    return jnp.concatenate([rw, qkvz, q, k, v, og, gates, z(96), dgates, z(104)], axis=1).astype(BF16)


def _rwkv_params(mu, w0, w2, a0, a2, g2, kk, ka, rk, ln_w, ln_b):
    rows = [w0[0], w0[1], a0[0], a0[1], kk, ka, rk, ln_w, ln_b]
    vec = jnp.stack(rows + [jnp.zeros_like(kk)] * 7)
    zero = jnp.zeros((64, RW_DIM), F32)
    w2c = jnp.stack([jnp.concatenate([w2[0], zero]), jnp.concatenate([zero, w2[1]])])
    a2c = jnp.stack([jnp.concatenate([a2[0], zero]), jnp.concatenate([zero, a2[1]])])
    return {"mu": mu, "vec": vec, "w2": w2c, "a2": a2c, "g2": g2}


def _gdn_params(conv, a_log, dt_bias, norm_w):
    cw = jnp.concatenate([conv, jnp.zeros((5, 3 * DN_DIM), F32)], axis=0)
    vec = jnp.zeros((8, DN_DIM), F32).at[0].set(jnp.tile(norm_w, DN_HEADS))
    gvec = jnp.zeros((8, LANES), F32)
    gvec = gvec.at[0, 0:12].set(a_log.reshape(12)).at[1, 0:12].set(dt_bias.reshape(12))
    return {"cw": cw, "vec": vec, "gvec": gvec}


def _gla_params(gk2, gk_b, norm_w):
    g = jnp.zeros((2, LANES, LANES), F32)
    g = g.at[0, 0:16].set(gk2[0]).at[1, 16:32].set(gk2[1])
    vec = jnp.zeros((8, GLA_DIM), F32)
    vec = vec.at[0, :GLA_KDIM].set(gk_b[0]).at[1, :GLA_KDIM].set(gk_b[1]).at[2].set(jnp.tile(norm_w, GLA_HEADS))
    return {"gk2": g, "vec": vec}


def _pair_state_in(s):
    b = s.shape[0]
    st = jnp.swapaxes(s, -1, -2).reshape(b, 2, 3, 2, HEAD, HEAD)
    st = jnp.transpose(st, (0, 2, 1, 3, 4, 5)).reshape(b, 3, 4, HEAD, HEAD)
    bd = jnp.einsum("bjgvk,gh->bjgvhk", st, jnp.eye(4, dtype=s.dtype))
    return bd.reshape(b, 3, QUAD, QUAD)


def _pair_state_out(st):
    return jnp.swapaxes(st, -1, -2)


def _gla_state_in(s):
    b = s.shape[0]
    st = jnp.swapaxes(s, -1, -2)
    eye = jnp.eye(GLA_HEADS, dtype=s.dtype)
    bd = jnp.einsum("bdhvk,hg->bdhvgk", st, eye)
    return bd.reshape(b, 2, GLA_DIM, GLA_KDIM)


def _gla_state_out(st):
    return jnp.swapaxes(st, -1, -2)


def kernel(x_prompt, x_sample, c, state_rwkv, state_gla, state_delta, c_ctx, ada_w, ada_b, norm1_w, norm2_w, final_norm_w, w_in, w_out, rwkv_mu, rwkv_w0, rwkv_w2, rwkv_a0, rwkv_a2, rwkv_g2, rwkv_kk, rwkv_ka, rwkv_rk, rwkv_ln_w, rwkv_ln_b, gla_gk2, gla_gk_b, gla_norm_w, dn_conv, dn_a_log, dn_dt_bias, dn_norm_w, moe_wg, moe_bg, moe_we, moe_be, moe_w_gate, moe_w_up, moe_w_down):
    n_ctx, t_ctx, d = x_prompt.shape
    n_lat, t_lat, _ = x_sample.shape
    depth = ada_w.shape[0]
    n_ctx_tok = n_ctx * t_ctx
    n_lat_tok = n_lat * t_lat
    assert d == D_MODEL and n_ctx_tok % 1024 == 0 and t_lat % 1024 == 0 and t_ctx % 256 == 0 and n_lat <= 7
    assert n_ctx_tok % t_lat == 0

    xs = x_sample + _pos_embed(t_lat, d)[None].astype(x_sample.dtype)
    x = jnp.concatenate([x_prompt.reshape(n_ctx_tok, d), xs.reshape(n_lat_tok, d)], axis=0)

    cond8 = jnp.zeros((8, d), F32).at[0].set(c_ctx).at[1:1 + n_lat].set(c)
    mod = _ada_call(cond8, ada_w, ada_b)
    mod3 = mod.reshape(depth * 8, 1, 6 * d)

    w_gate = moe_w_gate.reshape(depth * N_EXPERTS, d, D_EXPERT)
    w_up = moe_w_up.reshape(depth * N_EXPERTS, d, D_EXPERT)
    w_down = moe_w_down.reshape(depth * N_EXPERTS, D_EXPERT, d)

    lat_base = n_ctx_tok // t_lat
    rw_states, gla_states, dn_states = [], [], []
    for l in range(depth):
        p = _in_call(x, mod3, l, norm1_w[l], _relayout_w_in(w_in[l]), n_ctx_tok, t_lat)
        rw_p = _rwkv_params(rwkv_mu[l], rwkv_w0[l], rwkv_w2[l], rwkv_a0[l], rwkv_a2[l], rwkv_g2[l], rwkv_kk[l],
                            rwkv_ka[l], rwkv_rk[l], rwkv_ln_w[l], rwkv_ln_b[l])
        dn_p = _gdn_params(dn_conv[l], dn_a_log[l], dn_dt_bias[l], dn_norm_w[l])
        gla_p = _gla_params(gla_gk2[l], gla_gk_b[l], gla_norm_w[l])

        y_rw_c, s_rw = _rwkv_call(p, 0, n_ctx, t_ctx, rw_p, None)
        y_rw_l, _ = _rwkv_call(p, lat_base, n_lat, t_lat, rw_p, _pair_state_in(state_rwkv[:, l]))
        y_dn_c, s_dn = _gdn_call(p, 0, n_ctx, t_ctx, dn_p, None)
        y_dn_l, _ = _gdn_call(p, lat_base, n_lat, t_lat, dn_p, _pair_state_in(state_delta[:, l]))
        y_gla_c, s_gla = _gla_call(p, 0, n_ctx, t_ctx, gla_p, None)
        y_gla_l, _ = _gla_call(p, lat_base, n_lat, t_lat, gla_p, _gla_state_in(state_gla[:, l]))
        rw_states.append(_pair_state_out(s_rw))
        dn_states.append(_pair_state_out(s_dn))
        gla_states.append(_gla_state_out(s_gla))

        wr = jnp.zeros((d, LANES), F32)
        wr = wr.at[:, :N_GROUPS].set(moe_wg[l]).at[:, N_GROUPS:N_GROUPS + N_EXPERTS].set(moe_we[l].reshape(d, N_EXPERTS))
        br = jnp.zeros((1, LANES), F32)
        br = br.at[0, :N_GROUPS].set(moe_bg[l]).at[0, N_GROUPS:N_GROUPS + N_EXPERTS].set(moe_be[l].reshape(N_EXPERTS))
        x1, h2_rows, grp = _out_call(jnp.concatenate([y_rw_c, y_rw_l]), jnp.concatenate([y_gla_c, y_gla_l]),
                                     jnp.concatenate([y_dn_c, y_dn_l]), x, mod3, l, norm2_w[l],
                                     w_out[l].astype(BF16), wr, br, n_ctx_tok, t_lat)
        n_tok = n_ctx_tok + n_lat_tok
        slot, tile_group, n_tiles, max_tiles = _route_slots(grp, n_tok)
        src_row, dst_row = _row_maps(slot, n_tok, max_tiles * MOE_TILE)
        y_rows = _moe_group_call(tile_group, n_tiles, max_tiles, src_row, dst_row, h2_rows, l, wr, br,
                                 w_gate, w_up, w_down)
        x = _combine_call(y_rows, x1, mod3, l, n_ctx_tok, t_lat)

    y_prompt = _norm_call(x, final_norm_w, 0, n_ctx_tok // 256).reshape(n_ctx, t_ctx, d)
    y_sample = _norm_call(x, final_norm_w, n_ctx_tok // 256, n_lat_tok // 256).reshape(n_lat, t_lat, d)
    new_rw = jnp.stack(rw_states, axis=1).astype(x_prompt.dtype)
    new_gla = jnp.stack(gla_states, axis=1).astype(x_prompt.dtype)
    new_dn = jnp.stack(dn_states, axis=1).astype(x_prompt.dtype)
    return (y_prompt, y_sample, new_rw, new_gla, new_dn)
```

```python
import functools
import math

import jax
import jax.numpy as jnp
from jax import lax
from jax.experimental import pallas as pl
from jax.experimental.pallas import tpu as pltpu

F32 = jnp.float32
BF16 = jnp.bfloat16

D_MODEL = 1024
DEPTH = 4
GRID_W = 64
POS_BASE = 10000.0
RMS_EPS = 1e-6
L2_EPS = 1e-6

RW_HEADS = 6
RW_DIM = 384
RW_COLS = 1536
RW_LN_EPS = 64e-5
GLA_HEADS = 4
GLA_DK = 32
GLA_DV = 64
GLA_KDIM = 128
GLA_DIM = 256
GLA_GATE_RANK = 16
GLA_GATE_NORM = 16.0
DN_HEADS = 6
DN_DIM = 384
N_GROUPS = 4
EXPERTS_PER_GROUP = 4
N_EXPERTS = 16
D_EXPERT = 512

LANES = 128
CHUNK = 64
HEAD = 64
PAIRS_W = 384
DN_OFF = RW_COLS
GLA_OFF = DN_OFF + 4 * PAIRS_W
GLA_PCOLS = 896
DN_GATE_OFF = GLA_OFF + GLA_PCOLS
P_COLS = DN_GATE_OFF + LANES
VMEM_LIMIT = 56 * 1024 * 1024
SEQS_PER_STEP = 2
MIXER_ROWS_PER_STEP = 1024


def _seqs_per_step(n_seq, t_len):
    n_sub = max(SEQS_PER_STEP, MIXER_ROWS_PER_STEP // t_len)
    while n_seq % n_sub:
        n_sub //= 2
    return n_sub


def _seq_buffering(rows):
    return pl.Buffered(1) if rows * PAIRS_W * 4 > (2 << 20) else None


def _mm(a, b):
    return jnp.dot(a.astype(BF16), b.astype(BF16), preferred_element_type=F32)


def _mm_nt(a, b):
    return lax.dot_general(a.astype(BF16), b.astype(BF16), (((1,), (1,)), ((), ())),
                           preferred_element_type=F32)


def _mm_tn(a, b):
    return lax.dot_general(a.astype(BF16), b.astype(BF16), (((0,), (0,)), ((), ())),
                           preferred_element_type=F32)


def _iota(shape, dim):
    return lax.broadcasted_iota(jnp.int32, shape, dim)


def _sigmoid(x):
    return 1.0 / (1.0 + jnp.exp(-x))


def _silu(x):
    return x * _sigmoid(x)


def _softplus(x):
    return jnp.maximum(x, 0.0) + jnp.log(1.0 + jnp.exp(-jnp.abs(x)))


def _log_sigmoid(x):
    return -_softplus(-x)


QUAD = 4 * HEAD


def _quad_masks():
    t = _iota((CHUNK, QUAD), 0)
    lane = _iota((CHUNK, QUAD), 1)
    s = lane & (CHUNK - 1)
    fwd = lane < 2 * HEAD
    bwd = jnp.logical_not(fwd)
    strict = (fwd & (s < t)) | (bwd & (s > t))
    incl = strict | (s == t)
    return incl, strict, t, s


def _same_head():
    return (_iota((QUAD, QUAD), 0) >> 6) == (_iota((QUAD, QUAD), 1) >> 6)


def _qstack(a):
    a = a.astype(BF16)
    grp = _iota(a.shape, 1) >> 6
    return jnp.concatenate([jnp.where(grp == v, a, jnp.zeros_like(a)) for v in range(4)], axis=0)


def _inv_unit_tri_many(ls, t, s):
    same_blk = lambda width: _shr(t, width) == _shr(s, width)
    mul = lambda a, b: _mm(a, _qstack(b))
    n0 = [jnp.where(same_blk(8), -l, 0.0) for l in ls]
    p1 = [mul(x, x) for x in n0]
    inv = [jnp.where(s == t, 1.0, 0.0) + x for x in n0]
    p1_stacked = [_qstack(x) for x in p1]
    p2 = [_mm(x, xs) for x, xs in zip(p1, p1_stacked)]
    inv = [x + _mm(x, ps) for x, ps in zip(inv, p1_stacked)]
    inv = [x + mul(x, p) for x, p in zip(inv, p2)]
    for width in (8, 16, 32):
        blk = same_blk(2 * width) & jnp.logical_not(same_blk(width))
        half = [mul(x, jnp.where(blk, l, 0.0)) for x, l in zip(inv, ls)]
        inv = [x - mul(h, x) for x, h in zip(inv, half)]
    return inv


def _emit_pair_states(st_ref, sfin_ref, n_sub):
    for sub in range(n_sub):
        for j in range(3):
            for d in range(2):
                for hh in range(2):
                    g = 2 * d + hh
                    sfin_ref[sub, d, 2 * j + hh] = st_ref[sub, j, g * HEAD:(g + 1) * HEAD, g * HEAD:(g + 1) * HEAD]


def _split2(x):
    hi = x.astype(BF16)
    return hi, (x - hi.astype(F32)).astype(BF16)


def _mm_split(x, w):
    x_hi, x_lo = _split2(x)
    w_hi, w_lo = _split2(w)
    rows = x.shape[0]
    top = jnp.dot(jnp.concatenate([x_hi, x_lo], axis=0), w_hi, preferred_element_type=F32)
    return top[:rows] + top[rows:] + jnp.dot(x_hi, w_lo, preferred_element_type=F32)


def _head_sums(xs, bm):
    n_tiles = xs[0].shape[1] // LANES
    tiles = [x[:, j * LANES:(j + 1) * LANES] for x in xs for j in range(n_tiles)]
    parts = [p for tl in tiles for p in _split2(tl)]
    out = jnp.dot(jnp.concatenate(parts, axis=0), bm, preferred_element_type=F32)
    res = []
    for i in range(len(xs)):
        cols = []
        for j in range(n_tiles):
            base = (i * n_tiles + j) * 2 * CHUNK
            cols.append(out[base:base + CHUNK] + out[base + CHUNK:base + 2 * CHUNK])
        res.append(jnp.concatenate(cols, axis=1))
    return res


def _cumsum_rows(x, d):
    row = _iota(x.shape, 0)
    k = 1
    while k < CHUNK:
        if d == 0:
            x = x + jnp.where(row >= k, pltpu.roll(x, k, 0), 0.0)
        else:
            x = x + jnp.where(row < CHUNK - k, pltpu.roll(x, CHUNK - k, 0), 0.0)
        k *= 2
    return x


def _shr(x, width):
    return x >> (width.bit_length() - 1)


def _block_ones(n, width):
    r = _iota((n, n), 0)
    c = _iota((n, n), 1)
    return jnp.where(_shr(r, width) == _shr(c, width), 1.0, 0.0).astype(F32)


def _shifted(ref, off, start, c, n_chunks, t_len):
    pc = ref[pl.ds(off + start, CHUNK), :]
    pb = ref[pl.ds(pl.multiple_of(off + jnp.maximum(start - 8, 0), 8), 8), :]
    nb = ref[pl.ds(pl.multiple_of(off + jnp.minimum(start + CHUNK, t_len - 8), 8), 8), :]
    carry_p = jnp.where(c > 0, pb[7:8, :], 0.0)
    carry_n = jnp.where(c < n_chunks - 1, nb[0:1, :], 0.0)
    row = _iota(pc.shape, 0)
    prev = jnp.where(row == 0, carry_p, pltpu.roll(pc, 1, 0))
    nxt = jnp.where(row == CHUNK - 1, carry_n, pltpu.roll(pc, CHUNK - 1, 0))
    return pc, prev, nxt


def _ada_kernel(c_ref, w_ref, b_ref, o_ref):
    c = c_ref[...]
    o_ref[0] = _mm(_silu(c), w_ref[0]) + b_ref[0]


def _ada_call(cond8, ada_w, ada_b):
    nl = ada_w.shape[0]
    return pl.pallas_call(
        _ada_kernel,
        grid=(nl, 6),
        in_specs=[pl.BlockSpec((8, D_MODEL), lambda l, j: (0, 0)),
                  pl.BlockSpec((1, D_MODEL, D_MODEL), lambda l, j: (l, 0, j)),
                  pl.BlockSpec((1, 1, D_MODEL), lambda l, j: (l, 0, j))],
        out_specs=pl.BlockSpec((1, 8, D_MODEL), lambda l, j: (l, 0, j)),
        out_shape=jax.ShapeDtypeStruct((nl, 8, 6 * D_MODEL), F32),
        compiler_params=pltpu.CompilerParams(dimension_semantics=("arbitrary", "arbitrary"),
                                             vmem_limit_bytes=VMEM_LIMIT),
        name="ada_mod",
    )(cond8, ada_w, ada_b.reshape(nl, 1, 6 * D_MODEL))


def _cond_row(i, tile, n_ctx_tok, lat_t):
    tok = i * tile
    return jnp.where(tok < n_ctx_tok, 0, 1 + (tok - n_ctx_tok) // lat_t)


def _in_kernel(x_ref, sh_ref, sc_ref, nw_ref, w_ref, o_ref):
    x = x_ref[...]
    y = x * lax.rsqrt(jnp.mean(x * x, axis=-1, keepdims=True) + RMS_EPS) * nw_ref[...]
    h = y * (1.0 + sc_ref[0]) + sh_ref[0]
    o_ref[...] = jnp.dot(h.astype(BF16), w_ref[...], preferred_element_type=F32)


def _in_call(x, mod3, layer, nw, w_in_p, n_ctx_tok, lat_t, tile=256):
    n = x.shape[0]
    row = lambda i: layer * 8 + _cond_row(i, tile, n_ctx_tok, lat_t)
    return pl.pallas_call(
        _in_kernel,
        grid=(n // tile,),
        in_specs=[pl.BlockSpec((tile, D_MODEL), lambda i: (i, 0)),
                  pl.BlockSpec((1, 1, D_MODEL), lambda i: (row(i), 0, 0)),
                  pl.BlockSpec((1, 1, D_MODEL), lambda i: (row(i), 0, 1)),
                  pl.BlockSpec((1, D_MODEL), lambda i: (0, 0)),
                  pl.BlockSpec((D_MODEL, P_COLS), lambda i: (0, 0))],
        out_specs=pl.BlockSpec((tile, P_COLS), lambda i: (i, 0)),
        out_shape=jax.ShapeDtypeStruct((n, P_COLS), F32),
        compiler_params=pltpu.CompilerParams(dimension_semantics=("arbitrary",), vmem_limit_bytes=VMEM_LIMIT),
        name="in_proj",
    )(x, mod3, mod3, nw.reshape(1, D_MODEL), w_in_p)


EXP_M05 = math.exp(-0.5)


def _rwkv_kernel(*refs, t_len, n_sub, has_init):
    if has_init:
        (r_ref, k_ref, v_ref, x3_ref, mu_ref, vec_ref, w2_ref, a2_ref, g2_ref, s0_ref,
         y_ref, sfin_ref, xs_ref, yd_ref, st_ref) = refs
    else:
        (r_ref, k_ref, v_ref, x3_ref, mu_ref, vec_ref, w2_ref, a2_ref, g2_ref,
         y_ref, sfin_ref, xs_ref, yd_ref, st_ref) = refs
    n_chunks = t_len // CHUNK
    vec = vec_ref[...]
    bm = _block_ones(LANES, HEAD).astype(BF16)

    def shift_body(c, carry):
        start = pl.multiple_of(c * CHUNK, CHUNK)
        for sub in range(n_sub):
            for idx, ref in enumerate((r_ref, k_ref, v_ref, x3_ref)):
                pc, prev, nxt = _shifted(ref, sub * t_len, start, c, n_chunks, t_len)
                mu0 = mu_ref[0:1, idx * PAIRS_W:(idx + 1) * PAIRS_W]
                mu1 = mu_ref[1:2, idx * PAIRS_W:(idx + 1) * PAIRS_W]
                xs_ref[idx, pl.ds(sub * t_len + start, CHUNK), :] = pc + mu0 * (prev - pc) + mu1 * (nxt - pc)
        return carry

    lax.fori_loop(0, n_chunks, shift_body, 0)

    if has_init:
        st_ref[...] = s0_ref[...]
    else:
        st_ref[...] = jnp.zeros_like(st_ref)

    def prep(d, start):
        k = xs_ref[1, pl.ds(start, CHUNK), :]
        x3 = xs_ref[3, pl.ds(start, CHUNK), :]
        xw, xa = x3[:, 0:LANES], x3[:, LANES:2 * LANES]
        w_raw = vec[d:d + 1] + _mm(jnp.tanh(xw), w2_ref[d])
        lw = -EXP_M05 * _sigmoid(w_raw)
        a = _sigmoid(vec[2 + d:3 + d] + _mm(xa, a2_ref[d]))
        kmod = k * (1.0 + (a - 1.0) * vec[5:6])
        cum = _cumsum_rows(lw, d)
        half = 0.5 * jnp.sum(lw, axis=0, keepdims=True)
        return k * vec[4:5], a, kmod, cum, lw, half

    def seq_step(sub, i):
        off = sub * t_len
        starts = (pl.multiple_of(off + i * CHUNK, CHUNK), pl.multiple_of(off + (n_chunks - 1 - i) * CHUNK, CHUNK))
        pre = [prep(d, starts[d]) for d in range(2)]
        norms = _head_sums([pre[0][0] * pre[0][0], pre[1][0] * pre[1][0]], bm)
        fac = []
        for d in range(2):
            kk, a, kmod, cum, lw, half = pre[d]
            r = xs_ref[0, pl.ds(starts[d], CHUNK), :]
            kk = kk * lax.rsqrt(norms[d] + L2_EPS)
            em = jnp.exp(half)
            e_up = jnp.exp(half - cum)
            rh = r * jnp.exp(cum - half)
            ch = kk * jnp.exp(cum - lw - half)
            bh = kk * a * e_up
            kh = kmod * e_up
            v = xs_ref[2, pl.ds(starts[d], CHUNK), :]
            fac.append(dict(rh=rh, ch=ch, bh=bh, kh=kh, c0=ch * em, r0=rh * em, bt=bh * em, kt=kh * em, v=v,
                            dec=jnp.broadcast_to(em * em, (CHUNK, PAIRS_W))))
        return starts, fac

    def seq_body(i, carry):
        subs = [seq_step(sub, i) for sub in range(n_sub)]
        chains = [(sub, j) for sub in range(n_sub) for j in range(3)]

        def quad(c, name):
            fac, ln = subs[c[0]][1], slice(c[1] * LANES, (c[1] + 1) * LANES)
            return jnp.concatenate([fac[0][name][:, ln], fac[1][name][:, ln]], axis=1)

        incl, strict, t_idx, s_idx = _quad_masks()
        same_head = _same_head()
        v = [quad(c, "v") for c in chains]
        g = [_mm_nt(jnp.concatenate([quad(c, "ch"), quad(c, "rh")], axis=0),
                    jnp.concatenate([_qstack(quad(c, "bh")), _qstack(quad(c, "kh"))], axis=0))
             for c in chains]
        l_cb = [jnp.where(strict, x[:CHUNK, :QUAD], 0.0) for x in g]
        l_ck = [jnp.where(strict, x[:CHUNK, QUAD:], 0.0) for x in g]
        m_rb = [jnp.where(incl, x[CHUNK:, :QUAD], 0.0) for x in g]
        m_rk = [jnp.where(incl, x[CHUNK:, QUAD:], 0.0) for x in g]
        lv_y0 = [_mm(jnp.concatenate([a, b], axis=0), _qstack(x)) for a, b, x in zip(l_ck, m_rk, v)]
        t2 = _inv_unit_tri_many(l_cb, t_idx, s_idx)
        wu = [_mm(t, jnp.concatenate([_qstack(quad(c, "c0")), _qstack(x[:CHUNK])], axis=1))
              for t, c, x in zip(t2, chains, lv_y0)]
        st = [st_ref[c[0], c[1]] for c in chains]
        wr = [_mm_nt(jnp.concatenate([x[:, :QUAD], quad(c, "r0")], axis=0), s)
              for x, c, s in zip(wu, chains, st)]
        u = [-x[:, QUAD:] - y[:CHUNK] for x, y in zip(wu, wr)]
        mu_ = [_mm(m, _qstack(x)) for m, x in zip(m_rb, u)]
        upd = [_mm_tn(jnp.concatenate([x, vv], axis=0), jnp.concatenate([quad(c, "bt"), quad(c, "kt")], axis=0))
               for x, vv, c in zip(u, v, chains)]
        ys = [a[CHUNK:] + b[CHUNK:] + m for a, b, m in zip(lv_y0, wr, mu_)]
        for c, s, x in zip(chains, st, upd):
            st_ref[c[0], c[1]] = s * quad(c, "dec")[0:1] + jnp.where(same_head, x, 0.0)
        for sub in range(n_sub):
            for d in range(2):
                yd_ref[d, pl.ds(subs[sub][0][d], CHUNK), :] = jnp.concatenate(
                    [ys[sub * 3 + j][:, d * LANES:(d + 1) * LANES] for j in range(3)], axis=1)
        return carry

    lax.fori_loop(0, n_chunks, seq_body, 0)
    _emit_pair_states(st_ref, sfin_ref, n_sub)

    def post_body(c, carry):
        starts = [pl.multiple_of(sub * t_len + c * CHUNK, CHUNK) for sub in range(n_sub)]
        r = [xs_ref[0, pl.ds(s, CHUNK), :] for s in starts]
        k = [xs_ref[1, pl.ds(s, CHUNK), :] for s in starts]
        y = [yd_ref[0, pl.ds(s, CHUNK), :] + yd_ref[1, pl.ds(s, CHUNK), :] for s in starts]
        sums = _head_sums(y + [a * b * vec[6:7] for a, b in zip(r, k)], bm)
        yc = [a - m * (1.0 / HEAD) for a, m in zip(y, sums[:n_sub])]
        var = _head_sums([a * a for a in yc], bm)
        xg = jnp.concatenate([xs_ref[3, pl.ds(s, CHUNK), 2 * LANES:3 * LANES] for s in starts], axis=0)
        g = _mm(_sigmoid(xg), g2_ref[...])
        for i, s in enumerate(starts):
            yn = yc[i] * lax.rsqrt(var[i] * (1.0 / HEAD) + RW_LN_EPS) * vec[7:8] + vec[8:9]
            v = xs_ref[2, pl.ds(s, CHUNK), :]
            y_ref[pl.ds(s, CHUNK), :] = (yn + sums[n_sub + i] * v) * g[i * CHUNK:(i + 1) * CHUNK]
        return carry

    lax.fori_loop(0, n_chunks, post_body, 0)


def _rwkv_call(p, base_blk, n_seq, t_len, prm, s0):
    has_init = s0 is not None
    n_sub = _seqs_per_step(n_seq, t_len)
    assert n_seq % n_sub == 0 and base_blk % n_sub == 0
    rows = n_sub * t_len

    def pspec(col_blk):
        return pl.BlockSpec((rows, PAIRS_W), lambda b: (base_blk // n_sub + b, col_blk),
                            pipeline_mode=_seq_buffering(rows))

    full = lambda shape: pl.BlockSpec(shape, lambda b: (0,) * len(shape))
    in_specs = [pspec(0), pspec(1), pspec(2), pspec(3),
                full((2, RW_COLS)), full((16, PAIRS_W)), full((2, LANES, PAIRS_W)), full((2, LANES, PAIRS_W)),
                full((LANES, PAIRS_W))]
    args = [p, p, p, p, prm["mu"], prm["vec"], prm["w2"], prm["a2"], prm["g2"]]
    if has_init:
        in_specs.append(pl.BlockSpec((n_sub, 3, QUAD, QUAD), lambda b: (b, 0, 0, 0)))
        args.append(s0)
    y, sfin = pl.pallas_call(
        functools.partial(_rwkv_kernel, t_len=t_len, n_sub=n_sub, has_init=has_init),
        grid=(n_seq // n_sub,),
        in_specs=in_specs,
        out_specs=[pl.BlockSpec((rows, PAIRS_W), lambda b: (b, 0)),
                   pl.BlockSpec((n_sub, 2, 2 * 3, HEAD, HEAD), lambda b: (b, 0, 0, 0, 0))],
        out_shape=[jax.ShapeDtypeStruct((n_seq * t_len, RW_DIM), F32),
                   jax.ShapeDtypeStruct((n_seq, 2, 2 * 3, HEAD, HEAD), F32)],
        scratch_shapes=[pltpu.VMEM((4, rows, PAIRS_W), F32),
                        pltpu.VMEM((2, rows, PAIRS_W), F32),
                        pltpu.VMEM((n_sub, 3, QUAD, QUAD), F32)],
        compiler_params=pltpu.CompilerParams(dimension_semantics=("arbitrary",), vmem_limit_bytes=VMEM_LIMIT),
        name="rwkv7_mix",
    )(*args)
    return y, sfin


def _gdn_kernel(*refs, t_len, n_sub, has_init):
    if has_init:
        (q_ref, k_ref, v_ref, z_ref, gt_ref, cw_ref, vec_ref, gvec_ref, s0_ref,
         y_ref, sfin_ref, xs_ref, yd_ref, st_ref) = refs
    else:
        (q_ref, k_ref, v_ref, z_ref, gt_ref, cw_ref, vec_ref, gvec_ref,
         y_ref, sfin_ref, xs_ref, yd_ref, st_ref) = refs
    n_chunks = t_len // CHUNK
    vec = vec_ref[...]
    gvec = gvec_ref[...]
    bm = _block_ones(LANES, HEAD).astype(BF16)

    def conv_body(c, carry):
        start = pl.multiple_of(c * CHUNK, CHUNK)
        for sub in range(n_sub):
            for idx, ref in enumerate((q_ref, k_ref, v_ref)):
                pc, prev, nxt = _shifted(ref, sub * t_len, start, c, n_chunks, t_len)
                cw = cw_ref[:, idx * PAIRS_W:(idx + 1) * PAIRS_W]
                xs_ref[idx, pl.ds(sub * t_len + start, CHUNK), :] = _silu(
                    cw[0:1] * prev + cw[1:2] * pc + cw[2:3] * nxt)
        return carry

    lax.fori_loop(0, n_chunks, conv_body, 0)

    if has_init:
        st_ref[...] = s0_ref[...]
    else:
        st_ref[...] = jnp.zeros_like(st_ref)

    def lane_bcast(tile, lane, width):
        return jnp.broadcast_to(tile[:, lane:lane + 1], (tile.shape[0], width))

    def per_head(tile, first_lane):
        return jnp.concatenate([lane_bcast(tile, first_lane + h, HEAD) for h in range(DN_HEADS)], axis=1)

    def seq_step(sub, i):
        off = sub * t_len
        starts = (pl.multiple_of(off + i * CHUNK, CHUNK), pl.multiple_of(off + (n_chunks - 1 - i) * CHUNK, CHUNK))
        qs = [xs_ref[0, pl.ds(starts[d], CHUNK), :] for d in range(2)]
        ks = [xs_ref[1, pl.ds(starts[d], CHUNK), :] for d in range(2)]
        norms = _head_sums([qs[0] * qs[0], ks[0] * ks[0], qs[1] * qs[1], ks[1] * ks[1]], bm)
        fac, gam_tiles = [], []
        for d in range(2):
            gt = gt_ref[pl.ds(starts[d], CHUNK), :]
            q = qs[d] * lax.rsqrt(norms[2 * d] + L2_EPS) * (HEAD ** -0.5)
            k = ks[d] * lax.rsqrt(norms[2 * d + 1] + L2_EPS)
            v = xs_ref[2, pl.ds(starts[d], CHUNK), :]
            g_tile = -jnp.exp(gvec[0:1]) * _softplus(gt + gvec[1:2])
            gam_tile = _cumsum_rows(g_tile, d)
            gam = per_head(gam_tile, 6 * d)
            beta = per_head(_sigmoid(gt), 12 + 6 * d)
            last = gam[CHUNK - 1:CHUNK] if d == 0 else gam[0:1]
            eg = jnp.exp(gam)
            kb = k * beta
            fac.append(dict(q=q, k=k, kb=kb, vb=v * beta, kbe=kb * eg, qe=q * eg, ko=k * jnp.exp(last - gam), gam=gam,
                            el=jnp.broadcast_to(jnp.exp(last), (CHUNK, PAIRS_W))))
            gam_tiles.append(gam_tile)
        gam_rows = [jnp.transpose(gam_tiles[d]) for d in range(2)]
        incl = _quad_masks()[0]
        pairs = []
        for j in range(3):
            ln = slice(j * LANES, (j + 1) * LANES)
            gam_c = jnp.concatenate([fac[0]["gam"][:, ln], fac[1]["gam"][:, ln]], axis=1)
            gam_r = jnp.concatenate([gam_rows[d][6 * d + 2 * j + hh:6 * d + 2 * j + hh + 1, :]
                                     for d in range(2) for hh in range(2)], axis=1)
            pairs.append(jnp.exp(jnp.where(incl, gam_c - gam_r, -jnp.inf)))
        return starts, fac, pairs

    def seq_body(i, carry):
        subs = [seq_step(sub, i) for sub in range(n_sub)]
        chains = [(sub, j) for sub in range(n_sub) for j in range(3)]

        def quad(c, name):
            fac, ln = subs[c[0]][1], slice(c[1] * LANES, (c[1] + 1) * LANES)
            return jnp.concatenate([fac[0][name][:, ln], fac[1][name][:, ln]], axis=1)

        incl, strict, t_idx, s_idx = _quad_masks()
        same_head = _same_head()
        pair = [subs[c[0]][2][c[1]] for c in chains]
        g = [_mm_nt(jnp.concatenate([quad(c, "kb"), quad(c, "q")], axis=0), _qstack(quad(c, "k")))
             for c in chains]
        a2 = [jnp.where(strict, x[:CHUNK] * p, 0.0) for x, p in zip(g, pair)]
        qk = [x[CHUNK:] * p for x, p in zip(g, pair)]
        t2 = _inv_unit_tri_many(a2, t_idx, s_idx)
        uw = [_mm(t, jnp.concatenate([_qstack(quad(c, "vb")), _qstack(quad(c, "kbe"))], axis=1))
              for t, c in zip(t2, chains)]
        st = [st_ref[c[0], c[1]] for c in chains]
        wq = [_mm_nt(jnp.concatenate([x[:, QUAD:], quad(c, "qe")], axis=0), s)
              for x, c, s in zip(uw, chains, st)]
        vn = [x[:, :QUAD] - y[:CHUNK] for x, y in zip(uw, wq)]
        qv = [_mm(m, _qstack(x)) for m, x in zip(qk, vn)]
        upd = [_mm_tn(x, quad(c, "ko")) for x, c in zip(vn, chains)]
        os_ = [y[CHUNK:] + m for y, m in zip(wq, qv)]
        for c, s, x in zip(chains, st, upd):
            st_ref[c[0], c[1]] = s * quad(c, "el")[0:1] + jnp.where(same_head, x, 0.0)
        for sub in range(n_sub):
            for d in range(2):
                yd_ref[d, pl.ds(subs[sub][0][d], CHUNK), :] = jnp.concatenate(
                    [os_[sub * 3 + j][:, d * LANES:(d + 1) * LANES] for j in range(3)], axis=1)
        return carry

    lax.fori_loop(0, n_chunks, seq_body, 0)
    _emit_pair_states(st_ref, sfin_ref, n_sub)

    def post_body(c, carry):
        starts = [pl.multiple_of(sub * t_len + c * CHUNK, CHUNK) for sub in range(n_sub)]
        o = [yd_ref[0, pl.ds(s, CHUNK), :] + yd_ref[1, pl.ds(s, CHUNK), :] for s in starts]
        ms = _head_sums([a * a for a in o], bm)
        for i, s in enumerate(starts):
            z = z_ref[pl.ds(s, CHUNK), :]
            y_ref[pl.ds(s, CHUNK), :] = o[i] * lax.rsqrt(ms[i] * (1.0 / HEAD) + RMS_EPS) * vec[0:1] * _silu(z)
        return carry

    lax.fori_loop(0, n_chunks, post_body, 0)


def _gdn_call(p, base_blk, n_seq, t_len, prm, s0):
    has_init = s0 is not None
    dn_blk = DN_OFF // PAIRS_W
    n_sub = _seqs_per_step(n_seq, t_len)
    assert n_seq % n_sub == 0 and base_blk % n_sub == 0
    rows = n_sub * t_len

    def pspec(col_blk):
        return pl.BlockSpec((rows, PAIRS_W), lambda b: (base_blk // n_sub + b, dn_blk + col_blk),
                            pipeline_mode=_seq_buffering(rows))

    full = lambda shape: pl.BlockSpec(shape, lambda b: (0,) * len(shape))
    in_specs = [pspec(0), pspec(1), pspec(2), pspec(3),
                pl.BlockSpec((rows, LANES), lambda b: (base_blk // n_sub + b, DN_GATE_OFF // LANES)),
                full((8, 3 * PAIRS_W)), full((8, PAIRS_W)), full((8, LANES))]
    args = [p, p, p, p, p, prm["cw"], prm["vec"], prm["gvec"]]
    if has_init:
        in_specs.append(pl.BlockSpec((n_sub, 3, QUAD, QUAD), lambda b: (b, 0, 0, 0)))
        args.append(s0)
    y, sfin = pl.pallas_call(
        functools.partial(_gdn_kernel, t_len=t_len, n_sub=n_sub, has_init=has_init),
        grid=(n_seq // n_sub,),
        in_specs=in_specs,
        out_specs=[pl.BlockSpec((rows, PAIRS_W), lambda b: (b, 0)),
                   pl.BlockSpec((n_sub, 2, 2 * 3, HEAD, HEAD), lambda b: (b, 0, 0, 0, 0))],
        out_shape=[jax.ShapeDtypeStruct((n_seq * t_len, DN_DIM), F32),
                   jax.ShapeDtypeStruct((n_seq, 2, 2 * 3, HEAD, HEAD), F32)],
        scratch_shapes=[pltpu.VMEM((3, rows, PAIRS_W), F32),
                        pltpu.VMEM((2, rows, PAIRS_W), F32),
                        pltpu.VMEM((n_sub, 3, QUAD, QUAD), F32)],
        compiler_params=pltpu.CompilerParams(dimension_semantics=("arbitrary",), vmem_limit_bytes=VMEM_LIMIT),
        name="gdn_mix",
    )(*args)
    return y, sfin


def _gla_kernel(*refs, t_len, n_sub, has_init):
    if has_init:
        (q_ref, k_ref, v_ref, og_ref, gt_ref, gk2_ref, vec_ref, s0_ref,
         y_ref, sfin_ref, yd_ref, st_ref) = refs
    else:
        (q_ref, k_ref, v_ref, og_ref, gt_ref, gk2_ref, vec_ref,
         y_ref, sfin_ref, yd_ref, st_ref) = refs
    n_chunks = t_len // CHUNK
    vec = vec_ref[...]
    sr = _iota((GLA_DIM, GLA_KDIM), 0)
    sc = _iota((GLA_DIM, GLA_KDIM), 1)
    st_mask = _shr(sr, GLA_DV) == _shr(sc, GLA_DK)

    if has_init:
        st_ref[...] = s0_ref[...]
    else:
        st_ref[...] = jnp.zeros_like(st_ref)

    def seq_body(i, carry):
        chains = [(sub, d) for sub in range(n_sub) for d in range(2)]
        starts = [pl.multiple_of(sub * t_len + (i if d == 0 else n_chunks - 1 - i) * CHUNK, CHUNK)
                  for sub, d in chains]
        t_idx = _iota((CHUNK, GLA_DIM), 0)
        s_idx = _iota((CHUNK, GLA_DIM), 1) & (CHUNK - 1)
        k_lane = _shr(_iota((CHUNK, GLA_KDIM), 1), GLA_DK)
        q = [q_ref[pl.ds(s, CHUNK), :] * (GLA_DK ** -0.5) for s in starts]
        k = [k_ref[pl.ds(s, CHUNK), :] for s in starts]
        v = [v_ref[pl.ds(s, CHUNK), :] for s in starts]
        gk = [_log_sigmoid(_mm(gt_ref[pl.ds(s, CHUNK), :], gk2_ref[c[1]]) + vec[c[1]:c[1] + 1, :GLA_KDIM])
              * (1.0 / GLA_GATE_NORM) for s, c in zip(starts, chains)]
        b = [_cumsum_rows(x, c[1]) for x, c in zip(gk, chains)]
        tot = [jnp.sum(x, axis=0, keepdims=True) for x in gk]
        kt = [kk * jnp.exp(0.5 * tt - bb) for kk, tt, bb in zip(k, tot, b)]
        k_stack = [jnp.concatenate([jnp.where(k_lane == h, x, 0.0) for h in range(GLA_HEADS)], axis=0) for x in kt]
        scores = [_mm_nt(qq * jnp.exp(bb - 0.5 * tt), ks) for qq, bb, tt, ks in zip(q, b, tot, k_stack)]
        scores = [jnp.where((s_idx <= t_idx) if c[1] == 0 else (s_idx >= t_idx), x, 0.0)
                  for x, c in zip(scores, chains)]
        st = [st_ref[c[0], c[1]] for c in chains]
        o_intra = [_mm(x, _qstack(vv)) for x, vv in zip(scores, v)]
        o_inter = [_mm_nt(qq * jnp.exp(bb), s) for qq, bb, s in zip(q, b, st)]
        upd = [_mm_tn(vv, kk * jnp.exp(tt - bb)) for vv, kk, tt, bb in zip(v, k, tot, b)]
        for idx, c in enumerate(chains):
            st_ref[c[0], c[1]] = st[idx] * jnp.exp(tot[idx]) + jnp.where(st_mask, upd[idx], 0.0)
            yd_ref[c[1], pl.ds(starts[idx], CHUNK), :] = o_intra[idx] + o_inter[idx]
        return carry

    lax.fori_loop(0, n_chunks, seq_body, 0)
    for sub in range(n_sub):
        for d in range(2):
            for h in range(GLA_HEADS):
                sfin_ref[sub, d, h] = st_ref[sub, d, h * GLA_DV:(h + 1) * GLA_DV, h * GLA_DK:(h + 1) * GLA_DK]

    bm = _block_ones(LANES, GLA_DV).astype(BF16)

    def post_body(c, carry):
        start = pl.multiple_of(c * CHUNK, CHUNK)
        o = yd_ref[0, pl.ds(start, CHUNK), :] + yd_ref[1, pl.ds(start, CHUNK), :]
        og = og_ref[pl.ds(start, CHUNK), :]
        ms = _head_sums([o * o], bm)[0] * (1.0 / GLA_DV)
        y_ref[pl.ds(start, CHUNK), :] = o * lax.rsqrt(ms + RMS_EPS) * vec[2:3] * _silu(og)
        return carry

    lax.fori_loop(0, n_sub * n_chunks, post_body, 0)


def _gla_call(p, base_blk, n_seq, t_len, prm, s0):
    has_init = s0 is not None
    t128 = GLA_OFF // LANES
    t256 = GLA_OFF // GLA_DIM
    n_sub = _seqs_per_step(n_seq, t_len)
    assert n_seq % n_sub == 0 and base_blk % n_sub == 0
    rows = n_sub * t_len
    blk = lambda b: base_blk // n_sub + b
    in_specs = [pl.BlockSpec((rows, LANES), lambda b: (blk(b), t128)),
                pl.BlockSpec((rows, LANES), lambda b: (blk(b), t128 + 1)),
                pl.BlockSpec((rows, GLA_DIM), lambda b: (blk(b), t256 + 1)),
                pl.BlockSpec((rows, GLA_DIM), lambda b: (blk(b), t256 + 2)),
                pl.BlockSpec((rows, LANES), lambda b: (blk(b), t128 + 6)),
                pl.BlockSpec((2, LANES, LANES), lambda b: (0, 0, 0)),
                pl.BlockSpec((8, GLA_DIM), lambda b: (0, 0))]
    args = [p, p, p, p, p, prm["gk2"], prm["vec"]]
    if has_init:
        in_specs.append(pl.BlockSpec((n_sub, 2, GLA_DIM, GLA_KDIM), lambda b: (b, 0, 0, 0)))
        args.append(s0)
    y, sfin = pl.pallas_call(
        functools.partial(_gla_kernel, t_len=t_len, n_sub=n_sub, has_init=has_init),
        grid=(n_seq // n_sub,),
        in_specs=in_specs,
        out_specs=[pl.BlockSpec((rows, GLA_DIM), lambda b: (b, 0)),
                   pl.BlockSpec((n_sub, 2, GLA_HEADS, GLA_DV, GLA_DK), lambda b: (b, 0, 0, 0, 0))],
        out_shape=[jax.ShapeDtypeStruct((n_seq * t_len, GLA_DIM), F32),
                   jax.ShapeDtypeStruct((n_seq, 2, GLA_HEADS, GLA_DV, GLA_DK), F32)],
        scratch_shapes=[pltpu.VMEM((2, rows, GLA_DIM), F32),
                        pltpu.VMEM((n_sub, 2, GLA_DIM, GLA_KDIM), F32)],
        compiler_params=pltpu.CompilerParams(dimension_semantics=("arbitrary",), vmem_limit_bytes=VMEM_LIMIT),
        name="gla_mix",
    )(*args)
    return y, sfin


def _out_kernel(yr_ref, yg_ref, yd_ref, x_ref, g1_ref, sh_ref, sc_ref, nw_ref, wo_ref, wr_ref, br_ref,
                x1_ref, h2_ref, grp_ref):
    mix = (_mm(yr_ref[...], wo_ref[0:RW_DIM, :]) + _mm(yg_ref[...], wo_ref[RW_DIM:RW_DIM + GLA_DIM, :])
           + _mm(yd_ref[...], wo_ref[RW_DIM + GLA_DIM:, :]))
    x1 = x_ref[...] + g1_ref[0] * mix
    x1_ref[...] = x1
    h2 = x1 * lax.rsqrt(jnp.mean(x1 * x1, axis=-1, keepdims=True) + RMS_EPS) * nw_ref[...]
    h2 = h2 * (1.0 + sc_ref[0]) + sh_ref[0]
    for s in range(ROW_SLABS):
        h2_ref[:, s, :] = h2[:, s * LANES:(s + 1) * LANES]
    logits = _mm_split(h2, wr_ref[...]) + br_ref[...]
    lane_i = _iota(logits.shape, 1)
    lg = jnp.where(lane_i < N_GROUPS, logits, -jnp.inf)
    gmax = jnp.max(lg, axis=1, keepdims=True)
    gsel = jnp.min(jnp.where(lg == gmax, lane_i.astype(F32), 1e9), axis=1, keepdims=True)
    grp_ref[...] = jnp.broadcast_to(gsel, logits.shape).astype(jnp.int32)


def _out_call(y_rw, y_gla, y_dn, x, mod3, layer, nw, w_out, wr, br, n_ctx_tok, lat_t, tile=256):
    n = x.shape[0]
    row = lambda i: layer * 8 + _cond_row(i, tile, n_ctx_tok, lat_t)
    def modspec(part):
        return pl.BlockSpec((1, 1, D_MODEL), lambda i: (row(i), 0, part))

    return pl.pallas_call(
        _out_kernel,
        grid=(n // tile,),
        in_specs=[pl.BlockSpec((tile, RW_DIM), lambda i: (i, 0)),
                  pl.BlockSpec((tile, GLA_DIM), lambda i: (i, 0)),
                  pl.BlockSpec((tile, DN_DIM), lambda i: (i, 0)),
                  pl.BlockSpec((tile, D_MODEL), lambda i: (i, 0)),
                  modspec(2), modspec(3), modspec(4),
                  pl.BlockSpec((1, D_MODEL), lambda i: (0, 0)),
                  pl.BlockSpec((D_MODEL, D_MODEL), lambda i: (0, 0)),
                  pl.BlockSpec((D_MODEL, LANES), lambda i: (0, 0)),
                  pl.BlockSpec((1, LANES), lambda i: (0, 0))],
        out_specs=[pl.BlockSpec((tile, D_MODEL), lambda i: (i, 0)),
                   pl.BlockSpec((tile, ROW_SLABS, LANES), lambda i: (i, 0, 0)),
                   pl.BlockSpec((tile, LANES), lambda i: (i, 0))],
        out_shape=[jax.ShapeDtypeStruct((n, D_MODEL), F32),
                   jax.ShapeDtypeStruct((n, ROW_SLABS, LANES), F32),
                   jax.ShapeDtypeStruct((n, LANES), jnp.int32)],
        compiler_params=pltpu.CompilerParams(dimension_semantics=("arbitrary",), vmem_limit_bytes=VMEM_LIMIT),
        name="out_proj_router",
    )(y_rw, y_gla, y_dn, x, mod3, mod3, mod3, nw.reshape(1, D_MODEL), w_out, wr, br)


ROW_SLABS = D_MODEL // LANES
MOE_TILE = 512
DMA_TILE = 256


def _route_slots(grp, n_tok):
    gid = grp[:, 0]
    onehot = (gid[:, None] == jnp.arange(N_GROUPS, dtype=jnp.int32)[None, :]).astype(jnp.int32)
    csum = jnp.cumsum(onehot, axis=0)
    rank = jnp.sum(csum * onehot, axis=1) - 1
    counts = csum[-1]
    tiles_per = (counts + MOE_TILE - 1) // MOE_TILE
    tile_end = jnp.cumsum(tiles_per)
    offs = (tile_end - tiles_per) * MOE_TILE
    slot = jnp.sum(offs[None, :] * onehot, axis=1) + rank
    max_tiles = n_tok // MOE_TILE + N_GROUPS
    tile_group = jnp.sum((jnp.arange(max_tiles, dtype=jnp.int32)[:, None] >= tile_end[None, :]).astype(jnp.int32), axis=1)
    tile_group = jnp.minimum(tile_group, N_GROUPS - 1)
    return slot.astype(jnp.int32), tile_group.astype(jnp.int32), tile_end[-1:].astype(jnp.int32), max_tiles


def _row_maps(slot, n_tok, n_rows):
    tok = jnp.full((n_rows,), -1, jnp.int32).at[slot].set(jnp.arange(n_tok, dtype=jnp.int32))
    pad = tok < 0
    spare = n_tok + jnp.cumsum(pad.astype(jnp.int32)) - 1
    return jnp.where(pad, 0, tok), jnp.where(pad, spare, tok)


def _moe_group_kernel(tg_ref, nt_ref, src_ref, dst_ref, h_ref, init_ref, wr_ref, br_ref, wg_ref, wu_ref, wd_ref,
                      y_ref, xbuf_ref, obuf_ref, acc_ref, gate_ref, xb_ref, gsem, ssem):
    del init_ref
    i = pl.program_id(0)
    e = pl.program_id(1)
    n_tiles = nt_ref[0]
    buf = lax.rem(i, 2)

    def start_gather(tile, b):
        def body(r, carry):
            pltpu.make_async_copy(h_ref.at[src_ref[tile * MOE_TILE + r]], xbuf_ref.at[b, r], gsem.at[b]).start()
            return carry
        lax.fori_loop(0, MOE_TILE, body, 0, unroll=8)

    def wait_gather(b):
        pltpu.make_async_copy(h_ref.at[pl.ds(0, MOE_TILE)], xbuf_ref.at[b], gsem.at[b]).wait()

    def start_scatter(tile, b):
        def body(r, carry):
            pltpu.make_async_copy(obuf_ref.at[b, r], y_ref.at[dst_ref[tile * MOE_TILE + r]], ssem.at[b]).start()
            return carry
        lax.fori_loop(0, MOE_TILE, body, 0, unroll=8)

    def wait_scatter(b):
        pltpu.make_async_copy(obuf_ref.at[b], y_ref.at[pl.ds(0, MOE_TILE)], ssem.at[b]).wait()

    @pl.when(i < n_tiles)
    def _():
        @pl.when(e == 0)
        def _():
            @pl.when(i == 0)
            def _():
                start_gather(0, 0)

            wait_gather(buf)

            @pl.when(i + 1 < n_tiles)
            def _():
                start_gather(i + 1, 1 - buf)

            acc_ref[...] = jnp.zeros_like(acc_ref)
            x = jnp.concatenate([xbuf_ref[buf, :, s, :] for s in range(ROW_SLABS)], axis=1)
            xb_ref[...] = x.astype(BF16)
            logits = _mm_split(x, wr_ref[...]) + br_ref[...]
            lane_i = _iota(logits.shape, 1)
            lane = lane_i.astype(F32)
            grp = tg_ref[i]
            lg = jnp.where(lane_i < N_GROUPS, logits, -jnp.inf)
            gmax = jnp.max(lg, axis=1, keepdims=True)
            l_sel = jnp.sum(jnp.where(lane_i == grp, logits, 0.0), axis=1, keepdims=True)
            p_grp = jnp.exp(l_sel - gmax) / jnp.sum(jnp.exp(lg - gmax), axis=1, keepdims=True)
            first = N_GROUPS + grp * EXPERTS_PER_GROUP
            le = jnp.where((lane_i >= first) & (lane_i < first + EXPERTS_PER_GROUP), logits, -jnp.inf)
            v1 = jnp.max(le, axis=1, keepdims=True)
            i1 = jnp.min(jnp.where(le == v1, lane, 1e9), axis=1, keepdims=True)
            le2 = jnp.where(lane == i1, -jnp.inf, le)
            v2 = jnp.max(le2, axis=1, keepdims=True)
            i2 = jnp.min(jnp.where(le2 == v2, lane, 1e9), axis=1, keepdims=True)
            e2 = jnp.exp(v2 - v1)
            p1 = p_grp / (1.0 + e2)
            p2 = p_grp * e2 / (1.0 + e2)
            gate_ref[...] = jnp.where(lane == i1, p1, jnp.where(lane == i2, p2, 0.0))

        h = xb_ref[...]
        gates = gate_ref[...]
        lane_g = _iota(gates.shape, 1)
        gcol = jnp.sum(jnp.where(lane_g == N_GROUPS + tg_ref[i] * EXPERTS_PER_GROUP + e, gates, 0.0),
                       axis=1, keepdims=True)
        hid = _silu(jnp.dot(h, wg_ref[0].astype(BF16), preferred_element_type=F32))
        hid = hid * jnp.dot(h, wu_ref[0].astype(BF16), preferred_element_type=F32) * gcol
        acc_ref[...] += _mm(hid, wd_ref[0])

        @pl.when(e == EXPERTS_PER_GROUP - 1)
        def _():
            @pl.when(i >= 2)
            def _():
                wait_scatter(buf)

            acc = acc_ref[...]
            for s in range(ROW_SLABS):
                obuf_ref[buf, :, s, :] = acc[:, s * LANES:(s + 1) * LANES]
            start_scatter(i, buf)

            @pl.when(i == n_tiles - 1)
            def _():
                wait_scatter(buf)

                @pl.when(i >= 1)
                def _():
                    wait_scatter(1 - buf)


def _moe_group_call(tile_group, n_tiles, max_tiles, src_row, dst_row, h_rows, layer, wr, br, w_gate, w_up, w_down):
    n_rows = max_tiles * MOE_TILE
    tile_idx = lambda i, nt: jnp.minimum(i, nt[0] - 1)
    expert = lambda i, e, tg, nt: (layer * N_EXPERTS + tg[tile_idx(i, nt)] * EXPERTS_PER_GROUP
                                   + jnp.where(i < nt[0], e, EXPERTS_PER_GROUP - 1))
    wspec = lambda shape: pl.BlockSpec(shape, lambda i, e, tg, nt, sr, ds: (expert(i, e, tg, nt), 0, 0))
    init = jnp.zeros((n_rows, ROW_SLABS, LANES), F32)
    return pl.pallas_call(
        _moe_group_kernel,
        grid_spec=pltpu.PrefetchScalarGridSpec(
            num_scalar_prefetch=4,
            grid=(max_tiles, EXPERTS_PER_GROUP),
            in_specs=[pl.BlockSpec(memory_space=pl.ANY), pl.BlockSpec(memory_space=pl.ANY),
                      pl.BlockSpec((D_MODEL, LANES), lambda i, e, tg, nt, sr, ds: (0, 0)),
                      pl.BlockSpec((1, LANES), lambda i, e, tg, nt, sr, ds: (0, 0)),
                      wspec((1, D_MODEL, D_EXPERT)), wspec((1, D_MODEL, D_EXPERT)), wspec((1, D_EXPERT, D_MODEL))],
            out_specs=pl.BlockSpec(memory_space=pl.ANY),
            scratch_shapes=[pltpu.VMEM((2, MOE_TILE, ROW_SLABS, LANES), F32),
                            pltpu.VMEM((2, MOE_TILE, ROW_SLABS, LANES), F32),
                            pltpu.VMEM((MOE_TILE, D_MODEL), F32),
                            pltpu.VMEM((MOE_TILE, LANES), F32),
                            pltpu.VMEM((MOE_TILE, D_MODEL), BF16),
                            pltpu.SemaphoreType.DMA((2,)),
                            pltpu.SemaphoreType.DMA((2,))]),
        out_shape=jax.ShapeDtypeStruct((n_rows, ROW_SLABS, LANES), F32),
        input_output_aliases={5: 0},
        compiler_params=pltpu.CompilerParams(dimension_semantics=("arbitrary", "arbitrary"),
                                             vmem_limit_bytes=VMEM_LIMIT),
        name="moe_grouped",
    )(tile_group, n_tiles, src_row, dst_row, h_rows, init, wr, br, w_gate, w_up, w_down)


def _combine_kernel(y_ref, x_ref, g2_ref, o_ref):
    for s in range(ROW_SLABS):
        ln = slice(s * LANES, (s + 1) * LANES)
        o_ref[:, ln] = x_ref[:, ln] + g2_ref[0][:, ln] * y_ref[:, s, :]


def _combine_call(y_rows, x1, mod3, layer, n_ctx_tok, lat_t):
    n = x1.shape[0]
    row = lambda i: layer * 8 + _cond_row(i, DMA_TILE, n_ctx_tok, lat_t)
    return pl.pallas_call(
        _combine_kernel,
        grid=(n // DMA_TILE,),
        in_specs=[pl.BlockSpec((DMA_TILE, ROW_SLABS, LANES), lambda i: (i, 0, 0)),
                  pl.BlockSpec((DMA_TILE, D_MODEL), lambda i: (i, 0)),
                  pl.BlockSpec((1, 1, D_MODEL), lambda i: (row(i), 0, 5))],
        out_specs=pl.BlockSpec((DMA_TILE, D_MODEL), lambda i: (i, 0)),
        out_shape=jax.ShapeDtypeStruct((n, D_MODEL), F32),
        compiler_params=pltpu.CompilerParams(dimension_semantics=("arbitrary",), vmem_limit_bytes=VMEM_LIMIT),
        name="moe_combine",
    )(y_rows, x1, mod3)


def _norm_kernel(x_ref, w_ref, o_ref):
    x = x_ref[...]
    o_ref[...] = x * lax.rsqrt(jnp.mean(x * x, axis=-1, keepdims=True) + RMS_EPS) * w_ref[...]


def _norm_call(x, w, base_tile, n_tiles, tile=256):
    return pl.pallas_call(
        _norm_kernel,
        grid=(n_tiles,),
        in_specs=[pl.BlockSpec((tile, D_MODEL), lambda i: (base_tile + i, 0)),
                  pl.BlockSpec((1, D_MODEL), lambda i: (0, 0))],
        out_specs=pl.BlockSpec((tile, D_MODEL), lambda i: (i, 0)),
        out_shape=jax.ShapeDtypeStruct((n_tiles * tile, D_MODEL), F32),
        compiler_params=pltpu.CompilerParams(dimension_semantics=("arbitrary",), vmem_limit_bytes=VMEM_LIMIT),
        name="final_norm",
    )(x, w.reshape(1, D_MODEL))


def _pos_embed(n_tokens, dim):
    rows = n_tokens // GRID_W
    row = jnp.broadcast_to(jnp.arange(rows)[:, None], (rows, GRID_W)).reshape(-1)
    col = jnp.broadcast_to(jnp.arange(GRID_W)[None, :], (rows, GRID_W)).reshape(-1)
    quarter = dim // 4
    omega = 1.0 / (POS_BASE ** (jnp.arange(quarter, dtype=F32) / quarter))

    def axis_embed(pos):
        ang = pos.astype(F32)[:, None] * omega[None, :]
        return jnp.concatenate([jnp.sin(ang), jnp.cos(ang)], axis=-1)

    return jnp.concatenate([axis_embed(row), axis_embed(col)], axis=-1)


def _relayout_w_in(w_in_l):
    rw = w_in_l[:, :RW_COLS]
    g0 = RW_COLS
    q, k, v = w_in_l[:, g0:g0 + 128], w_in_l[:, g0 + 128:g0 + 256], w_in_l[:, g0 + 256:g0 + 512]
    gates = w_in_l[:, g0 + 512:g0 + 544]
    og = w_in_l[:, g0 + 544:g0 + 800]
    d0 = g0 + 800
    qkvz = w_in_l[:, d0:d0 + 1536]
    dgates = w_in_l[:, d0 + 1536:d0 + 1560]
    z = lambda n: jnp.zeros((D_MODEL, n), w_in_l.dtype)
    return jnp.concatenate([rw, qkvz, q, k, v, og, gates, z(96), dgates, z(104)], axis=1).astype(BF16)


def _rwkv_params(mu, w0, w2, a0, a2, g2, kk, ka, rk, ln_w, ln_b):
    rows = [w0[0], w0[1], a0[0], a0[1], kk, ka, rk, ln_w, ln_b]
    vec = jnp.stack(rows + [jnp.zeros_like(kk)] * 7)
    zero = jnp.zeros((64, RW_DIM), F32)
    w2c = jnp.stack([jnp.concatenate([w2[0], zero]), jnp.concatenate([zero, w2[1]])])
    a2c = jnp.stack([jnp.concatenate([a2[0], zero]), jnp.concatenate([zero, a2[1]])])
    return {"mu": mu, "vec": vec, "w2": w2c, "a2": a2c, "g2": g2}


def _gdn_params(conv, a_log, dt_bias, norm_w):
    cw = jnp.concatenate([conv, jnp.zeros((5, 3 * DN_DIM), F32)], axis=0)
    vec = jnp.zeros((8, DN_DIM), F32).at[0].set(jnp.tile(norm_w, DN_HEADS))
    gvec = jnp.zeros((8, LANES), F32)
    gvec = gvec.at[0, 0:12].set(a_log.reshape(12)).at[1, 0:12].set(dt_bias.reshape(12))
    return {"cw": cw, "vec": vec, "gvec": gvec}


def _gla_params(gk2, gk_b, norm_w):
    g = jnp.zeros((2, LANES, LANES), F32)
    g = g.at[0, 0:16].set(gk2[0]).at[1, 16:32].set(gk2[1])
    vec = jnp.zeros((8, GLA_DIM), F32)
    vec = vec.at[0, :GLA_KDIM].set(gk_b[0]).at[1, :GLA_KDIM].set(gk_b[1]).at[2].set(jnp.tile(norm_w, GLA_HEADS))
    return {"gk2": g, "vec": vec}


def _pair_state_in(s):
    b = s.shape[0]
    st = jnp.swapaxes(s, -1, -2).reshape(b, 2, 3, 2, HEAD, HEAD)
    st = jnp.transpose(st, (0, 2, 1, 3, 4, 5)).reshape(b, 3, 4, HEAD, HEAD)
    bd = jnp.einsum("bjgvk,gh->bjgvhk", st, jnp.eye(4, dtype=s.dtype))
    return bd.reshape(b, 3, QUAD, QUAD)


def _pair_state_out(st):
    return jnp.swapaxes(st, -1, -2)


def _gla_state_in(s):
    b = s.shape[0]
    st = jnp.swapaxes(s, -1, -2)
    eye = jnp.eye(GLA_HEADS, dtype=s.dtype)
    bd = jnp.einsum("bdhvk,hg->bdhvgk", st, eye)
    return bd.reshape(b, 2, GLA_DIM, GLA_KDIM)


def _gla_state_out(st):
    return jnp.swapaxes(st, -1, -2)


def kernel(x_prompt, x_sample, c, state_rwkv, state_gla, state_delta, c_ctx, ada_w, ada_b, norm1_w, norm2_w, final_norm_w, w_in, w_out, rwkv_mu, rwkv_w0, rwkv_w2, rwkv_a0, rwkv_a2, rwkv_g2, rwkv_kk, rwkv_ka, rwkv_rk, rwkv_ln_w, rwkv_ln_b, gla_gk2, gla_gk_b, gla_norm_w, dn_conv, dn_a_log, dn_dt_bias, dn_norm_w, moe_wg, moe_bg, moe_we, moe_be, moe_w_gate, moe_w_up, moe_w_down):
    n_ctx, t_ctx, d = x_prompt.shape
    n_lat, t_lat, _ = x_sample.shape
    depth = ada_w.shape[0]
    n_ctx_tok = n_ctx * t_ctx
    n_lat_tok = n_lat * t_lat
    assert d == D_MODEL and n_ctx_tok % 1024 == 0 and t_lat % 1024 == 0 and t_ctx % 256 == 0 and n_lat <= 7
    assert n_ctx_tok % t_lat == 0

    xs = x_sample + _pos_embed(t_lat, d)[None].astype(x_sample.dtype)
    x = jnp.concatenate([x_prompt.reshape(n_ctx_tok, d), xs.reshape(n_lat_tok, d)], axis=0)

    cond8 = jnp.zeros((8, d), F32).at[0].set(c_ctx).at[1:1 + n_lat].set(c)
    mod = _ada_call(cond8, ada_w, ada_b)
    mod3 = mod.reshape(depth * 8, 1, 6 * d)

    w_gate = moe_w_gate.reshape(depth * N_EXPERTS, d, D_EXPERT)
    w_up = moe_w_up.reshape(depth * N_EXPERTS, d, D_EXPERT)
    w_down = moe_w_down.reshape(depth * N_EXPERTS, D_EXPERT, d)

    lat_base = n_ctx_tok // t_lat
    rw_states, gla_states, dn_states = [], [], []
    for l in range(depth):
        p = _in_call(x, mod3, l, norm1_w[l], _relayout_w_in(w_in[l]), n_ctx_tok, t_lat)
        rw_p = _rwkv_params(rwkv_mu[l], rwkv_w0[l], rwkv_w2[l], rwkv_a0[l], rwkv_a2[l], rwkv_g2[l], rwkv_kk[l],
                            rwkv_ka[l], rwkv_rk[l], rwkv_ln_w[l], rwkv_ln_b[l])
        dn_p = _gdn_params(dn_conv[l], dn_a_log[l], dn_dt_bias[l], dn_norm_w[l])
        gla_p = _gla_params(gla_gk2[l], gla_gk_b[l], gla_norm_w[l])

        y_rw_c, s_rw = _rwkv_call(p, 0, n_ctx, t_ctx, rw_p, None)
        y_rw_l, _ = _rwkv_call(p, lat_base, n_lat, t_lat, rw_p, _pair_state_in(state_rwkv[:, l]))
        y_dn_c, s_dn = _gdn_call(p, 0, n_ctx, t_ctx, dn_p, None)
        y_dn_l, _ = _gdn_call(p, lat_base, n_lat, t_lat, dn_p, _pair_state_in(state_delta[:, l]))
        y_gla_c, s_gla = _gla_call(p, 0, n_ctx, t_ctx, gla_p, None)
        y_gla_l, _ = _gla_call(p, lat_base, n_lat, t_lat, gla_p, _gla_state_in(state_gla[:, l]))
        rw_states.append(_pair_state_out(s_rw))
        dn_states.append(_pair_state_out(s_dn))
        gla_states.append(_gla_state_out(s_gla))

        wr = jnp.zeros((d, LANES), F32)
        wr = wr.at[:, :N_GROUPS].set(moe_wg[l]).at[:, N_GROUPS:N_GROUPS + N_EXPERTS].set(moe_we[l].reshape(d, N_EXPERTS))
        br = jnp.zeros((1, LANES), F32)
        br = br.at[0, :N_GROUPS].set(moe_bg[l]).at[0, N_GROUPS:N_GROUPS + N_EXPERTS].set(moe_be[l].reshape(N_EXPERTS))
        x1, h2_rows, grp = _out_call(jnp.concatenate([y_rw_c, y_rw_l]), jnp.concatenate([y_gla_c, y_gla_l]),
                                     jnp.concatenate([y_dn_c, y_dn_l]), x, mod3, l, norm2_w[l],
                                     w_out[l].astype(BF16), wr, br, n_ctx_tok, t_lat)
        n_tok = n_ctx_tok + n_lat_tok
        slot, tile_group, n_tiles, max_tiles = _route_slots(grp, n_tok)
        src_row, dst_row = _row_maps(slot, n_tok, max_tiles * MOE_TILE)
        y_rows = _moe_group_call(tile_group, n_tiles, max_tiles, src_row, dst_row, h2_rows, l, wr, br,
                                 w_gate, w_up, w_down)
        x = _combine_call(y_rows, x1, mod3, l, n_ctx_tok, t_lat)

    y_prompt = _norm_call(x, final_norm_w, 0, n_ctx_tok // 256).reshape(n_ctx, t_ctx, d)
    y_sample = _norm_call(x, final_norm_w, n_ctx_tok // 256, n_lat_tok // 256).reshape(n_lat, t_lat, d)
    new_rw = jnp.stack(rw_states, axis=1).astype(x_prompt.dtype)
    new_gla = jnp.stack(gla_states, axis=1).astype(x_prompt.dtype)
    new_dn = jnp.stack(dn_states, axis=1).astype(x_prompt.dtype)
    return (y_prompt, y_sample, new_rw, new_gla, new_dn)
```

```python
import functools
import math

import jax
import jax.numpy as jnp
from jax import lax
from jax.experimental import pallas as pl
from jax.experimental.pallas import tpu as pltpu

F32 = jnp.float32
BF16 = jnp.bfloat16

D_MODEL = 1024
DEPTH = 4
GRID_W = 64
POS_BASE = 10000.0
RMS_EPS = 1e-6
L2_EPS = 1e-6

RW_HEADS = 6
RW_DIM = 384
RW_COLS = 1536
RW_LN_EPS = 64e-5
GLA_HEADS = 4
GLA_DK = 32
GLA_DV = 64
GLA_KDIM = 128
GLA_DIM = 256
GLA_GATE_RANK = 16
GLA_GATE_NORM = 16.0
DN_HEADS = 6
DN_DIM = 384
N_GROUPS = 4
EXPERTS_PER_GROUP = 4
N_EXPERTS = 16
D_EXPERT = 512

LANES = 128
CHUNK = 64
HEAD = 64
PAIRS_W = 384
DN_OFF = RW_COLS
GLA_OFF = DN_OFF + 4 * PAIRS_W
GLA_PCOLS = 896
DN_GATE_OFF = GLA_OFF + GLA_PCOLS
P_COLS = DN_GATE_OFF + LANES
VMEM_LIMIT = 56 * 1024 * 1024
SEQS_PER_STEP = 2
MIXER_ROWS_PER_STEP = 1024


def _seqs_per_step(n_seq, t_len):
    n_sub = max(SEQS_PER_STEP, MIXER_ROWS_PER_STEP // t_len)
    while n_seq % n_sub:
        n_sub //= 2
    return n_sub


def _seq_buffering(rows):
    return pl.Buffered(1) if rows * PAIRS_W * 4 > (2 << 20) else None


def _mm(a, b):
    return jnp.dot(a.astype(BF16), b.astype(BF16), preferred_element_type=F32)


def _mm_nt(a, b):
    return lax.dot_general(a.astype(BF16), b.astype(BF16), (((1,), (1,)), ((), ())),
                           preferred_element_type=F32)


def _mm_tn(a, b):
    return lax.dot_general(a.astype(BF16), b.astype(BF16), (((0,), (0,)), ((), ())),
                           preferred_element_type=F32)


def _iota(shape, dim):
    return lax.broadcasted_iota(jnp.int32, shape, dim)


def _sigmoid(x):
    return 1.0 / (1.0 + jnp.exp(-x))


def _silu(x):
    return x * _sigmoid(x)


def _softplus(x):
    return jnp.maximum(x, 0.0) + jnp.log(1.0 + jnp.exp(-jnp.abs(x)))


def _log_sigmoid(x):
    return -_softplus(-x)


QUAD = 4 * HEAD


def _quad_masks():
    t = _iota((CHUNK, QUAD), 0)
    lane = _iota((CHUNK, QUAD), 1)
    s = lane & (CHUNK - 1)
    fwd = lane < 2 * HEAD
    bwd = jnp.logical_not(fwd)
    strict = (fwd & (s < t)) | (bwd & (s > t))
    incl = strict | (s == t)
    return incl, strict, t, s


def _same_head():
    return (_iota((QUAD, QUAD), 0) >> 6) == (_iota((QUAD, QUAD), 1) >> 6)


def _qstack(a):
    a = a.astype(BF16)
    grp = _iota(a.shape, 1) >> 6
    return jnp.concatenate([jnp.where(grp == v, a, jnp.zeros_like(a)) for v in range(4)], axis=0)


def _inv_unit_tri_many(ls, t, s):
    same_blk = lambda width: _shr(t, width) == _shr(s, width)
    mul = lambda a, b: _mm(a, _qstack(b))
    n0 = [jnp.where(same_blk(8), -l, 0.0) for l in ls]
    p1 = [mul(x, x) for x in n0]
    inv = [jnp.where(s == t, 1.0, 0.0) + x for x in n0]
    p1_stacked = [_qstack(x) for x in p1]
    p2 = [_mm(x, xs) for x, xs in zip(p1, p1_stacked)]
    inv = [x + _mm(x, ps) for x, ps in zip(inv, p1_stacked)]
    inv = [x + mul(x, p) for x, p in zip(inv, p2)]
    for width in (8, 16, 32):
        blk = same_blk(2 * width) & jnp.logical_not(same_blk(width))
        half = [mul(x, jnp.where(blk, l, 0.0)) for x, l in zip(inv, ls)]
        inv = [x - mul(h, x) for x, h in zip(inv, half)]
    return inv


def _emit_pair_states(st_ref, sfin_ref, n_sub):
    for sub in range(n_sub):
        for j in range(3):
            for d in range(2):
                for hh in range(2):
                    g = 2 * d + hh
                    sfin_ref[sub, d, 2 * j + hh] = st_ref[sub, j, g * HEAD:(g + 1) * HEAD, g * HEAD:(g + 1) * HEAD]


def _split2(x):
    hi = x.astype(BF16)
    return hi, (x - hi.astype(F32)).astype(BF16)


def _mm_split(x, w):
    x_hi, x_lo = _split2(x)
    w_hi, w_lo = _split2(w)
    rows = x.shape[0]
    top = jnp.dot(jnp.concatenate([x_hi, x_lo], axis=0), w_hi, preferred_element_type=F32)
    return top[:rows] + top[rows:] + jnp.dot(x_hi, w_lo, preferred_element_type=F32)


def _head_sums(xs, bm):
    n_tiles = xs[0].shape[1] // LANES
    tiles = [x[:, j * LANES:(j + 1) * LANES] for x in xs for j in range(n_tiles)]
    parts = [p for tl in tiles for p in _split2(tl)]
    out = jnp.dot(jnp.concatenate(parts, axis=0), bm, preferred_element_type=F32)
    res = []
    for i in range(len(xs)):
        cols = []
        for j in range(n_tiles):
            base = (i * n_tiles + j) * 2 * CHUNK
            cols.append(out[base:base + CHUNK] + out[base + CHUNK:base + 2 * CHUNK])
        res.append(jnp.concatenate(cols, axis=1))
    return res


def _cumsum_rows(x, d):
    row = _iota(x.shape, 0)
    k = 1
    while k < CHUNK:
        if d == 0:
            x = x + jnp.where(row >= k, pltpu.roll(x, k, 0), 0.0)
        else:
            x = x + jnp.where(row < CHUNK - k, pltpu.roll(x, CHUNK - k, 0), 0.0)
        k *= 2
    return x


def _shr(x, width):
    return x >> (width.bit_length() - 1)


def _block_ones(n, width):
    r = _iota((n, n), 0)
    c = _iota((n, n), 1)
    return jnp.where(_shr(r, width) == _shr(c, width), 1.0, 0.0).astype(F32)


def _shifted(ref, off, start, c, n_chunks, t_len):
    pc = ref[pl.ds(off + start, CHUNK), :]
    pb = ref[pl.ds(pl.multiple_of(off + jnp.maximum(start - 8, 0), 8), 8), :]
    nb = ref[pl.ds(pl.multiple_of(off + jnp.minimum(start + CHUNK, t_len - 8), 8), 8), :]
    carry_p = jnp.where(c > 0, pb[7:8, :], 0.0)
    carry_n = jnp.where(c < n_chunks - 1, nb[0:1, :], 0.0)
    row = _iota(pc.shape, 0)
    prev = jnp.where(row == 0, carry_p, pltpu.roll(pc, 1, 0))
    nxt = jnp.where(row == CHUNK - 1, carry_n, pltpu.roll(pc, CHUNK - 1, 0))
    return pc, prev, nxt


def _ada_kernel(c_ref, w_ref, b_ref, o_ref):
    c = c_ref[...]
    o_ref[0] = _mm(_silu(c), w_ref[0]) + b_ref[0]


def _ada_call(cond8, ada_w, ada_b):
    nl = ada_w.shape[0]
    return pl.pallas_call(
        _ada_kernel,
        grid=(nl, 6),
        in_specs=[pl.BlockSpec((8, D_MODEL), lambda l, j: (0, 0)),
                  pl.BlockSpec((1, D_MODEL, D_MODEL), lambda l, j: (l, 0, j)),
                  pl.BlockSpec((1, 1, D_MODEL), lambda l, j: (l, 0, j))],
        out_specs=pl.BlockSpec((1, 8, D_MODEL), lambda l, j: (l, 0, j)),
        out_shape=jax.ShapeDtypeStruct((nl, 8, 6 * D_MODEL), F32),
        compiler_params=pltpu.CompilerParams(dimension_semantics=("arbitrary", "arbitrary"),
                                             vmem_limit_bytes=VMEM_LIMIT),
        name="ada_mod",
    )(cond8, ada_w, ada_b.reshape(nl, 1, 6 * D_MODEL))


def _cond_row(i, tile, n_ctx_tok, lat_t):
    tok = i * tile
    return jnp.where(tok < n_ctx_tok, 0, 1 + (tok - n_ctx_tok) // lat_t)


def _in_kernel(x_ref, sh_ref, sc_ref, nw_ref, w_ref, o_ref):
    x = x_ref[...]
    y = x * lax.rsqrt(jnp.mean(x * x, axis=-1, keepdims=True) + RMS_EPS) * nw_ref[...]
    h = y * (1.0 + sc_ref[0]) + sh_ref[0]
    o_ref[...] = jnp.dot(h.astype(BF16), w_ref[0], preferred_element_type=F32)


def _in_call(x, mod3, layer, nw, w_in_p, n_ctx_tok, lat_t, tile=256):
    n = x.shape[0]
    row = lambda i: layer * 8 + _cond_row(i, tile, n_ctx_tok, lat_t)
    return pl.pallas_call(
        _in_kernel,
        grid=(n // tile,),
        in_specs=[pl.BlockSpec((tile, D_MODEL), lambda i: (i, 0)),
                  pl.BlockSpec((1, 1, D_MODEL), lambda i: (row(i), 0, 0)),
                  pl.BlockSpec((1, 1, D_MODEL), lambda i: (row(i), 0, 1)),
                  pl.BlockSpec((1, D_MODEL), lambda i: (0, 0)),
                  pl.BlockSpec((1, D_MODEL, P_COLS), lambda i: (layer, 0, 0))],
        out_specs=pl.BlockSpec((tile, P_COLS), lambda i: (i, 0)),
        out_shape=jax.ShapeDtypeStruct((n, P_COLS), F32),
        compiler_params=pltpu.CompilerParams(dimension_semantics=("arbitrary",), vmem_limit_bytes=VMEM_LIMIT),
        name="in_proj",
    )(x, mod3, mod3, nw.reshape(1, D_MODEL), w_in_p)


EXP_M05 = math.exp(-0.5)


def _rwkv_kernel(*refs, t_len, n_sub, has_init):
    if has_init:
        (r_ref, k_ref, v_ref, x3_ref, mu_ref, vec_ref, w2_ref, a2_ref, g2_ref, s0_ref,
         y_ref, sfin_ref, xs_ref, yd_ref, st_ref) = refs
    else:
        (r_ref, k_ref, v_ref, x3_ref, mu_ref, vec_ref, w2_ref, a2_ref, g2_ref,
         y_ref, sfin_ref, xs_ref, yd_ref, st_ref) = refs
    n_chunks = t_len // CHUNK
    vec = vec_ref[...]
    bm = _block_ones(LANES, HEAD).astype(BF16)

    def shift_body(c, carry):
        start = pl.multiple_of(c * CHUNK, CHUNK)
        for sub in range(n_sub):
            for idx, ref in enumerate((r_ref, k_ref, v_ref, x3_ref)):
                pc, prev, nxt = _shifted(ref, sub * t_len, start, c, n_chunks, t_len)
                mu0 = mu_ref[0:1, idx * PAIRS_W:(idx + 1) * PAIRS_W]
                mu1 = mu_ref[1:2, idx * PAIRS_W:(idx + 1) * PAIRS_W]
                xs_ref[idx, pl.ds(sub * t_len + start, CHUNK), :] = pc + mu0 * (prev - pc) + mu1 * (nxt - pc)
        return carry

    lax.fori_loop(0, n_chunks, shift_body, 0)

    if has_init:
        st_ref[...] = s0_ref[...]
    else:
        st_ref[...] = jnp.zeros_like(st_ref)

    def prep(d, start):
        k = xs_ref[1, pl.ds(start, CHUNK), :]
        x3 = xs_ref[3, pl.ds(start, CHUNK), :]
        xw, xa = x3[:, 0:LANES], x3[:, LANES:2 * LANES]
        w_raw = vec[d:d + 1] + _mm(jnp.tanh(xw), w2_ref[d])
        lw = -EXP_M05 * _sigmoid(w_raw)
        a = _sigmoid(vec[2 + d:3 + d] + _mm(xa, a2_ref[d]))
        kmod = k * (1.0 + (a - 1.0) * vec[5:6])
        cum = _cumsum_rows(lw, d)
        half = 0.5 * jnp.sum(lw, axis=0, keepdims=True)
        return k * vec[4:5], a, kmod, cum, lw, half

    def seq_step(sub, i):
        off = sub * t_len
        starts = (pl.multiple_of(off + i * CHUNK, CHUNK), pl.multiple_of(off + (n_chunks - 1 - i) * CHUNK, CHUNK))
        pre = [prep(d, starts[d]) for d in range(2)]
        norms = _head_sums([pre[0][0] * pre[0][0], pre[1][0] * pre[1][0]], bm)
        fac = []
        for d in range(2):
            kk, a, kmod, cum, lw, half = pre[d]
            r = xs_ref[0, pl.ds(starts[d], CHUNK), :]
            kk = kk * lax.rsqrt(norms[d] + L2_EPS)
            em = jnp.exp(half)
            e_up = jnp.exp(half - cum)
            rh = r * jnp.exp(cum - half)
            ch = kk * jnp.exp(cum - lw - half)
            bh = kk * a * e_up
            kh = kmod * e_up
            v = xs_ref[2, pl.ds(starts[d], CHUNK), :]
            fac.append(dict(rh=rh, ch=ch, bh=bh, kh=kh, c0=ch * em, r0=rh * em, bt=bh * em, kt=kh * em, v=v,
                            dec=jnp.broadcast_to(em * em, (CHUNK, PAIRS_W))))
        return starts, fac

    def seq_body(i, carry):
        subs = [seq_step(sub, i) for sub in range(n_sub)]
        chains = [(sub, j) for sub in range(n_sub) for j in range(3)]

        def quad(c, name):
            fac, ln = subs[c[0]][1], slice(c[1] * LANES, (c[1] + 1) * LANES)
            return jnp.concatenate([fac[0][name][:, ln], fac[1][name][:, ln]], axis=1)

        incl, strict, t_idx, s_idx = _quad_masks()
        same_head = _same_head()
        v = [quad(c, "v") for c in chains]
        g = [_mm_nt(jnp.concatenate([quad(c, "ch"), quad(c, "rh")], axis=0),
                    jnp.concatenate([_qstack(quad(c, "bh")), _qstack(quad(c, "kh"))], axis=0))
             for c in chains]
        l_cb = [jnp.where(strict, x[:CHUNK, :QUAD], 0.0) for x in g]
        l_ck = [jnp.where(strict, x[:CHUNK, QUAD:], 0.0) for x in g]
        m_rb = [jnp.where(incl, x[CHUNK:, :QUAD], 0.0) for x in g]
        m_rk = [jnp.where(incl, x[CHUNK:, QUAD:], 0.0) for x in g]
        lv_y0 = [_mm(jnp.concatenate([a, b], axis=0), _qstack(x)) for a, b, x in zip(l_ck, m_rk, v)]
        t2 = _inv_unit_tri_many(l_cb, t_idx, s_idx)
        wu = [_mm(t, jnp.concatenate([_qstack(quad(c, "c0")), _qstack(x[:CHUNK])], axis=1))
              for t, c, x in zip(t2, chains, lv_y0)]
        st = [st_ref[c[0], c[1]] for c in chains]
        wr = [_mm_nt(jnp.concatenate([x[:, :QUAD], quad(c, "r0")], axis=0), s)
              for x, c, s in zip(wu, chains, st)]
        u = [-x[:, QUAD:] - y[:CHUNK] for x, y in zip(wu, wr)]
        mu_ = [_mm(m, _qstack(x)) for m, x in zip(m_rb, u)]
        upd = [_mm_tn(jnp.concatenate([x, vv], axis=0), jnp.concatenate([quad(c, "bt"), quad(c, "kt")], axis=0))
               for x, vv, c in zip(u, v, chains)]
        ys = [a[CHUNK:] + b[CHUNK:] + m for a, b, m in zip(lv_y0, wr, mu_)]
        for c, s, x in zip(chains, st, upd):
            st_ref[c[0], c[1]] = s * quad(c, "dec")[0:1] + jnp.where(same_head, x, 0.0)
        for sub in range(n_sub):
            for d in range(2):
                yd_ref[d, pl.ds(subs[sub][0][d], CHUNK), :] = jnp.concatenate(
                    [ys[sub * 3 + j][:, d * LANES:(d + 1) * LANES] for j in range(3)], axis=1)
        return carry

    lax.fori_loop(0, n_chunks, seq_body, 0)
    _emit_pair_states(st_ref, sfin_ref, n_sub)

    def post_body(c, carry):
        starts = [pl.multiple_of(sub * t_len + c * CHUNK, CHUNK) for sub in range(n_sub)]
        r = [xs_ref[0, pl.ds(s, CHUNK), :] for s in starts]
        k = [xs_ref[1, pl.ds(s, CHUNK), :] for s in starts]
        y = [yd_ref[0, pl.ds(s, CHUNK), :] + yd_ref[1, pl.ds(s, CHUNK), :] for s in starts]
        sums = _head_sums(y + [a * b * vec[6:7] for a, b in zip(r, k)], bm)
        yc = [a - m * (1.0 / HEAD) for a, m in zip(y, sums[:n_sub])]
        var = _head_sums([a * a for a in yc], bm)
        xg = jnp.concatenate([xs_ref[3, pl.ds(s, CHUNK), 2 * LANES:3 * LANES] for s in starts], axis=0)
        g = _mm(_sigmoid(xg), g2_ref[...])
        for i, s in enumerate(starts):
            yn = yc[i] * lax.rsqrt(var[i] * (1.0 / HEAD) + RW_LN_EPS) * vec[7:8] + vec[8:9]
            v = xs_ref[2, pl.ds(s, CHUNK), :]
            y_ref[pl.ds(s, CHUNK), :] = (yn + sums[n_sub + i] * v) * g[i * CHUNK:(i + 1) * CHUNK]
        return carry

    lax.fori_loop(0, n_chunks, post_body, 0)


def _rwkv_call(p, base_blk, n_seq, t_len, prm, s0):
    has_init = s0 is not None
    n_sub = _seqs_per_step(n_seq, t_len)
    assert n_seq % n_sub == 0 and base_blk % n_sub == 0
    rows = n_sub * t_len

    def pspec(col_blk):
        return pl.BlockSpec((rows, PAIRS_W), lambda b: (base_blk // n_sub + b, col_blk),
                            pipeline_mode=_seq_buffering(rows))

    full = lambda shape: pl.BlockSpec(shape, lambda b: (0,) * len(shape))
    in_specs = [pspec(0), pspec(1), pspec(2), pspec(3),
                full((2, RW_COLS)), full((16, PAIRS_W)), full((2, LANES, PAIRS_W)), full((2, LANES, PAIRS_W)),
                full((LANES, PAIRS_W))]
    args = [p, p, p, p, prm["mu"], prm["vec"], prm["w2"], prm["a2"], prm["g2"]]
    if has_init:
        in_specs.append(pl.BlockSpec((n_sub, 3, QUAD, QUAD), lambda b: (b, 0, 0, 0)))
        args.append(s0)
    y, sfin = pl.pallas_call(
        functools.partial(_rwkv_kernel, t_len=t_len, n_sub=n_sub, has_init=has_init),
        grid=(n_seq // n_sub,),
        in_specs=in_specs,
        out_specs=[pl.BlockSpec((rows, PAIRS_W), lambda b: (b, 0)),
                   pl.BlockSpec((n_sub, 2, 2 * 3, HEAD, HEAD), lambda b: (b, 0, 0, 0, 0))],
        out_shape=[jax.ShapeDtypeStruct((n_seq * t_len, RW_DIM), F32),
                   jax.ShapeDtypeStruct((n_seq, 2, 2 * 3, HEAD, HEAD), F32)],
        scratch_shapes=[pltpu.VMEM((4, rows, PAIRS_W), F32),
                        pltpu.VMEM((2, rows, PAIRS_W), F32),
                        pltpu.VMEM((n_sub, 3, QUAD, QUAD), F32)],
        compiler_params=pltpu.CompilerParams(dimension_semantics=("arbitrary",), vmem_limit_bytes=VMEM_LIMIT),
        name="rwkv7_mix",
    )(*args)
    return y, sfin


def _gdn_kernel(*refs, t_len, n_sub, has_init):
    if has_init:
        (q_ref, k_ref, v_ref, z_ref, gt_ref, cw_ref, vec_ref, gvec_ref, s0_ref,
         y_ref, sfin_ref, xs_ref, yd_ref, st_ref) = refs
    else:
        (q_ref, k_ref, v_ref, z_ref, gt_ref, cw_ref, vec_ref, gvec_ref,
         y_ref, sfin_ref, xs_ref, yd_ref, st_ref) = refs
    n_chunks = t_len // CHUNK
    vec = vec_ref[...]
    gvec = gvec_ref[...]
    bm = _block_ones(LANES, HEAD).astype(BF16)

    def conv_body(c, carry):
        start = pl.multiple_of(c * CHUNK, CHUNK)
        for sub in range(n_sub):
            for idx, ref in enumerate((q_ref, k_ref, v_ref)):
                pc, prev, nxt = _shifted(ref, sub * t_len, start, c, n_chunks, t_len)
                cw = cw_ref[:, idx * PAIRS_W:(idx + 1) * PAIRS_W]
                xs_ref[idx, pl.ds(sub * t_len + start, CHUNK), :] = _silu(
                    cw[0:1] * prev + cw[1:2] * pc + cw[2:3] * nxt)
        return carry

    lax.fori_loop(0, n_chunks, conv_body, 0)

    if has_init:
        st_ref[...] = s0_ref[...]
    else:
        st_ref[...] = jnp.zeros_like(st_ref)

    def lane_bcast(tile, lane, width):
        return jnp.broadcast_to(tile[:, lane:lane + 1], (tile.shape[0], width))

    def per_head(tile, first_lane):
        return jnp.concatenate([lane_bcast(tile, first_lane + h, HEAD) for h in range(DN_HEADS)], axis=1)

    def seq_step(sub, i):
        off = sub * t_len
        starts = (pl.multiple_of(off + i * CHUNK, CHUNK), pl.multiple_of(off + (n_chunks - 1 - i) * CHUNK, CHUNK))
        qs = [xs_ref[0, pl.ds(starts[d], CHUNK), :] for d in range(2)]
        ks = [xs_ref[1, pl.ds(starts[d], CHUNK), :] for d in range(2)]
        norms = _head_sums([qs[0] * qs[0], ks[0] * ks[0], qs[1] * qs[1], ks[1] * ks[1]], bm)
        fac, gam_tiles = [], []
        for d in range(2):
            gt = gt_ref[pl.ds(starts[d], CHUNK), :]
            q = qs[d] * lax.rsqrt(norms[2 * d] + L2_EPS) * (HEAD ** -0.5)
            k = ks[d] * lax.rsqrt(norms[2 * d + 1] + L2_EPS)
            v = xs_ref[2, pl.ds(starts[d], CHUNK), :]
            g_tile = -jnp.exp(gvec[0:1]) * _softplus(gt + gvec[1:2])
            gam_tile = _cumsum_rows(g_tile, d)
            gam = per_head(gam_tile, 6 * d)
            beta = per_head(_sigmoid(gt), 12 + 6 * d)
            last = gam[CHUNK - 1:CHUNK] if d == 0 else gam[0:1]
            eg = jnp.exp(gam)
            kb = k * beta
            fac.append(dict(q=q, k=k, kb=kb, vb=v * beta, kbe=kb * eg, qe=q * eg, ko=k * jnp.exp(last - gam), gam=gam,
                            el=jnp.broadcast_to(jnp.exp(last), (CHUNK, PAIRS_W))))
            gam_tiles.append(gam_tile)
        gam_rows = [jnp.transpose(gam_tiles[d]) for d in range(2)]
        incl = _quad_masks()[0]
        pairs = []
        for j in range(3):
            ln = slice(j * LANES, (j + 1) * LANES)
            gam_c = jnp.concatenate([fac[0]["gam"][:, ln], fac[1]["gam"][:, ln]], axis=1)
            gam_r = jnp.concatenate([gam_rows[d][6 * d + 2 * j + hh:6 * d + 2 * j + hh + 1, :]
                                     for d in range(2) for hh in range(2)], axis=1)
            pairs.append(jnp.exp(jnp.where(incl, gam_c - gam_r, -jnp.inf)))
        return starts, fac, pairs

    def seq_body(i, carry):
        subs = [seq_step(sub, i) for sub in range(n_sub)]
        chains = [(sub, j) for sub in range(n_sub) for j in range(3)]

        def quad(c, name):
            fac, ln = subs[c[0]][1], slice(c[1] * LANES, (c[1] + 1) * LANES)
            return jnp.concatenate([fac[0][name][:, ln], fac[1][name][:, ln]], axis=1)

        incl, strict, t_idx, s_idx = _quad_masks()
        same_head = _same_head()
        pair = [subs[c[0]][2][c[1]] for c in chains]
        g = [_mm_nt(jnp.concatenate([quad(c, "kb"), quad(c, "q")], axis=0), _qstack(quad(c, "k")))
             for c in chains]
        a2 = [jnp.where(strict, x[:CHUNK] * p, 0.0) for x, p in zip(g, pair)]
        qk = [x[CHUNK:] * p for x, p in zip(g, pair)]
        t2 = _inv_unit_tri_many(a2, t_idx, s_idx)
        uw = [_mm(t, jnp.concatenate([_qstack(quad(c, "vb")), _qstack(quad(c, "kbe"))], axis=1))
              for t, c in zip(t2, chains)]
        st = [st_ref[c[0], c[1]] for c in chains]
        wq = [_mm_nt(jnp.concatenate([x[:, QUAD:], quad(c, "qe")], axis=0), s)
              for x, c, s in zip(uw, chains, st)]
        vn = [x[:, :QUAD] - y[:CHUNK] for x, y in zip(uw, wq)]
        qv = [_mm(m, _qstack(x)) for m, x in zip(qk, vn)]
        upd = [_mm_tn(x, quad(c, "ko")) for x, c in zip(vn, chains)]
        os_ = [y[CHUNK:] + m for y, m in zip(wq, qv)]
        for c, s, x in zip(chains, st, upd):
            st_ref[c[0], c[1]] = s * quad(c, "el")[0:1] + jnp.where(same_head, x, 0.0)
        for sub in range(n_sub):
            for d in range(2):
                yd_ref[d, pl.ds(subs[sub][0][d], CHUNK), :] = jnp.concatenate(
                    [os_[sub * 3 + j][:, d * LANES:(d + 1) * LANES] for j in range(3)], axis=1)
        return carry

    lax.fori_loop(0, n_chunks, seq_body, 0)
    _emit_pair_states(st_ref, sfin_ref, n_sub)

    def post_body(c, carry):
        starts = [pl.multiple_of(sub * t_len + c * CHUNK, CHUNK) for sub in range(n_sub)]
        o = [yd_ref[0, pl.ds(s, CHUNK), :] + yd_ref[1, pl.ds(s, CHUNK), :] for s in starts]
        ms = _head_sums([a * a for a in o], bm)
        for i, s in enumerate(starts):
            z = z_ref[pl.ds(s, CHUNK), :]
            y_ref[pl.ds(s, CHUNK), :] = o[i] * lax.rsqrt(ms[i] * (1.0 / HEAD) + RMS_EPS) * vec[0:1] * _silu(z)
        return carry

    lax.fori_loop(0, n_chunks, post_body, 0)


def _gdn_call(p, base_blk, n_seq, t_len, prm, s0):
    has_init = s0 is not None
    dn_blk = DN_OFF // PAIRS_W
    n_sub = _seqs_per_step(n_seq, t_len)
    assert n_seq % n_sub == 0 and base_blk % n_sub == 0
    rows = n_sub * t_len

    def pspec(col_blk):
        return pl.BlockSpec((rows, PAIRS_W), lambda b: (base_blk // n_sub + b, dn_blk + col_blk),
                            pipeline_mode=_seq_buffering(rows))

    full = lambda shape: pl.BlockSpec(shape, lambda b: (0,) * len(shape))
    in_specs = [pspec(0), pspec(1), pspec(2), pspec(3),
                pl.BlockSpec((rows, LANES), lambda b: (base_blk // n_sub + b, DN_GATE_OFF // LANES)),
                full((8, 3 * PAIRS_W)), full((8, PAIRS_W)), full((8, LANES))]
    args = [p, p, p, p, p, prm["cw"], prm["vec"], prm["gvec"]]
    if has_init:
        in_specs.append(pl.BlockSpec((n_sub, 3, QUAD, QUAD), lambda b: (b, 0, 0, 0)))
        args.append(s0)
    y, sfin = pl.pallas_call(
        functools.partial(_gdn_kernel, t_len=t_len, n_sub=n_sub, has_init=has_init),
        grid=(n_seq // n_sub,),
        in_specs=in_specs,
        out_specs=[pl.BlockSpec((rows, PAIRS_W), lambda b: (b, 0)),
                   pl.BlockSpec((n_sub, 2, 2 * 3, HEAD, HEAD), lambda b: (b, 0, 0, 0, 0))],
        out_shape=[jax.ShapeDtypeStruct((n_seq * t_len, DN_DIM), F32),
                   jax.ShapeDtypeStruct((n_seq, 2, 2 * 3, HEAD, HEAD), F32)],
        scratch_shapes=[pltpu.VMEM((3, rows, PAIRS_W), F32),
                        pltpu.VMEM((2, rows, PAIRS_W), F32),
                        pltpu.VMEM((n_sub, 3, QUAD, QUAD), F32)],
        compiler_params=pltpu.CompilerParams(dimension_semantics=("arbitrary",), vmem_limit_bytes=VMEM_LIMIT),
        name="gdn_mix",
    )(*args)
    return y, sfin


def _gla_kernel(*refs, t_len, n_sub, has_init):
    if has_init:
        (q_ref, k_ref, v_ref, og_ref, gt_ref, gk2_ref, vec_ref, s0_ref,
         y_ref, sfin_ref, yd_ref, st_ref) = refs
    else:
        (q_ref, k_ref, v_ref, og_ref, gt_ref, gk2_ref, vec_ref,
         y_ref, sfin_ref, yd_ref, st_ref) = refs
    n_chunks = t_len // CHUNK
    vec = vec_ref[...]
    sr = _iota((GLA_DIM, GLA_KDIM), 0)
    sc = _iota((GLA_DIM, GLA_KDIM), 1)
    st_mask = _shr(sr, GLA_DV) == _shr(sc, GLA_DK)

    if has_init:
        st_ref[...] = s0_ref[...]
    else:
        st_ref[...] = jnp.zeros_like(st_ref)

    def seq_body(i, carry):
        chains = [(sub, d) for sub in range(n_sub) for d in range(2)]
        starts = [pl.multiple_of(sub * t_len + (i if d == 0 else n_chunks - 1 - i) * CHUNK, CHUNK)
                  for sub, d in chains]
        t_idx = _iota((CHUNK, GLA_DIM), 0)
        s_idx = _iota((CHUNK, GLA_DIM), 1) & (CHUNK - 1)
        k_lane = _shr(_iota((CHUNK, GLA_KDIM), 1), GLA_DK)
        q = [q_ref[pl.ds(s, CHUNK), :] * (GLA_DK ** -0.5) for s in starts]
        k = [k_ref[pl.ds(s, CHUNK), :] for s in starts]
        v = [v_ref[pl.ds(s, CHUNK), :] for s in starts]
        gk = [_log_sigmoid(_mm(gt_ref[pl.ds(s, CHUNK), :], gk2_ref[c[1]]) + vec[c[1]:c[1] + 1, :GLA_KDIM])
              * (1.0 / GLA_GATE_NORM) for s, c in zip(starts, chains)]
        b = [_cumsum_rows(x, c[1]) for x, c in zip(gk, chains)]
        tot = [jnp.sum(x, axis=0, keepdims=True) for x in gk]
        kt = [kk * jnp.exp(0.5 * tt - bb) for kk, tt, bb in zip(k, tot, b)]
        k_stack = [jnp.concatenate([jnp.where(k_lane == h, x, 0.0) for h in range(GLA_HEADS)], axis=0) for x in kt]
        scores = [_mm_nt(qq * jnp.exp(bb - 0.5 * tt), ks) for qq, bb, tt, ks in zip(q, b, tot, k_stack)]
        scores = [jnp.where((s_idx <= t_idx) if c[1] == 0 else (s_idx >= t_idx), x, 0.0)
                  for x, c in zip(scores, chains)]
        st = [st_ref[c[0], c[1]] for c in chains]
        o_intra = [_mm(x, _qstack(vv)) for x, vv in zip(scores, v)]
        o_inter = [_mm_nt(qq * jnp.exp(bb), s) for qq, bb, s in zip(q, b, st)]
        upd = [_mm_tn(vv, kk * jnp.exp(tt - bb)) for vv, kk, tt, bb in zip(v, k, tot, b)]
        for idx, c in enumerate(chains):
            st_ref[c[0], c[1]] = st[idx] * jnp.exp(tot[idx]) + jnp.where(st_mask, upd[idx], 0.0)
            yd_ref[c[1], pl.ds(starts[idx], CHUNK), :] = o_intra[idx] + o_inter[idx]
        return carry

    lax.fori_loop(0, n_chunks, seq_body, 0)
    for sub in range(n_sub):
        for d in range(2):
            for h in range(GLA_HEADS):
                sfin_ref[sub, d, h] = st_ref[sub, d, h * GLA_DV:(h + 1) * GLA_DV, h * GLA_DK:(h + 1) * GLA_DK]

    bm = _block_ones(LANES, GLA_DV).astype(BF16)

    def post_body(c, carry):
        start = pl.multiple_of(c * CHUNK, CHUNK)
        o = yd_ref[0, pl.ds(start, CHUNK), :] + yd_ref[1, pl.ds(start, CHUNK), :]
        og = og_ref[pl.ds(start, CHUNK), :]
        ms = _head_sums([o * o], bm)[0] * (1.0 / GLA_DV)
        y_ref[pl.ds(start, CHUNK), :] = o * lax.rsqrt(ms + RMS_EPS) * vec[2:3] * _silu(og)
        return carry

    lax.fori_loop(0, n_sub * n_chunks, post_body, 0)


def _gla_call(p, base_blk, n_seq, t_len, prm, s0):
    has_init = s0 is not None
    t128 = GLA_OFF // LANES
    t256 = GLA_OFF // GLA_DIM
    n_sub = _seqs_per_step(n_seq, t_len)
    assert n_seq % n_sub == 0 and base_blk % n_sub == 0
    rows = n_sub * t_len
    blk = lambda b: base_blk // n_sub + b
    in_specs = [pl.BlockSpec((rows, LANES), lambda b: (blk(b), t128)),
                pl.BlockSpec((rows, LANES), lambda b: (blk(b), t128 + 1)),
                pl.BlockSpec((rows, GLA_DIM), lambda b: (blk(b), t256 + 1)),
                pl.BlockSpec((rows, GLA_DIM), lambda b: (blk(b), t256 + 2)),
                pl.BlockSpec((rows, LANES), lambda b: (blk(b), t128 + 6)),
                pl.BlockSpec((2, LANES, LANES), lambda b: (0, 0, 0)),
                pl.BlockSpec((8, GLA_DIM), lambda b: (0, 0))]
    args = [p, p, p, p, p, prm["gk2"], prm["vec"]]
    if has_init:
        in_specs.append(pl.BlockSpec((n_sub, 2, GLA_DIM, GLA_KDIM), lambda b: (b, 0, 0, 0)))
        args.append(s0)
    y, sfin = pl.pallas_call(
        functools.partial(_gla_kernel, t_len=t_len, n_sub=n_sub, has_init=has_init),
        grid=(n_seq // n_sub,),
        in_specs=in_specs,
        out_specs=[pl.BlockSpec((rows, GLA_DIM), lambda b: (b, 0)),
                   pl.BlockSpec((n_sub, 2, GLA_HEADS, GLA_DV, GLA_DK), lambda b: (b, 0, 0, 0, 0))],
        out_shape=[jax.ShapeDtypeStruct((n_seq * t_len, GLA_DIM), F32),
                   jax.ShapeDtypeStruct((n_seq, 2, GLA_HEADS, GLA_DV, GLA_DK), F32)],
        scratch_shapes=[pltpu.VMEM((2, rows, GLA_DIM), F32),
                        pltpu.VMEM((n_sub, 2, GLA_DIM, GLA_KDIM), F32)],
        compiler_params=pltpu.CompilerParams(dimension_semantics=("arbitrary",), vmem_limit_bytes=VMEM_LIMIT),
        name="gla_mix",
    )(*args)
    return y, sfin


def _out_kernel(yr_ref, yg_ref, yd_ref, x_ref, g1_ref, sh_ref, sc_ref, nw_ref, wo_ref, wr_ref, br_ref,
                x1_ref, h2_ref, grp_ref):
    mix = (_mm(yr_ref[...], wo_ref[0, 0:RW_DIM, :]) + _mm(yg_ref[...], wo_ref[0, RW_DIM:RW_DIM + GLA_DIM, :])
           + _mm(yd_ref[...], wo_ref[0, RW_DIM + GLA_DIM:, :]))
    x1 = x_ref[...] + g1_ref[0] * mix
    x1_ref[...] = x1
    h2 = x1 * lax.rsqrt(jnp.mean(x1 * x1, axis=-1, keepdims=True) + RMS_EPS) * nw_ref[...]
    h2 = h2 * (1.0 + sc_ref[0]) + sh_ref[0]
    for s in range(ROW_SLABS):
        h2_ref[:, s, :] = h2[:, s * LANES:(s + 1) * LANES]
    logits = _mm_split(h2, wr_ref[...]) + br_ref[...]
    lane_i = _iota(logits.shape, 1)
    lg = jnp.where(lane_i < N_GROUPS, logits, -jnp.inf)
    gmax = jnp.max(lg, axis=1, keepdims=True)
    gsel = jnp.min(jnp.where(lg == gmax, lane_i.astype(F32), 1e9), axis=1, keepdims=True)
    grp_ref[...] = jnp.broadcast_to(gsel, logits.shape).astype(jnp.int32)


def _out_call(y_rw, y_gla, y_dn, x, mod3, layer, nw, w_out, wr, br, n_ctx_tok, lat_t, tile=256):
    n = x.shape[0]
    row = lambda i: layer * 8 + _cond_row(i, tile, n_ctx_tok, lat_t)
    def modspec(part):
        return pl.BlockSpec((1, 1, D_MODEL), lambda i: (row(i), 0, part))

    return pl.pallas_call(
        _out_kernel,
        grid=(n // tile,),
        in_specs=[pl.BlockSpec((tile, RW_DIM), lambda i: (i, 0)),
                  pl.BlockSpec((tile, GLA_DIM), lambda i: (i, 0)),
                  pl.BlockSpec((tile, DN_DIM), lambda i: (i, 0)),
                  pl.BlockSpec((tile, D_MODEL), lambda i: (i, 0)),
                  modspec(2), modspec(3), modspec(4),
                  pl.BlockSpec((1, D_MODEL), lambda i: (0, 0)),
                  pl.BlockSpec((1, D_MODEL, D_MODEL), lambda i: (layer, 0, 0)),
                  pl.BlockSpec((D_MODEL, LANES), lambda i: (0, 0)),
                  pl.BlockSpec((1, LANES), lambda i: (0, 0))],
        out_specs=[pl.BlockSpec((tile, D_MODEL), lambda i: (i, 0)),
                   pl.BlockSpec((tile, ROW_SLABS, LANES), lambda i: (i, 0, 0)),
                   pl.BlockSpec((tile, LANES), lambda i: (i, 0))],
        out_shape=[jax.ShapeDtypeStruct((n, D_MODEL), F32),
                   jax.ShapeDtypeStruct((n, ROW_SLABS, LANES), F32),
                   jax.ShapeDtypeStruct((n, LANES), jnp.int32)],
        compiler_params=pltpu.CompilerParams(dimension_semantics=("arbitrary",), vmem_limit_bytes=VMEM_LIMIT),
        name="out_proj_router",
    )(y_rw, y_gla, y_dn, x, mod3, mod3, mod3, nw.reshape(1, D_MODEL), w_out, wr, br)


ROW_SLABS = D_MODEL // LANES
MOE_TILE = 512
DMA_TILE = 256


def _route_slots(grp, n_tok):
    gid = grp[:, 0]
    onehot = (gid[:, None] == jnp.arange(N_GROUPS, dtype=jnp.int32)[None, :]).astype(jnp.int32)
    csum = jnp.cumsum(onehot, axis=0)
    rank = jnp.sum(csum * onehot, axis=1) - 1
    counts = csum[-1]
    tiles_per = (counts + MOE_TILE - 1) // MOE_TILE
    tile_end = jnp.cumsum(tiles_per)
    offs = (tile_end - tiles_per) * MOE_TILE
    slot = jnp.sum(offs[None, :] * onehot, axis=1) + rank
    max_tiles = n_tok // MOE_TILE + N_GROUPS
    tile_group = jnp.sum((jnp.arange(max_tiles, dtype=jnp.int32)[:, None] >= tile_end[None, :]).astype(jnp.int32), axis=1)
    tile_group = jnp.minimum(tile_group, N_GROUPS - 1)
    return slot.astype(jnp.int32), tile_group.astype(jnp.int32), tile_end[-1:].astype(jnp.int32), max_tiles


def _row_maps(slot, n_tok, n_rows):
    tok = jnp.full((n_rows,), -1, jnp.int32).at[slot].set(jnp.arange(n_tok, dtype=jnp.int32))
    pad = tok < 0
    spare = n_tok + jnp.cumsum(pad.astype(jnp.int32)) - 1
    return jnp.where(pad, 0, tok), jnp.where(pad, spare, tok)


def _moe_group_kernel(tg_ref, nt_ref, src_ref, dst_ref, h_ref, init_ref, wr_ref, br_ref, wg_ref, wu_ref, wd_ref,
                      y_ref, xbuf_ref, obuf_ref, acc_ref, gate_ref, xb_ref, gsem, ssem):
    del init_ref
    i = pl.program_id(0)
    e = pl.program_id(1)
    n_tiles = nt_ref[0]
    buf = lax.rem(i, 2)

    def start_gather(tile, b):
        def body(r, carry):
            pltpu.make_async_copy(h_ref.at[src_ref[tile * MOE_TILE + r]], xbuf_ref.at[b, r], gsem.at[b]).start()
            return carry
        lax.fori_loop(0, MOE_TILE, body, 0, unroll=8)

    def wait_gather(b):
        pltpu.make_async_copy(h_ref.at[pl.ds(0, MOE_TILE)], xbuf_ref.at[b], gsem.at[b]).wait()

    def start_scatter(tile, b):
        def body(r, carry):
            pltpu.make_async_copy(obuf_ref.at[b, r], y_ref.at[dst_ref[tile * MOE_TILE + r]], ssem.at[b]).start()
            return carry
        lax.fori_loop(0, MOE_TILE, body, 0, unroll=8)

    def wait_scatter(b):
        pltpu.make_async_copy(obuf_ref.at[b], y_ref.at[pl.ds(0, MOE_TILE)], ssem.at[b]).wait()

    @pl.when(i < n_tiles)
    def _():
        @pl.when(e == 0)
        def _():
            @pl.when(i == 0)
            def _():
                start_gather(0, 0)

            wait_gather(buf)

            @pl.when(i + 1 < n_tiles)
            def _():
                start_gather(i + 1, 1 - buf)

            acc_ref[...] = jnp.zeros_like(acc_ref)
            x = jnp.concatenate([xbuf_ref[buf, :, s, :] for s in range(ROW_SLABS)], axis=1)
            xb_ref[...] = x.astype(BF16)
            logits = _mm_split(x, wr_ref[...]) + br_ref[...]
            lane_i = _iota(logits.shape, 1)
            lane = lane_i.astype(F32)
            grp = tg_ref[i]
            lg = jnp.where(lane_i < N_GROUPS, logits, -jnp.inf)
            gmax = jnp.max(lg, axis=1, keepdims=True)
            l_sel = jnp.sum(jnp.where(lane_i == grp, logits, 0.0), axis=1, keepdims=True)
            p_grp = jnp.exp(l_sel - gmax) / jnp.sum(jnp.exp(lg - gmax), axis=1, keepdims=True)
            first = N_GROUPS + grp * EXPERTS_PER_GROUP
            le = jnp.where((lane_i >= first) & (lane_i < first + EXPERTS_PER_GROUP), logits, -jnp.inf)
            v1 = jnp.max(le, axis=1, keepdims=True)
            i1 = jnp.min(jnp.where(le == v1, lane, 1e9), axis=1, keepdims=True)
            le2 = jnp.where(lane == i1, -jnp.inf, le)
            v2 = jnp.max(le2, axis=1, keepdims=True)
            i2 = jnp.min(jnp.where(le2 == v2, lane, 1e9), axis=1, keepdims=True)
            e2 = jnp.exp(v2 - v1)
            p1 = p_grp / (1.0 + e2)
            p2 = p_grp * e2 / (1.0 + e2)
            gate_ref[...] = jnp.where(lane == i1, p1, jnp.where(lane == i2, p2, 0.0))

        h = xb_ref[...]
        gates = gate_ref[...]
        lane_g = _iota(gates.shape, 1)
        gcol = jnp.sum(jnp.where(lane_g == N_GROUPS + tg_ref[i] * EXPERTS_PER_GROUP + e, gates, 0.0),
                       axis=1, keepdims=True)
        hid = _silu(jnp.dot(h, wg_ref[0].astype(BF16), preferred_element_type=F32))
        hid = hid * jnp.dot(h, wu_ref[0].astype(BF16), preferred_element_type=F32) * gcol
        acc_ref[...] += _mm(hid, wd_ref[0])

        @pl.when(e == EXPERTS_PER_GROUP - 1)
        def _():
            @pl.when(i >= 2)
            def _():
                wait_scatter(buf)

            acc = acc_ref[...]
            for s in range(ROW_SLABS):
                obuf_ref[buf, :, s, :] = acc[:, s * LANES:(s + 1) * LANES]
            start_scatter(i, buf)

            @pl.when(i == n_tiles - 1)
            def _():
                wait_scatter(buf)

                @pl.when(i >= 1)
                def _():
                    wait_scatter(1 - buf)


def _moe_group_call(tile_group, n_tiles, max_tiles, src_row, dst_row, h_rows, layer, wr, br, w_gate, w_up, w_down):
    n_rows = max_tiles * MOE_TILE
    tile_idx = lambda i, nt: jnp.minimum(i, nt[0] - 1)
    expert = lambda i, e, tg, nt: (layer * N_EXPERTS + tg[tile_idx(i, nt)] * EXPERTS_PER_GROUP
                                   + jnp.where(i < nt[0], e, EXPERTS_PER_GROUP - 1))
    wspec = lambda shape: pl.BlockSpec(shape, lambda i, e, tg, nt, sr, ds: (expert(i, e, tg, nt), 0, 0))
    init = jnp.zeros((n_rows, ROW_SLABS, LANES), F32)
    return pl.pallas_call(
        _moe_group_kernel,
        grid_spec=pltpu.PrefetchScalarGridSpec(
            num_scalar_prefetch=4,
            grid=(max_tiles, EXPERTS_PER_GROUP),
            in_specs=[pl.BlockSpec(memory_space=pl.ANY), pl.BlockSpec(memory_space=pl.ANY),
                      pl.BlockSpec((D_MODEL, LANES), lambda i, e, tg, nt, sr, ds: (0, 0)),
                      pl.BlockSpec((1, LANES), lambda i, e, tg, nt, sr, ds: (0, 0)),
                      wspec((1, D_MODEL, D_EXPERT)), wspec((1, D_MODEL, D_EXPERT)), wspec((1, D_EXPERT, D_MODEL))],
            out_specs=pl.BlockSpec(memory_space=pl.ANY),
            scratch_shapes=[pltpu.VMEM((2, MOE_TILE, ROW_SLABS, LANES), F32),
                            pltpu.VMEM((2, MOE_TILE, ROW_SLABS, LANES), F32),
                            pltpu.VMEM((MOE_TILE, D_MODEL), F32),
                            pltpu.VMEM((MOE_TILE, LANES), F32),
                            pltpu.VMEM((MOE_TILE, D_MODEL), BF16),
                            pltpu.SemaphoreType.DMA((2,)),
                            pltpu.SemaphoreType.DMA((2,))]),
        out_shape=jax.ShapeDtypeStruct((n_rows, ROW_SLABS, LANES), F32),
        input_output_aliases={5: 0},
        compiler_params=pltpu.CompilerParams(dimension_semantics=("arbitrary", "arbitrary"),
                                             vmem_limit_bytes=VMEM_LIMIT),
        name="moe_grouped",
    )(tile_group, n_tiles, src_row, dst_row, h_rows, init, wr, br, w_gate, w_up, w_down)


def _combine_kernel(y_ref, x_ref, g2_ref, o_ref):
    for s in range(ROW_SLABS):
        ln = slice(s * LANES, (s + 1) * LANES)
        o_ref[:, ln] = x_ref[:, ln] + g2_ref[0][:, ln] * y_ref[:, s, :]


def _combine_call(y_rows, x1, mod3, layer, n_ctx_tok, lat_t):
    n = x1.shape[0]
    row = lambda i: layer * 8 + _cond_row(i, DMA_TILE, n_ctx_tok, lat_t)
    return pl.pallas_call(
        _combine_kernel,
        grid=(n // DMA_TILE,),
        in_specs=[pl.BlockSpec((DMA_TILE, ROW_SLABS, LANES), lambda i: (i, 0, 0)),
                  pl.BlockSpec((DMA_TILE, D_MODEL), lambda i: (i, 0)),
                  pl.BlockSpec((1, 1, D_MODEL), lambda i: (row(i), 0, 5))],
        out_specs=pl.BlockSpec((DMA_TILE, D_MODEL), lambda i: (i, 0)),
        out_shape=jax.ShapeDtypeStruct((n, D_MODEL), F32),
        compiler_params=pltpu.CompilerParams(dimension_semantics=("arbitrary",), vmem_limit_bytes=VMEM_LIMIT),
        name="moe_combine",
    )(y_rows, x1, mod3)


def _norm_kernel(x_ref, w_ref, o_ref):
    x = x_ref[...]
    o_ref[...] = x * lax.rsqrt(jnp.mean(x * x, axis=-1, keepdims=True) + RMS_EPS) * w_ref[...]


def _norm_call(x, w, base_tile, n_tiles, tile=256):
    return pl.pallas_call(
        _norm_kernel,
        grid=(n_tiles,),
        in_specs=[pl.BlockSpec((tile, D_MODEL), lambda i: (base_tile + i, 0)),
                  pl.BlockSpec((1, D_MODEL), lambda i: (0, 0))],
        out_specs=pl.BlockSpec((tile, D_MODEL), lambda i: (i, 0)),
        out_shape=jax.ShapeDtypeStruct((n_tiles * tile, D_MODEL), F32),
        compiler_params=pltpu.CompilerParams(dimension_semantics=("arbitrary",), vmem_limit_bytes=VMEM_LIMIT),
        name="final_norm",
    )(x, w.reshape(1, D_MODEL))


def _pos_embed(n_tokens, dim):
    rows = n_tokens // GRID_W
    row = jnp.broadcast_to(jnp.arange(rows)[:, None], (rows, GRID_W)).reshape(-1)
    col = jnp.broadcast_to(jnp.arange(GRID_W)[None, :], (rows, GRID_W)).reshape(-1)
    quarter = dim // 4
    omega = 1.0 / (POS_BASE ** (jnp.arange(quarter, dtype=F32) / quarter))

    def axis_embed(pos):
        ang = pos.astype(F32)[:, None] * omega[None, :]
        return jnp.concatenate([jnp.sin(ang), jnp.cos(ang)], axis=-1)

    return jnp.concatenate([axis_embed(row), axis_embed(col)], axis=-1)


def _relayout_w_in(w_in_l):
    rw = w_in_l[:, :RW_COLS]
    g0 = RW_COLS
    q, k, v = w_in_l[:, g0:g0 + 128], w_in_l[:, g0 + 128:g0 + 256], w_in_l[:, g0 + 256:g0 + 512]
    gates = w_in_l[:, g0 + 512:g0 + 544]
    og = w_in_l[:, g0 + 544:g0 + 800]
    d0 = g0 + 800
    qkvz = w_in_l[:, d0:d0 + 1536]
    dgates = w_in_l[:, d0 + 1536:d0 + 1560]
    z = lambda n: jnp.zeros((D_MODEL, n), w_in_l.dtype)
    return jnp.concatenate([rw, qkvz, q, k, v, og, gates, z(96), dgates, z(104)], axis=1).astype(BF16)


def _rwkv_params(mu, w0, w2, a0, a2, g2, kk, ka, rk, ln_w, ln_b):
    rows = [w0[0], w0[1], a0[0], a0[1], kk, ka, rk, ln_w, ln_b]
    vec = jnp.stack(rows + [jnp.zeros_like(kk)] * 7)
    zero = jnp.zeros((64, RW_DIM), F32)
    w2c = jnp.stack([jnp.concatenate([w2[0], zero]), jnp.concatenate([zero, w2[1]])])
    a2c = jnp.stack([jnp.concatenate([a2[0], zero]), jnp.concatenate([zero, a2[1]])])
    return {"mu": mu, "vec": vec, "w2": w2c, "a2": a2c, "g2": g2}


def _gdn_params(conv, a_log, dt_bias, norm_w):
    cw = jnp.concatenate([conv, jnp.zeros((5, 3 * DN_DIM), F32)], axis=0)
    vec = jnp.zeros((8, DN_DIM), F32).at[0].set(jnp.tile(norm_w, DN_HEADS))
    gvec = jnp.zeros((8, LANES), F32)
    gvec = gvec.at[0, 0:12].set(a_log.reshape(12)).at[1, 0:12].set(dt_bias.reshape(12))
    return {"cw": cw, "vec": vec, "gvec": gvec}


def _gla_params(gk2, gk_b, norm_w):
    g = jnp.zeros((2, LANES, LANES), F32)
    g = g.at[0, 0:16].set(gk2[0]).at[1, 16:32].set(gk2[1])
    vec = jnp.zeros((8, GLA_DIM), F32)
    vec = vec.at[0, :GLA_KDIM].set(gk_b[0]).at[1, :GLA_KDIM].set(gk_b[1]).at[2].set(jnp.tile(norm_w, GLA_HEADS))
    return {"gk2": g, "vec": vec}


def _pair_state_in(s):
    b = s.shape[0]
    st = jnp.swapaxes(s, -1, -2).reshape(b, 2, 3, 2, HEAD, HEAD)
    st = jnp.transpose(st, (0, 2, 1, 3, 4, 5)).reshape(b, 3, 4, HEAD, HEAD)
    bd = jnp.einsum("bjgvk,gh->bjgvhk", st, jnp.eye(4, dtype=s.dtype))
    return bd.reshape(b, 3, QUAD, QUAD)


def _pair_state_out(st):
    return jnp.swapaxes(st, -1, -2)


def _gla_state_in(s):
    b = s.shape[0]
    st = jnp.swapaxes(s, -1, -2)
    eye = jnp.eye(GLA_HEADS, dtype=s.dtype)
    bd = jnp.einsum("bdhvk,hg->bdhvgk", st, eye)
    return bd.reshape(b, 2, GLA_DIM, GLA_KDIM)


def _gla_state_out(st):
    return jnp.swapaxes(st, -1, -2)


def kernel(x_prompt, x_sample, c, state_rwkv, state_gla, state_delta, c_ctx, ada_w, ada_b, norm1_w, norm2_w, final_norm_w, w_in, w_out, rwkv_mu, rwkv_w0, rwkv_w2, rwkv_a0, rwkv_a2, rwkv_g2, rwkv_kk, rwkv_ka, rwkv_rk, rwkv_ln_w, rwkv_ln_b, gla_gk2, gla_gk_b, gla_norm_w, dn_conv, dn_a_log, dn_dt_bias, dn_norm_w, moe_wg, moe_bg, moe_we, moe_be, moe_w_gate, moe_w_up, moe_w_down):
    n_ctx, t_ctx, d = x_prompt.shape
    n_lat, t_lat, _ = x_sample.shape
    depth = ada_w.shape[0]
    n_ctx_tok = n_ctx * t_ctx
    n_lat_tok = n_lat * t_lat
    assert d == D_MODEL and n_ctx_tok % 1024 == 0 and t_lat % 1024 == 0 and t_ctx % 256 == 0 and n_lat <= 7
    assert n_ctx_tok % t_lat == 0

    xs = x_sample + _pos_embed(t_lat, d)[None].astype(x_sample.dtype)
    x = jnp.concatenate([x_prompt.reshape(n_ctx_tok, d), xs.reshape(n_lat_tok, d)], axis=0)

    cond8 = jnp.zeros((8, d), F32).at[0].set(c_ctx).at[1:1 + n_lat].set(c)
    mod = _ada_call(cond8, ada_w, ada_b)
    mod3 = mod.reshape(depth * 8, 1, 6 * d)

    w_in_all = jax.vmap(_relayout_w_in)(w_in)
    w_out_all = w_out.astype(BF16)
    w_gate = moe_w_gate.reshape(depth * N_EXPERTS, d, D_EXPERT)
    w_up = moe_w_up.reshape(depth * N_EXPERTS, d, D_EXPERT)
    w_down = moe_w_down.reshape(depth * N_EXPERTS, D_EXPERT, d)

    lat_base = n_ctx_tok // t_lat
    rw_states, gla_states, dn_states = [], [], []
    for l in range(depth):
        p = _in_call(x, mod3, l, norm1_w[l], w_in_all, n_ctx_tok, t_lat)
        rw_p = _rwkv_params(rwkv_mu[l], rwkv_w0[l], rwkv_w2[l], rwkv_a0[l], rwkv_a2[l], rwkv_g2[l], rwkv_kk[l],
                            rwkv_ka[l], rwkv_rk[l], rwkv_ln_w[l], rwkv_ln_b[l])
        dn_p = _gdn_params(dn_conv[l], dn_a_log[l], dn_dt_bias[l], dn_norm_w[l])
        gla_p = _gla_params(gla_gk2[l], gla_gk_b[l], gla_norm_w[l])

        y_rw_c, s_rw = _rwkv_call(p, 0, n_ctx, t_ctx, rw_p, None)
        y_rw_l, _ = _rwkv_call(p, lat_base, n_lat, t_lat, rw_p, _pair_state_in(state_rwkv[:, l]))
        y_dn_c, s_dn = _gdn_call(p, 0, n_ctx, t_ctx, dn_p, None)
        y_dn_l, _ = _gdn_call(p, lat_base, n_lat, t_lat, dn_p, _pair_state_in(state_delta[:, l]))
        y_gla_c, s_gla = _gla_call(p, 0, n_ctx, t_ctx, gla_p, None)
        y_gla_l, _ = _gla_call(p, lat_base, n_lat, t_lat, gla_p, _gla_state_in(state_gla[:, l]))
        rw_states.append(_pair_state_out(s_rw))
        dn_states.append(_pair_state_out(s_dn))
        gla_states.append(_gla_state_out(s_gla))

        wr = jnp.zeros((d, LANES), F32)
        wr = wr.at[:, :N_GROUPS].set(moe_wg[l]).at[:, N_GROUPS:N_GROUPS + N_EXPERTS].set(moe_we[l].reshape(d, N_EXPERTS))
        br = jnp.zeros((1, LANES), F32)
        br = br.at[0, :N_GROUPS].set(moe_bg[l]).at[0, N_GROUPS:N_GROUPS + N_EXPERTS].set(moe_be[l].reshape(N_EXPERTS))
        x1, h2_rows, grp = _out_call(jnp.concatenate([y_rw_c, y_rw_l]), jnp.concatenate([y_gla_c, y_gla_l]),
                                     jnp.concatenate([y_dn_c, y_dn_l]), x, mod3, l, norm2_w[l],
                                     w_out_all, wr, br, n_ctx_tok, t_lat)
        n_tok = n_ctx_tok + n_lat_tok
        slot, tile_group, n_tiles, max_tiles = _route_slots(grp, n_tok)
        src_row, dst_row = _row_maps(slot, n_tok, max_tiles * MOE_TILE)
        y_rows = _moe_group_call(tile_group, n_tiles, max_tiles, src_row, dst_row, h2_rows, l, wr, br,
                                 w_gate, w_up, w_down)
        x = _combine_call(y_rows, x1, mod3, l, n_ctx_tok, t_lat)

    y_prompt = _norm_call(x, final_norm_w, 0, n_ctx_tok // 256).reshape(n_ctx, t_ctx, d)
    y_sample = _norm_call(x, final_norm_w, n_ctx_tok // 256, n_lat_tok // 256).reshape(n_lat, t_lat, d)
    new_rw = jnp.stack(rw_states, axis=1).astype(x_prompt.dtype)
    new_gla = jnp.stack(gla_states, axis=1).astype(x_prompt.dtype)
    new_dn = jnp.stack(dn_states, axis=1).astype(x_prompt.dtype)
    return (y_prompt, y_sample, new_rw, new_gla, new_dn)
```
